```python
import jax, jax.numpy as jnp
from jax import lax
import numpy as np

D_MODEL = 1024
BATCH = 32
SEQ = 256
DEPTH = 2
DEC_BATCH = 8
DEC_SEQ = 2048
PAST_LEN = 256

GRID_W = 64
HEAD_DIM = 64
BLK = 128
ROPE_THETA = 10000.0
EPS = 1e-6
N_EVEN = (DEPTH + 1) // 2
N_ODD = DEPTH // 2
HA = D_MODEL // 128
DK = HEAD_DIM
CONV_W = 5
DN_CHUNK = 64
HB = D_MODEL // 128
HKV_B = HB // 4
WINDOW = 128
HC = (3 * D_MODEL // 4) // HEAD_DIM
HKV_C = HC // 3
D_S5 = D_MODEL // 4
S5_CH = 16
G_D = D_S5 // S5_CH
S5_P = 64
N_GROUPS = 4
EXP_PER_GROUP = 8
N_EXP = N_GROUPS * EXP_PER_GROUP
EXPERT_FF = D_MODEL // 8
TOP_K = 2

EVEN_SPLITS = (3 * HA * DK, HA * DK, 2 * HA, 2 * HA, HB * HEAD_DIM, HKV_B * HEAD_DIM, HKV_B * HEAD_DIM)
D_IN_EVEN = sum(EVEN_SPLITS)
D_MIX_EVEN = HA * DK + HB * HEAD_DIM
ODD_SPLITS = (HC * HEAD_DIM, HKV_C * HEAD_DIM, HKV_C * HEAD_DIM, D_S5)
D_IN_ODD = sum(ODD_SPLITS)
D_MIX_ODD = HC * HEAD_DIM + D_S5

kernel_name = 'hybrid_prefix_diffusion_trunk_step'

F32 = jnp.float32


def split_cols(x, sizes):
    idx = np.cumsum(sizes)[:-1].tolist()
    return jnp.split(x, idx, axis=-1)


def rms_norm(x, w):
    x32 = x.astype(F32)
    y = x32 * lax.rsqrt(jnp.mean(x32 * x32, axis=-1, keepdims=True) + EPS)
    return (y * w.astype(F32)).astype(x.dtype)


def l2norm(x):
    return x * lax.rsqrt(jnp.sum(x * x, axis=-1, keepdims=True) + EPS)


def ada_modulation(cvec, w, b):
    m = jax.nn.silu(cvec) @ w + b
    return [t[:, None, :] for t in jnp.split(m, 6, axis=-1)]


def modulate(x, w, shift, scale):
    return rms_norm(x, w) * (1 + scale) + shift


def axial_rope_tables(rows):
    row = jnp.repeat(jnp.arange(rows), GRID_W).astype(F32)
    col = jnp.tile(jnp.arange(GRID_W), rows).astype(F32)
    quarter = HEAD_DIM // 4
    freqs = ROPE_THETA ** (-jnp.arange(quarter, dtype=F32) / quarter)
    ang_r = row[:, None] * freqs
    ang_c = col[:, None] * freqs
    ang = jnp.concatenate([ang_r, ang_r, ang_c, ang_c], axis=-1)
    return jnp.cos(ang), jnp.sin(ang)


def apply_axial_rope(x, cos, sin):
    q4 = HEAD_DIM // 4
    half = HEAD_DIM // 2

    def rot(u):
        return jnp.concatenate([-u[..., q4:], u[..., :q4]], axis=-1)

    xrot = jnp.concatenate([rot(x[..., :half]), rot(x[..., half:])], axis=-1)
    return (x * cos[:, None, :] + xrot * sin[:, None, :]).astype(x.dtype)


def short_conv(x, w):
    pad = CONV_W // 2
    y = lax.conv_general_dilated(x, w[:, None, :].astype(x.dtype), window_strides=(1,), padding=[(pad, pad)],
                                 dimension_numbers=('NWC', 'WIO', 'NWC'), feature_group_count=x.shape[-1])
    return jax.nn.silu(y)


def gated_delta_chunked(q, k, v, g, beta, s0):
    b, l, h, d = q.shape
    n = l // DN_CHUNK

    def chunks(t):
        t = t.reshape((b, n, DN_CHUNK, h) + t.shape[3:])
        return jnp.moveaxis(t, (1, 3), (0, 2))

    q = chunks(q * (d ** -0.5))
    k = chunks(k)
    v = chunks(v)
    gc = jnp.cumsum(chunks(g), axis=-1)
    beta = chunks(beta)
    tri = jnp.tril(jnp.ones((DN_CHUNK, DN_CHUNK), bool))
    strict = jnp.tril(jnp.ones((DN_CHUNK, DN_CHUNK), bool), -1)
    decay = jnp.exp(jnp.where(tri, gc[..., :, None] - gc[..., None, :], -jnp.inf))
    kb = k * beta[..., None]
    m = jnp.where(strict, jnp.einsum('nbhid,nbhjd->nbhij', kb, k) * decay, 0.0)
    eye = jnp.eye(DN_CHUNK, dtype=F32)
    rhs = jnp.concatenate([v * beta[..., None], kb * jnp.exp(gc)[..., None]], axis=-1)
    sol = lax.linalg.triangular_solve(eye + m, rhs, left_side=True, lower=True, unit_diagonal=True)
    u, w = sol[..., :d], sol[..., d:]

    def step(s, inp):
        qi, ki, ui, wi, gci, di = inp
        attn = jnp.einsum('bhid,bhjd->bhij', qi, ki) * di
        v_new = ui - jnp.einsum('bhcd,bhde->bhce', wi, s)
        o = (jnp.einsum('bhcd,bhde->bhce', qi * jnp.exp(gci)[..., None], s)
             + jnp.einsum('bhij,bhje->bhie', attn, v_new))
        g_last = gci[..., -1]
        s = (s * jnp.exp(g_last)[..., None, None]
             + jnp.einsum('bhcd,bhce->bhde', ki * jnp.exp(g_last[..., None] - gci)[..., None], v_new))
        return s, o

    s_fin, o = lax.scan(step, s0, (q, k, u, w, gc, decay))
    o = jnp.moveaxis(o, (0, 2), (1, 3)).reshape(b, l, h, d)
    return o, s_fin


def deltanet_inputs(qkv, a, bb, ep):
    b, l, _ = qkv.shape
    qkv = short_conv(qkv, ep['conv_w']).astype(F32).reshape(b, l, 3, HA, DK)
    q = l2norm(qkv[:, :, 0])
    k = l2norm(qkv[:, :, 1])
    v = qkv[:, :, 2]
    a = a.astype(F32).reshape(b, l, 2, HA)
    bb = bb.astype(F32).reshape(b, l, 2, HA)
    g = -jnp.exp(ep['a_log'].astype(F32)) * jax.nn.softplus(a + ep['dt_bias'].astype(F32))
    beta = jax.nn.sigmoid(bb)
    return q, k, v, g, beta


def deltanet_bidir(q, k, v, g, beta, s0_f, s0_b):
    o_f, s_f = gated_delta_chunked(q, k, v, g[:, :, 0], beta[:, :, 0], s0_f)
    fl = lambda t: jnp.flip(t, axis=1)
    o_b, s_b = gated_delta_chunked(fl(q), fl(k), fl(v), fl(g[:, :, 1]), fl(beta[:, :, 1]), s0_b)
    return o_f + fl(o_b), s_f, s_b


def deltanet_out(o, z, norm_w):
    b, l = o.shape[:2]
    o = o * lax.rsqrt(jnp.mean(o * o, axis=-1, keepdims=True) + EPS) * norm_w.astype(F32)
    return (o.reshape(b, l, HA * DK) * jax.nn.silu(z.astype(F32))).astype(z.dtype)


def blocked_attention(q, k, v, sink=None):
    b, lq, h, d = q.shape
    hk = k.shape[2]
    grp = h // hk
    nb = lq // BLK
    qb = jnp.moveaxis(q.reshape(b, nb, BLK, hk, grp, d), 1, 0)
    scale = d ** -0.5

    def one_block(qblk):
        s = jnp.einsum('bqhgd,bkhd->bhgqk', qblk, k).astype(F32) * scale
        if sink is not None:
            s_sink = jnp.broadcast_to(sink.astype(F32).reshape(1, hk, grp, 1, 1), s.shape[:-1] + (1,))
            p = jax.nn.softmax(jnp.concatenate([s, s_sink], axis=-1), axis=-1)[..., :-1]
        else:
            p = jax.nn.softmax(s, axis=-1)
        return jnp.einsum('bhgqk,bkhd->bqhgd', p.astype(v.dtype), v)

    o = lax.map(one_block, qb)
    return jnp.moveaxis(o, 0, 1).reshape(b, lq, h * d)


def windowed_attention(q, k, v, kc, vc, sink):
    b, n, h, d = q.shape
    hk = k.shape[2]
    grp = h // hk
    nb = n // BLK
    pad = ((0, 0), (BLK, BLK), (0, 0), (0, 0))

    def windows(t):
        t = jnp.pad(t, pad).reshape(b, nb + 2, BLK, hk, d)
        return jnp.concatenate([t[:, :-2], t[:, 1:-1], t[:, 2:]], axis=2)

    kw, vw = windows(k), windows(v)
    qb = q.reshape(b, nb, BLK, hk, grp, d)
    scale = d ** -0.5
    s_loc = jnp.einsum('bnqhgd,bnkhd->bnhgqk', qb, kw.astype(q.dtype)).astype(F32) * scale
    qi = jnp.arange(BLK)[:, None] + BLK
    kj = jnp.arange(3 * BLK)[None, :]
    kabs = (jnp.arange(nb)[:, None, None] - 1) * BLK + kj
    mask = (jnp.abs(qi - kj) <= WINDOW)[None] & (kabs >= 0) & (kabs < n)
    s_loc = jnp.where(mask[None, :, None, None], s_loc, -jnp.inf)
    s_ctx = jnp.einsum('bnqhgd,bkhd->bnhgqk', qb, kc.astype(q.dtype)).astype(F32) * scale
    s_sink = jnp.broadcast_to(sink.astype(F32).reshape(1, 1, hk, grp, 1, 1), s_loc.shape[:-1] + (1,))
    p = jax.nn.softmax(jnp.concatenate([s_loc, s_ctx, s_sink], axis=-1), axis=-1)
    p_loc = p[..., :3 * BLK].astype(v.dtype)
    p_ctx = p[..., 3 * BLK:-1].astype(v.dtype)
    o = (jnp.einsum('bnhgqk,bnkhd->bnqhgd', p_loc, vw)
         + jnp.einsum('bnhgqk,bkhd->bnqhgd', p_ctx, vc.astype(v.dtype)))
    return o.reshape(b, n, h * d)


def _ssm_combine(e1, e2):
    a1, b1 = e1
    a2, b2 = e2
    return a2 * a1, a2 * b1 + b2


def s5_bidir(u, op, s0):
    b, l, _ = u.shape
    u32 = u.astype(F32).reshape(b, l, G_D, S5_CH)
    lam = lax.complex(op['lam_re'].astype(F32), op['lam_im'].astype(F32))
    dt = jnp.exp(op['log_dt'].astype(F32))[..., None]
    a_bar = jnp.exp(lam * dt)
    bmat = lax.complex(op['b_re'].astype(F32), op['b_im'].astype(F32))
    b_bar = ((a_bar - 1) / lam)[..., None] * bmat
    cmat = lax.complex(op['c_re'].astype(F32), op['c_im'].astype(F32))

    def scan_dir(dr, useq, s_init):
        bu = jnp.einsum('gpc,blgc->lbgp', b_bar[dr], useq.astype(jnp.complex64))
        bu = bu.at[0].add(a_bar[dr] * s_init)
        a_seq = jnp.broadcast_to(a_bar[dr], (l, 1, G_D, S5_P))
        _, xs = lax.associative_scan(_ssm_combine, (a_seq, bu), axis=0)
        y = jnp.einsum('gcp,lbgp->blgc', cmat[dr], xs).real
        return y, xs[-1]

    y_f, s_f = scan_dir(0, u32, s0[:, 0])
    y_b, s_b = scan_dir(1, jnp.flip(u32, axis=1), s0[:, 1])
    y = y_f + jnp.flip(y_b, axis=1) + u32 * op['d'].astype(F32).reshape(G_D, S5_CH)
    y = jax.nn.gelu(y.reshape(b, l, D_S5))
    y = y * jax.nn.sigmoid(y @ op['glu_w'].astype(F32) + op['glu_b'].astype(F32))
    return y.astype(u.dtype), jnp.stack([s_f, s_b], axis=1)


def even_mixer_context(h, ep):
    b, l, _ = h.shape
    qkv, z, a, bb, bq, bk, bv = split_cols(h @ ep['w_in'], EVEN_SPLITS)
    q, k, v, g, beta = deltanet_inputs(qkv, a, bb, ep)
    zero = jnp.zeros((b, HA, DK, DK), F32)
    o, s_f, s_b = deltanet_bidir(q, k, v, g, beta, zero, zero)
    y_a = deltanet_out(o, z, ep['norm_w'])
    bq = bq.reshape(b, l, HB, HEAD_DIM)
    bk = bk.reshape(b, l, HKV_B, HEAD_DIM)
    bv = bv.reshape(b, l, HKV_B, HEAD_DIM)
    y_b = blocked_attention(bq, bk, bv, ep['sink'])
    y = jnp.concatenate([y_a, y_b.astype(y_a.dtype)], axis=-1) @ ep['w_out']
    return y, jnp.stack([s_f, s_b], axis=1), bk, bv


def even_mixer_latent(h, ep, dn_state, ctx_k, ctx_v, cos, sin):
    b, n, _ = h.shape
    qkv, z, a, bb, bq, bk, bv = split_cols(h @ ep['w_in'], EVEN_SPLITS)
    q, k, v, g, beta = deltanet_inputs(qkv, a, bb, ep)
    o, _, _ = deltanet_bidir(q, k, v, g, beta, dn_state[:, 0].astype(F32), dn_state[:, 1].astype(F32))
    y_a = deltanet_out(o, z, ep['norm_w'])
    bq = apply_axial_rope(bq.reshape(b, n, HB, HEAD_DIM), cos, sin)
    bk = apply_axial_rope(bk.reshape(b, n, HKV_B, HEAD_DIM), cos, sin)
    bv = bv.reshape(b, n, HKV_B, HEAD_DIM)
    y_b = windowed_attention(bq, bk, bv, ctx_k, ctx_v, ep['sink'])
    return jnp.concatenate([y_a, y_b.astype(y_a.dtype)], axis=-1) @ ep['w_out']


def odd_mixer_context(h, op):
    b, l, _ = h.shape
    cq, ck, cv, u = split_cols(h @ op['w_in'], ODD_SPLITS)
    cq = rms_norm(cq.reshape(b, l, HC, HEAD_DIM), op['q_norm'])
    ck = rms_norm(ck.reshape(b, l, HKV_C, HEAD_DIM), op['k_norm'])
    cv = cv.reshape(b, l, HKV_C, HEAD_DIM)
    y_c = blocked_attention(cq, ck, cv)
    zero = jnp.zeros((b, 2, G_D, S5_P), jnp.complex64)
    y_d, s_fin = s5_bidir(u, op, zero)
    y = jnp.concatenate([y_c, y_d.astype(y_c.dtype)], axis=-1) @ op['w_out']
    return y, ck, cv, jnp.stack([s_fin.real, s_fin.imag], axis=-1)


def odd_mixer_latent(h, op, ctx_k, ctx_v, s5_state, cos, sin):
    b, n, _ = h.shape
    cq, ck, cv, u = split_cols(h @ op['w_in'], ODD_SPLITS)
    cq = apply_axial_rope(rms_norm(cq.reshape(b, n, HC, HEAD_DIM), op['q_norm']), cos, sin)
    ck = apply_axial_rope(rms_norm(ck.reshape(b, n, HKV_C, HEAD_DIM), op['k_norm']), cos, sin)
    cv = cv.reshape(b, n, HKV_C, HEAD_DIM)
    keys = jnp.concatenate([ck, ctx_k.astype(ck.dtype)], axis=1)
    vals = jnp.concatenate([cv, ctx_v.astype(cv.dtype)], axis=1)
    y_c = blocked_attention(cq, keys, vals)
    s0 = lax.complex(s5_state[..., 0].astype(F32), s5_state[..., 1].astype(F32))
    y_d, _ = s5_bidir(u, op, s0)
    return jnp.concatenate([y_c, y_d.astype(y_c.dtype)], axis=-1) @ op['w_out']


def hier_moe(x, mp):
    shp = x.shape
    t = x.reshape(-1, D_MODEL)
    lg = (t @ mp['rg'] + mp['rg_b']).astype(F32)
    pg = jax.nn.softmax(lg, axis=-1)
    gidx = jnp.argmax(lg, axis=-1)
    gate_g = jnp.take_along_axis(pg, gidx[:, None], axis=-1)
    le = (t @ mp['re'] + mp['re_b']).astype(F32).reshape(-1, N_GROUPS, EXP_PER_GROUP)
    le = jnp.take_along_axis(le, gidx[:, None, None], axis=1)[:, 0]
    top_v, top_i = lax.top_k(le, TOP_K)
    pe = jax.nn.softmax(top_v, axis=-1) * gate_g
    eid = gidx[:, None] * EXP_PER_GROUP + top_i
    combine = jnp.sum(jax.nn.one_hot(eid, N_EXP, dtype=F32) * pe[..., None], axis=1)
    hid = jax.nn.silu(t @ mp['wg']) * (t @ mp['wu'])
    hid = hid.reshape(-1, N_EXP, EXPERT_FF) * combine[..., None].astype(t.dtype)
    y = hid.reshape(-1, N_EXP * EXPERT_FF) @ mp['wd']
    return y.reshape(shp)


def setup_inputs(seed: int = 0) -> dict:
    key = jax.random.key(seed)
    ks = iter(jax.random.split(key, 64))

    def nrm(shape, scale):
        return scale * jax.random.normal(next(ks), shape, F32)

    def unif(shape, lo, hi):
        return jax.random.uniform(next(ks), shape, F32, lo, hi)

    dsc = D_MODEL ** -0.5
    dt_dn = jnp.exp(unif((N_EVEN, 2, HA), float(np.log(1e-3)), float(np.log(1e-1))))
    lam_im = jnp.broadcast_to(np.pi * jnp.arange(S5_P, dtype=F32), (N_ODD, 2, G_D, S5_P))
    return {
        'x_prompt': nrm((BATCH, SEQ, D_MODEL), 1.0),
        'x_sample': nrm((DEC_BATCH, DEC_SEQ, D_MODEL), 1.0),
        'c': nrm((DEC_BATCH, D_MODEL), 1.0),
        'c_ctx': nrm((D_MODEL,), 1.0),
        'state_dn': nrm((DEC_BATCH, N_EVEN, 2, HA, DK, DK), 0.1),
        'cache_b_k': nrm((DEC_BATCH, N_EVEN, PAST_LEN, HKV_B, HEAD_DIM), 1.0),
        'cache_b_v': nrm((DEC_BATCH, N_EVEN, PAST_LEN, HKV_B, HEAD_DIM), 1.0),
        'cache_c_k': nrm((DEC_BATCH, N_ODD, PAST_LEN, HKV_C, HEAD_DIM), 1.0),
        'cache_c_v': nrm((DEC_BATCH, N_ODD, PAST_LEN, HKV_C, HEAD_DIM), 1.0),
        'state_s5': nrm((DEC_BATCH, N_ODD, 2, G_D, S5_P, 2), 0.5),
        'ada_w': nrm((DEPTH, D_MODEL, 6 * D_MODEL), dsc),
        'ada_b': nrm((DEPTH, 6 * D_MODEL), 0.02),
        'norm_w': 1.0 + nrm((DEPTH, 2, D_MODEL), 0.05),
        'w_in_e': nrm((N_EVEN, D_MODEL, D_IN_EVEN), dsc),
        'dn_conv_w': nrm((N_EVEN, CONV_W, 3 * HA * DK), CONV_W ** -0.5),
        'dn_a_log': jnp.log(unif((N_EVEN, 2, HA), 1.0, 16.0)),
        'dn_dt_bias': dt_dn + jnp.log(-jnp.expm1(-dt_dn)),
        'dn_norm_w': 1.0 + nrm((N_EVEN, DK), 0.05),
        'b_sink': nrm((N_EVEN, HB), 0.5),
        'w_out_e': nrm((N_EVEN, D_MIX_EVEN, D_MODEL), D_MIX_EVEN ** -0.5),
        'w_in_o': nrm((N_ODD, D_MODEL, D_IN_ODD), dsc),
        'c_q_norm': 1.0 + nrm((N_ODD, HEAD_DIM), 0.05),
        'c_k_norm': 1.0 + nrm((N_ODD, HEAD_DIM), 0.05),
        's5_lam_re': -0.5 + nrm((N_ODD, 2, G_D, S5_P), 0.01),
        's5_lam_im': lam_im + nrm((N_ODD, 2, G_D, S5_P), 0.01),
        's5_log_dt': unif((N_ODD, 2, G_D), float(np.log(1e-3)), float(np.log(1e-1))),
        's5_b_re': nrm((N_ODD, 2, G_D, S5_P, S5_CH), (2 * S5_CH) ** -0.5),
        's5_b_im': nrm((N_ODD, 2, G_D, S5_P, S5_CH), (2 * S5_CH) ** -0.5),
        's5_c_re': nrm((N_ODD, 2, G_D, S5_CH, S5_P), S5_P ** -0.5),
        's5_c_im': nrm((N_ODD, 2, G_D, S5_CH, S5_P), S5_P ** -0.5),
        's5_d': nrm((N_ODD, D_S5), 0.5),
        's5_glu_w': nrm((N_ODD, D_S5, D_S5), D_S5 ** -0.5),
        's5_glu_b': nrm((N_ODD, D_S5), 0.02),
        'w_out_o': nrm((N_ODD, D_MIX_ODD, D_MODEL), D_MIX_ODD ** -0.5),
        'moe_rg': nrm((DEPTH, D_MODEL, N_GROUPS), dsc),
        'moe_rg_b': nrm((DEPTH, N_GROUPS), 0.01),
        'moe_re': nrm((DEPTH, D_MODEL, N_EXP), dsc),
        'moe_re_b': nrm((DEPTH, N_EXP), 0.01),
        'moe_wg': nrm((DEPTH, D_MODEL, N_EXP * EXPERT_FF), dsc),
        'moe_wu': nrm((DEPTH, D_MODEL, N_EXP * EXPERT_FF), dsc),
        'moe_wd': nrm((DEPTH, N_EXP * EXPERT_FF, D_MODEL), EXPERT_FF ** -0.5),
        'final_norm_w': 1.0 + nrm((D_MODEL,), 0.05),
    }


def reference(x_prompt, x_sample, c, c_ctx, state_dn, cache_b_k, cache_b_v, cache_c_k, cache_c_v, state_s5,
              ada_w, ada_b, norm_w, w_in_e, dn_conv_w, dn_a_log, dn_dt_bias, dn_norm_w, b_sink, w_out_e,
              w_in_o, c_q_norm, c_k_norm, s5_lam_re, s5_lam_im, s5_log_dt, s5_b_re, s5_b_im, s5_c_re, s5_c_im,
              s5_d, s5_glu_w, s5_glu_b, w_out_o, moe_rg, moe_rg_b, moe_re, moe_re_b, moe_wg, moe_wu, moe_wd,
              final_norm_w):
    n_lat = x_sample.shape[1]
    rows = n_lat // GRID_W
    cos, sin = axial_rope_tables(rows)
    xp, xs = x_prompt, x_sample
    new_dn, new_bk, new_bv, new_ck, new_cv, new_s5 = [], [], [], [], [], []
    for layer in range(DEPTH):
        mod_p = ada_modulation(c_ctx[None, :], ada_w[layer], ada_b[layer])
        mod_s = ada_modulation(c, ada_w[layer], ada_b[layer])
        hp = modulate(xp, norm_w[layer, 0], mod_p[0], mod_p[1])
        hs = modulate(xs, norm_w[layer, 0], mod_s[0], mod_s[1])
        if layer % 2 == 0:
            i = layer // 2
            ep = {'w_in': w_in_e[i], 'conv_w': dn_conv_w[i], 'a_log': dn_a_log[i], 'dt_bias': dn_dt_bias[i],
                  'norm_w': dn_norm_w[i], 'sink': b_sink[i], 'w_out': w_out_e[i]}
            yp, dn, bk, bv = even_mixer_context(hp, ep)
            ys = even_mixer_latent(hs, ep, state_dn[:, i], cache_b_k[:, i], cache_b_v[:, i], cos, sin)
            new_dn.append(dn)
            new_bk.append(bk)
            new_bv.append(bv)
        else:
            i = layer // 2
            op = {'w_in': w_in_o[i], 'q_norm': c_q_norm[i], 'k_norm': c_k_norm[i], 'lam_re': s5_lam_re[i],
                  'lam_im': s5_lam_im[i], 'log_dt': s5_log_dt[i], 'b_re': s5_b_re[i], 'b_im': s5_b_im[i],
                  'c_re': s5_c_re[i], 'c_im': s5_c_im[i], 'd': s5_d[i], 'glu_w': s5_glu_w[i],
                  'glu_b': s5_glu_b[i], 'w_out': w_out_o[i]}
            yp, ck, cv, s5 = odd_mixer_context(hp, op)
            ys = odd_mixer_latent(hs, op, cache_c_k[:, i], cache_c_v[:, i], state_s5[:, i], cos, sin)
            new_ck.append(ck)
            new_cv.append(cv)
            new_s5.append(s5)
        xp = xp + mod_p[2] * yp
        xs = xs + mod_s[2] * ys
        mp = {'rg': moe_rg[layer], 'rg_b': moe_rg_b[layer], 're': moe_re[layer], 're_b': moe_re_b[layer],
              'wg': moe_wg[layer], 'wu': moe_wu[layer], 'wd': moe_wd[layer]}
        xp = xp + mod_p[5] * hier_moe(modulate(xp, norm_w[layer, 1], mod_p[3], mod_p[4]), mp)
        xs = xs + mod_s[5] * hier_moe(modulate(xs, norm_w[layer, 1], mod_s[3], mod_s[4]), mp)
    y_prompt = rms_norm(xp, final_norm_w)
    y_sample = rms_norm(xs, final_norm_w)
    return (y_prompt, y_sample, jnp.stack(new_dn, axis=1), jnp.stack(new_bk, axis=1), jnp.stack(new_bv, axis=1),
            jnp.stack(new_ck, axis=1), jnp.stack(new_cv, axis=1), jnp.stack(new_s5, axis=1))
```

```python
import functools

import numpy as np
import jax
import jax.numpy as jnp
from jax import lax
from jax.experimental import pallas as pl
from jax.experimental.pallas import tpu as pltpu

D_MODEL = 1024
BATCH = 32
SEQ = 256
DEPTH = 2
DEC_BATCH = 8
DEC_SEQ = 2048
PAST_LEN = 256

GRID_W = 64
HEAD_DIM = 64
BLK = 128
ROPE_THETA = 10000.0
EPS = 1e-6
N_EVEN = (DEPTH + 1) // 2
N_ODD = DEPTH // 2
HA = D_MODEL // 128
DK = HEAD_DIM
CONV_W = 5
DN_CHUNK = 64
HB = D_MODEL // 128
HKV_B = HB // 4
WINDOW = 128
HC = (3 * D_MODEL // 4) // HEAD_DIM
HKV_C = HC // 3
D_S5 = D_MODEL // 4
S5_CH = 16
G_D = D_S5 // S5_CH
S5_P = 64
N_GROUPS = 4
EXP_PER_GROUP = 8
N_EXP = N_GROUPS * EXP_PER_GROUP
EXPERT_FF = D_MODEL // 8
TOP_K = 2

EVEN_SPLITS = (3 * HA * DK, HA * DK, 2 * HA, 2 * HA, HB * HEAD_DIM, HKV_B * HEAD_DIM, HKV_B * HEAD_DIM)
D_IN_EVEN = sum(EVEN_SPLITS)
D_MIX_EVEN = HA * DK + HB * HEAD_DIM
ODD_SPLITS = (HC * HEAD_DIM, HKV_C * HEAD_DIM, HKV_C * HEAD_DIM, D_S5)
D_IN_ODD = sum(ODD_SPLITS)
D_MIX_ODD = HC * HEAD_DIM + D_S5

F32 = jnp.float32


def split_cols(x, sizes):
    idx = np.cumsum(sizes)[:-1].tolist()
    return jnp.split(x, idx, axis=-1)


def rms_norm(x, w):
    x32 = x.astype(F32)
    y = x32 * lax.rsqrt(jnp.mean(x32 * x32, axis=-1, keepdims=True) + EPS)
    return (y * w.astype(F32)).astype(x.dtype)


def l2norm(x):
    return x * lax.rsqrt(jnp.sum(x * x, axis=-1, keepdims=True) + EPS)


def ada_modulation(cvec, w, b):
    m = jax.nn.silu(cvec) @ w + b
    return [t[:, None, :] for t in jnp.split(m, 6, axis=-1)]


def modulate(x, w, shift, scale):
    return rms_norm(x, w) * (1 + scale) + shift


def axial_rope_tables(rows):
    row = jnp.repeat(jnp.arange(rows), GRID_W).astype(F32)
    col = jnp.tile(jnp.arange(GRID_W), rows).astype(F32)
    quarter = HEAD_DIM // 4
    freqs = ROPE_THETA ** (-jnp.arange(quarter, dtype=F32) / quarter)
    ang_r = row[:, None] * freqs
    ang_c = col[:, None] * freqs
    ang = jnp.concatenate([ang_r, ang_r, ang_c, ang_c], axis=-1)
    return jnp.cos(ang), jnp.sin(ang)


def apply_axial_rope(x, cos, sin):
    q4 = HEAD_DIM // 4
    half = HEAD_DIM // 2

    def rot(u):
        return jnp.concatenate([-u[..., q4:], u[..., :q4]], axis=-1)

    xrot = jnp.concatenate([rot(x[..., :half]), rot(x[..., half:])], axis=-1)
    return (x * cos[:, None, :] + xrot * sin[:, None, :]).astype(x.dtype)


def short_conv(x, w):
    pad = CONV_W // 2
    y = lax.conv_general_dilated(x, w[:, None, :].astype(x.dtype), window_strides=(1,), padding=[(pad, pad)],
                                 dimension_numbers=('NWC', 'WIO', 'NWC'), feature_group_count=x.shape[-1])
    return jax.nn.silu(y)


def gated_delta_chunked(q, k, v, g, beta, s0):
    b, l, h, d = q.shape
    n = l // DN_CHUNK

    def chunks(t):
        t = t.reshape((b, n, DN_CHUNK, h) + t.shape[3:])
        return jnp.moveaxis(t, (1, 3), (0, 2))

    q = chunks(q * (d ** -0.5))
    k = chunks(k)
    v = chunks(v)
    gc = jnp.cumsum(chunks(g), axis=-1)
    beta = chunks(beta)
    tri = jnp.tril(jnp.ones((DN_CHUNK, DN_CHUNK), bool))
    strict = jnp.tril(jnp.ones((DN_CHUNK, DN_CHUNK), bool), -1)
    decay = jnp.exp(jnp.where(tri, gc[..., :, None] - gc[..., None, :], -jnp.inf))
    kb = k * beta[..., None]
    m = jnp.where(strict, jnp.einsum('nbhid,nbhjd->nbhij', kb, k) * decay, 0.0)
    eye = jnp.eye(DN_CHUNK, dtype=F32)
    rhs = jnp.concatenate([v * beta[..., None], kb * jnp.exp(gc)[..., None]], axis=-1)
    sol = lax.linalg.triangular_solve(eye + m, rhs, left_side=True, lower=True, unit_diagonal=True)
    u, w = sol[..., :d], sol[..., d:]

    def step(s, inp):
        qi, ki, ui, wi, gci, di = inp
        attn = jnp.einsum('bhid,bhjd->bhij', qi, ki) * di
        v_new = ui - jnp.einsum('bhcd,bhde->bhce', wi, s)
        o = (jnp.einsum('bhcd,bhde->bhce', qi * jnp.exp(gci)[..., None], s)
             + jnp.einsum('bhij,bhje->bhie', attn, v_new))
        g_last = gci[..., -1]
        s = (s * jnp.exp(g_last)[..., None, None]
             + jnp.einsum('bhcd,bhce->bhde', ki * jnp.exp(g_last[..., None] - gci)[..., None], v_new))
        return s, o

    s_fin, o = lax.scan(step, s0, (q, k, u, w, gc, decay))
    o = jnp.moveaxis(o, (0, 2), (1, 3)).reshape(b, l, h, d)
    return o, s_fin


def deltanet_inputs(qkv, a, bb, ep):
    b, l, _ = qkv.shape
    qkv = short_conv(qkv, ep['conv_w']).astype(F32).reshape(b, l, 3, HA, DK)
    q = l2norm(qkv[:, :, 0])
    k = l2norm(qkv[:, :, 1])
    v = qkv[:, :, 2]
    a = a.astype(F32).reshape(b, l, 2, HA)
    bb = bb.astype(F32).reshape(b, l, 2, HA)
    g = -jnp.exp(ep['a_log'].astype(F32)) * jax.nn.softplus(a + ep['dt_bias'].astype(F32))
    beta = jax.nn.sigmoid(bb)
    return q, k, v, g, beta


def deltanet_bidir(q, k, v, g, beta, s0_f, s0_b):
    o_f, s_f = gated_delta_chunked(q, k, v, g[:, :, 0], beta[:, :, 0], s0_f)
    fl = lambda t: jnp.flip(t, axis=1)
    o_b, s_b = gated_delta_chunked(fl(q), fl(k), fl(v), fl(g[:, :, 1]), fl(beta[:, :, 1]), s0_b)
    return o_f + fl(o_b), s_f, s_b


def deltanet_out(o, z, norm_w):
    b, l = o.shape[:2]
    o = o * lax.rsqrt(jnp.mean(o * o, axis=-1, keepdims=True) + EPS) * norm_w.astype(F32)
    return (o.reshape(b, l, HA * DK) * jax.nn.silu(z.astype(F32))).astype(z.dtype)


def blocked_attention(q, k, v, sink=None):
    b, lq, h, d = q.shape
    hk = k.shape[2]
    grp = h // hk
    nb = lq // BLK
    qb = jnp.moveaxis(q.reshape(b, nb, BLK, hk, grp, d), 1, 0)
    scale = d ** -0.5

    def one_block(qblk):
        s = jnp.einsum('bqhgd,bkhd->bhgqk', qblk, k).astype(F32) * scale
        if sink is not None:
            s_sink = jnp.broadcast_to(sink.astype(F32).reshape(1, hk, grp, 1, 1), s.shape[:-1] + (1,))
            p = jax.nn.softmax(jnp.concatenate([s, s_sink], axis=-1), axis=-1)[..., :-1]
        else:
            p = jax.nn.softmax(s, axis=-1)
        return jnp.einsum('bhgqk,bkhd->bqhgd', p.astype(v.dtype), v)

    o = lax.map(one_block, qb)
    return jnp.moveaxis(o, 0, 1).reshape(b, lq, h * d)


def windowed_attention(q, k, v, kc, vc, sink):
    b, n, h, d = q.shape
    hk = k.shape[2]
    grp = h // hk
    nb = n // BLK
    pad = ((0, 0), (BLK, BLK), (0, 0), (0, 0))

    def windows(t):
        t = jnp.pad(t, pad).reshape(b, nb + 2, BLK, hk, d)
        return jnp.concatenate([t[:, :-2], t[:, 1:-1], t[:, 2:]], axis=2)

    kw, vw = windows(k), windows(v)
    qb = q.reshape(b, nb, BLK, hk, grp, d)
    scale = d ** -0.5
    s_loc = jnp.einsum('bnqhgd,bnkhd->bnhgqk', qb, kw.astype(q.dtype)).astype(F32) * scale
    qi = jnp.arange(BLK)[:, None] + BLK
    kj = jnp.arange(3 * BLK)[None, :]
    kabs = (jnp.arange(nb)[:, None, None] - 1) * BLK + kj
    mask = (jnp.abs(qi - kj) <= WINDOW)[None] & (kabs >= 0) & (kabs < n)
    s_loc = jnp.where(mask[None, :, None, None], s_loc, -jnp.inf)
    s_ctx = jnp.einsum('bnqhgd,bkhd->bnhgqk', qb, kc.astype(q.dtype)).astype(F32) * scale
    s_sink = jnp.broadcast_to(sink.astype(F32).reshape(1, 1, hk, grp, 1, 1), s_loc.shape[:-1] + (1,))
    p = jax.nn.softmax(jnp.concatenate([s_loc, s_ctx, s_sink], axis=-1), axis=-1)
    p_loc = p[..., :3 * BLK].astype(v.dtype)
    p_ctx = p[..., 3 * BLK:-1].astype(v.dtype)
    o = (jnp.einsum('bnhgqk,bnkhd->bnqhgd', p_loc, vw)
         + jnp.einsum('bnhgqk,bkhd->bnqhgd', p_ctx, vc.astype(v.dtype)))
    return o.reshape(b, n, h * d)


def _ssm_combine(e1, e2):
    a1, b1 = e1
    a2, b2 = e2
    return a2 * a1, a2 * b1 + b2


def s5_bidir(u, op, s0):
    b, l, _ = u.shape
    u32 = u.astype(F32).reshape(b, l, G_D, S5_CH)
    lam = lax.complex(op['lam_re'].astype(F32), op['lam_im'].astype(F32))
    dt = jnp.exp(op['log_dt'].astype(F32))[..., None]
    a_bar = jnp.exp(lam * dt)
    bmat = lax.complex(op['b_re'].astype(F32), op['b_im'].astype(F32))
    b_bar = ((a_bar - 1) / lam)[..., None] * bmat
    cmat = lax.complex(op['c_re'].astype(F32), op['c_im'].astype(F32))

    def scan_dir(dr, useq, s_init):
        bu = jnp.einsum('gpc,blgc->lbgp', b_bar[dr], useq.astype(jnp.complex64))
        bu = bu.at[0].add(a_bar[dr] * s_init)
        a_seq = jnp.broadcast_to(a_bar[dr], (l, 1, G_D, S5_P))
        _, xs = lax.associative_scan(_ssm_combine, (a_seq, bu), axis=0)
        y = jnp.einsum('gcp,lbgp->blgc', cmat[dr], xs).real
        return y, xs[-1]

    y_f, s_f = scan_dir(0, u32, s0[:, 0])
    y_b, s_b = scan_dir(1, jnp.flip(u32, axis=1), s0[:, 1])
    y = y_f + jnp.flip(y_b, axis=1) + u32 * op['d'].astype(F32).reshape(G_D, S5_CH)
    y = jax.nn.gelu(y.reshape(b, l, D_S5))
    y = y * jax.nn.sigmoid(y @ op['glu_w'].astype(F32) + op['glu_b'].astype(F32))
    return y.astype(u.dtype), jnp.stack([s_f, s_b], axis=1)


BF16 = jnp.bfloat16
SUBLANES = 8
VMEM_LIMIT = 56 * 1024 * 1024
S5_ROWS = 1024
S5_STATE = G_D * S5_P


def _s5_scan_kernel(u_ref, bmat_ref, cmat_ref, a_ref, s0_ref, y_ref, sfin_ref, xs_ref, st_ref, *, nblk, bsz):
    dr = pl.program_id(0)
    blk = pl.program_id(1)

    @pl.when(blk == 0)
    def _():
        st_ref[...] = s0_ref[0]

    xs_ref[...] = jnp.dot(u_ref[...].astype(BF16), bmat_ref[0], preferred_element_type=F32)
    a_re = jnp.broadcast_to(a_ref[0, 0:1, :], (SUBLANES, S5_STATE))
    a_im = jnp.broadcast_to(a_ref[0, 1:2, :], (SUBLANES, S5_STATE))
    steps = S5_ROWS // bsz
    for sg in range(bsz // SUBLANES):
        rows = pl.ds(sg * SUBLANES, SUBLANES)

        def body(i, carry, sg=sg):
            x_re, x_im = carry
            l = i + dr * (steps - 1 - 2 * i)
            r = pl.ds(pl.multiple_of(l * bsz + sg * SUBLANES, SUBLANES), SUBLANES)
            n_re = a_re * x_re - a_im * x_im + xs_ref[r, 0:S5_STATE]
            n_im = a_re * x_im + a_im * x_re + xs_ref[r, S5_STATE:2 * S5_STATE]
            xs_ref[r, 0:S5_STATE] = n_re
            xs_ref[r, S5_STATE:2 * S5_STATE] = n_im
            return n_re, n_im

        x_re, x_im = lax.fori_loop(0, steps, body, (st_ref[rows, 0:S5_STATE], st_ref[rows, S5_STATE:2 * S5_STATE]),
                                   unroll=4)
        st_ref[rows, 0:S5_STATE] = x_re
        st_ref[rows, S5_STATE:2 * S5_STATE] = x_im
    y_ref[0] = jnp.dot(xs_ref[...].astype(BF16), cmat_ref[0], preferred_element_type=F32)

    @pl.when(blk == nblk - 1)
    def _():
        sfin_ref[0] = st_ref[...]


def _s5_out_kernel(yf_ref, yb_ref, u_ref, d_ref, w_ref, b_ref, o_ref):
    y = yf_ref[0] + yb_ref[0] + u_ref[...] * d_ref[...]
    y = jax.nn.gelu(y)
    gate = jnp.dot(y.astype(BF16), w_ref[...], preferred_element_type=F32) + b_ref[...]
    o_ref[...] = y * jax.nn.sigmoid(gate)


def s5_operators(op):
    lam = lax.complex(op['lam_re'].astype(F32), op['lam_im'].astype(F32))
    dt = jnp.exp(op['log_dt'].astype(F32))[..., None]
    a_bar = jnp.exp(lam * dt)
    bmat = lax.complex(op['b_re'].astype(F32), op['b_im'].astype(F32))
    b_bar = ((a_bar - 1) / lam)[..., None] * bmat
    eye = jnp.eye(G_D, dtype=F32)

    def b_blocks(t):
        return jnp.einsum('dgpc,gh->dgchp', t, eye).reshape(2, D_S5, S5_STATE)

    def c_blocks(t):
        return jnp.einsum('dgcp,gh->dgphc', t, eye).reshape(2, S5_STATE, D_S5)

    b_blk = jnp.concatenate([b_blocks(b_bar.real), b_blocks(b_bar.imag)], axis=-1).astype(BF16)
    c_blk = jnp.concatenate([c_blocks(op['c_re'].astype(F32)), -c_blocks(op['c_im'].astype(F32))],
                            axis=1).astype(BF16)
    a_vec = jnp.stack([a_bar.real.reshape(2, S5_STATE), a_bar.imag.reshape(2, S5_STATE)], axis=1)
    return b_blk, c_blk, a_vec


def s5_pallas(u, op, ops, s0):
    b, l, _ = u.shape
    b_blk, c_blk, a_vec = ops
    n = b * l
    nblk = n // S5_ROWS
    u_t = jnp.swapaxes(u, 0, 1).reshape(n, D_S5)

    def rows_map(dr, blk):
        return (blk + dr * (nblk - 1 - 2 * blk), 0)

    y2, s_fin = pl.pallas_call(
        functools.partial(_s5_scan_kernel, nblk=nblk, bsz=b),
        grid=(2, nblk),
        in_specs=[
            pl.BlockSpec((S5_ROWS, D_S5), rows_map),
            pl.BlockSpec((1, D_S5, 2 * S5_STATE), lambda dr, blk: (dr, 0, 0)),
            pl.BlockSpec((1, 2 * S5_STATE, D_S5), lambda dr, blk: (dr, 0, 0)),
            pl.BlockSpec((1, 2, S5_STATE), lambda dr, blk: (dr, 0, 0)),
            pl.BlockSpec((1, b, 2 * S5_STATE), lambda dr, blk: (dr, 0, 0)),
        ],
        out_specs=[
            pl.BlockSpec((1, S5_ROWS, D_S5), lambda dr, blk: (dr,) + rows_map(dr, blk)),
            pl.BlockSpec((1, b, 2 * S5_STATE), lambda dr, blk: (dr, 0, 0)),
        ],
        out_shape=[jax.ShapeDtypeStruct((2, n, D_S5), F32), jax.ShapeDtypeStruct((2, b, 2 * S5_STATE), F32)],
        scratch_shapes=[pltpu.VMEM((S5_ROWS, 2 * S5_STATE), F32), pltpu.VMEM((b, 2 * S5_STATE), F32)],
        compiler_params=pltpu.CompilerParams(dimension_semantics=("arbitrary", "arbitrary"),
                                             vmem_limit_bytes=VMEM_LIMIT),
        name="s5_scan",
    )(u_t, b_blk, c_blk, a_vec, s0)
    tm = S5_ROWS
    y = pl.pallas_call(
        _s5_out_kernel,
        grid=(n // tm,),
        in_specs=[
            pl.BlockSpec((1, tm, D_S5), lambda i: (0, i, 0)),
            pl.BlockSpec((1, tm, D_S5), lambda i: (1, i, 0)),
            pl.BlockSpec((tm, D_S5), lambda i: (i, 0)),
            pl.BlockSpec((1, D_S5), lambda i: (0, 0)),
            pl.BlockSpec((D_S5, D_S5), lambda i: (0, 0)),
            pl.BlockSpec((1, D_S5), lambda i: (0, 0)),
        ],
        out_specs=pl.BlockSpec((tm, D_S5), lambda i: (i, 0)),
        out_shape=jax.ShapeDtypeStruct((n, D_S5), F32),
        name="s5_out",
    )(y2, y2, u_t, op['d'].astype(F32).reshape(1, D_S5), op['glu_w'].astype(BF16),
      op['glu_b'].astype(F32).reshape(1, D_S5))
    return jnp.swapaxes(y.reshape(l, b, D_S5), 0, 1), s_fin


def even_mixer_context(h, ep):
    b, l, _ = h.shape
    qkv, z, a, bb, bq, bk, bv = split_cols(h @ ep['w_in'], EVEN_SPLITS)
    q, k, v, g, beta = deltanet_inputs(qkv, a, bb, ep)
    zero = jnp.zeros((b, HA, DK, DK), F32)
    o, s_f, s_b = deltanet_bidir(q, k, v, g, beta, zero, zero)
    y_a = deltanet_out(o, z, ep['norm_w'])
    bq = bq.reshape(b, l, HB, HEAD_DIM)
    bk = bk.reshape(b, l, HKV_B, HEAD_DIM)
    bv = bv.reshape(b, l, HKV_B, HEAD_DIM)
    y_b = blocked_attention(bq, bk, bv, ep['sink'])
    y = jnp.concatenate([y_a, y_b.astype(y_a.dtype)], axis=-1) @ ep['w_out']
    return y, jnp.stack([s_f, s_b], axis=1), bk, bv


def even_mixer_latent(h, ep, dn_state, ctx_k, ctx_v, cos, sin):
    b, n, _ = h.shape
    qkv, z, a, bb, bq, bk, bv = split_cols(h @ ep['w_in'], EVEN_SPLITS)
    q, k, v, g, beta = deltanet_inputs(qkv, a, bb, ep)
    o, _, _ = deltanet_bidir(q, k, v, g, beta, dn_state[:, 0].astype(F32), dn_state[:, 1].astype(F32))
    y_a = deltanet_out(o, z, ep['norm_w'])
    bq = apply_axial_rope(bq.reshape(b, n, HB, HEAD_DIM), cos, sin)
    bk = apply_axial_rope(bk.reshape(b, n, HKV_B, HEAD_DIM), cos, sin)
    bv = bv.reshape(b, n, HKV_B, HEAD_DIM)
    y_b = windowed_attention(bq, bk, bv, ctx_k, ctx_v, ep['sink'])
    return jnp.concatenate([y_a, y_b.astype(y_a.dtype)], axis=-1) @ ep['w_out']


def odd_mixer_context(h, op):
    b, l, _ = h.shape
    cq, ck, cv, u = split_cols(h @ op['w_in'], ODD_SPLITS)
    cq = rms_norm(cq.reshape(b, l, HC, HEAD_DIM), op['q_norm'])
    ck = rms_norm(ck.reshape(b, l, HKV_C, HEAD_DIM), op['k_norm'])
    cv = cv.reshape(b, l, HKV_C, HEAD_DIM)
    y_c = blocked_attention(cq, ck, cv)
    y_d, s_fin = s5_pallas(u, op, s5_operators(op), jnp.zeros((2, b, 2 * S5_STATE), F32))
    y = jnp.concatenate([y_c, y_d.astype(y_c.dtype)], axis=-1) @ op['w_out']
    s_fin = jnp.transpose(s_fin.reshape(2, b, 2, G_D, S5_P), (1, 0, 3, 4, 2))
    return y, ck, cv, s_fin


def odd_mixer_latent(h, op, ctx_k, ctx_v, s5_state, cos, sin):
    b, n, _ = h.shape
    cq, ck, cv, u = split_cols(h @ op['w_in'], ODD_SPLITS)
    cq = apply_axial_rope(rms_norm(cq.reshape(b, n, HC, HEAD_DIM), op['q_norm']), cos, sin)
    ck = apply_axial_rope(rms_norm(ck.reshape(b, n, HKV_C, HEAD_DIM), op['k_norm']), cos, sin)
    cv = cv.reshape(b, n, HKV_C, HEAD_DIM)
    keys = jnp.concatenate([ck, ctx_k.astype(ck.dtype)], axis=1)
    vals = jnp.concatenate([cv, ctx_v.astype(cv.dtype)], axis=1)
    y_c = blocked_attention(cq, keys, vals)
    s0 = jnp.transpose(s5_state.astype(F32), (1, 0, 4, 2, 3)).reshape(2, b, 2 * S5_STATE)
    y_d, _ = s5_pallas(u, op, s5_operators(op), s0)
    return jnp.concatenate([y_c, y_d.astype(y_c.dtype)], axis=-1) @ op['w_out']


def hier_moe(x, mp):
    shp = x.shape
    t = x.reshape(-1, D_MODEL)
    lg = (t @ mp['rg'] + mp['rg_b']).astype(F32)
    pg = jax.nn.softmax(lg, axis=-1)
    gidx = jnp.argmax(lg, axis=-1)
    gate_g = jnp.take_along_axis(pg, gidx[:, None], axis=-1)
    le = (t @ mp['re'] + mp['re_b']).astype(F32).reshape(-1, N_GROUPS, EXP_PER_GROUP)
    le = jnp.take_along_axis(le, gidx[:, None, None], axis=1)[:, 0]
    top_v, top_i = lax.top_k(le, TOP_K)
    pe = jax.nn.softmax(top_v, axis=-1) * gate_g
    eid = gidx[:, None] * EXP_PER_GROUP + top_i
    combine = jnp.sum(jax.nn.one_hot(eid, N_EXP, dtype=F32) * pe[..., None], axis=1)
    hid = jax.nn.silu(t @ mp['wg']) * (t @ mp['wu'])
    hid = hid.reshape(-1, N_EXP, EXPERT_FF) * combine[..., None].astype(t.dtype)
    y = hid.reshape(-1, N_EXP * EXPERT_FF) @ mp['wd']
    return y.reshape(shp)


def _final_norm_kernel(x_ref, w_ref, o_ref):
    x = x_ref[...]
    y = x * lax.rsqrt(jnp.mean(x * x, axis=-1, keepdims=True) + EPS)
    o_ref[...] = y * w_ref[...]


def final_norm(x, w):
    t, d = x.shape
    tm = 512
    return pl.pallas_call(
        _final_norm_kernel,
        grid=(t // tm,),
        in_specs=[pl.BlockSpec((tm, d), lambda i: (i, 0)), pl.BlockSpec((1, d), lambda i: (0, 0))],
        out_specs=pl.BlockSpec((tm, d), lambda i: (i, 0)),
        out_shape=jax.ShapeDtypeStruct((t, d), F32),
        name="final_norm",
    )(x, w.reshape(1, d))


def kernel(x_prompt, x_sample, c, c_ctx, state_dn, cache_b_k, cache_b_v, cache_c_k, cache_c_v, state_s5,
           ada_w, ada_b, norm_w, w_in_e, dn_conv_w, dn_a_log, dn_dt_bias, dn_norm_w, b_sink, w_out_e,
           w_in_o, c_q_norm, c_k_norm, s5_lam_re, s5_lam_im, s5_log_dt, s5_b_re, s5_b_im, s5_c_re, s5_c_im,
           s5_d, s5_glu_w, s5_glu_b, w_out_o, moe_rg, moe_rg_b, moe_re, moe_re_b, moe_wg, moe_wu, moe_wd,
           final_norm_w):
    n_lat = x_sample.shape[1]
    rows = n_lat // GRID_W
    cos, sin = axial_rope_tables(rows)
    xp, xs = x_prompt, x_sample
    new_dn, new_bk, new_bv, new_ck, new_cv, new_s5 = [], [], [], [], [], []
    for layer in range(DEPTH):
        mod_p = ada_modulation(c_ctx[None, :], ada_w[layer], ada_b[layer])
        mod_s = ada_modulation(c, ada_w[layer], ada_b[layer])
        hp = modulate(xp, norm_w[layer, 0], mod_p[0], mod_p[1])
        hs = modulate(xs, norm_w[layer, 0], mod_s[0], mod_s[1])
        if layer % 2 == 0:
            i = layer // 2
            ep = {'w_in': w_in_e[i], 'conv_w': dn_conv_w[i], 'a_log': dn_a_log[i], 'dt_bias': dn_dt_bias[i],
                  'norm_w': dn_norm_w[i], 'sink': b_sink[i], 'w_out': w_out_e[i]}
            yp, dn, bk, bv = even_mixer_context(hp, ep)
            ys = even_mixer_latent(hs, ep, state_dn[:, i], cache_b_k[:, i], cache_b_v[:, i], cos, sin)
            new_dn.append(dn)
            new_bk.append(bk)
            new_bv.append(bv)
        else:
            i = layer // 2
            op = {'w_in': w_in_o[i], 'q_norm': c_q_norm[i], 'k_norm': c_k_norm[i], 'lam_re': s5_lam_re[i],
                  'lam_im': s5_lam_im[i], 'log_dt': s5_log_dt[i], 'b_re': s5_b_re[i], 'b_im': s5_b_im[i],
                  'c_re': s5_c_re[i], 'c_im': s5_c_im[i], 'd': s5_d[i], 'glu_w': s5_glu_w[i],
                  'glu_b': s5_glu_b[i], 'w_out': w_out_o[i]}
            yp, ck, cv, s5 = odd_mixer_context(hp, op)
            ys = odd_mixer_latent(hs, op, cache_c_k[:, i], cache_c_v[:, i], state_s5[:, i], cos, sin)
            new_ck.append(ck)
            new_cv.append(cv)
            new_s5.append(s5)
        xp = xp + mod_p[2] * yp
        xs = xs + mod_s[2] * ys
        mp = {'rg': moe_rg[layer], 'rg_b': moe_rg_b[layer], 're': moe_re[layer], 're_b': moe_re_b[layer],
              'wg': moe_wg[layer], 'wu': moe_wu[layer], 'wd': moe_wd[layer]}
        xp = xp + mod_p[5] * hier_moe(modulate(xp, norm_w[layer, 1], mod_p[3], mod_p[4]), mp)
        xs = xs + mod_s[5] * hier_moe(modulate(xs, norm_w[layer, 1], mod_s[3], mod_s[4]), mp)
    y_prompt = final_norm(xp.reshape(-1, D_MODEL), final_norm_w).reshape(xp.shape)
    y_sample = final_norm(xs.reshape(-1, D_MODEL), final_norm_w).reshape(xs.shape)
    return (y_prompt, y_sample, jnp.stack(new_dn, axis=1), jnp.stack(new_bk, axis=1), jnp.stack(new_bv, axis=1),
            jnp.stack(new_ck, axis=1), jnp.stack(new_cv, axis=1), jnp.stack(new_s5, axis=1))
```

```python
import functools

import numpy as np
import jax
import jax.numpy as jnp
from jax import lax
from jax.experimental import pallas as pl
from jax.experimental.pallas import tpu as pltpu

D_MODEL = 1024
BATCH = 32
SEQ = 256
DEPTH = 2
DEC_BATCH = 8
DEC_SEQ = 2048
PAST_LEN = 256

GRID_W = 64
HEAD_DIM = 64
BLK = 128
ROPE_THETA = 10000.0
EPS = 1e-6
N_EVEN = (DEPTH + 1) // 2
N_ODD = DEPTH // 2
HA = D_MODEL // 128
DK = HEAD_DIM
CONV_W = 5
DN_CHUNK = 64
HB = D_MODEL // 128
HKV_B = HB // 4
WINDOW = 128
HC = (3 * D_MODEL // 4) // HEAD_DIM
HKV_C = HC // 3
D_S5 = D_MODEL // 4
S5_CH = 16
G_D = D_S5 // S5_CH
S5_P = 64
N_GROUPS = 4
EXP_PER_GROUP = 8
N_EXP = N_GROUPS * EXP_PER_GROUP
EXPERT_FF = D_MODEL // 8
TOP_K = 2

EVEN_SPLITS = (3 * HA * DK, HA * DK, 2 * HA, 2 * HA, HB * HEAD_DIM, HKV_B * HEAD_DIM, HKV_B * HEAD_DIM)
D_IN_EVEN = sum(EVEN_SPLITS)
D_MIX_EVEN = HA * DK + HB * HEAD_DIM
ODD_SPLITS = (HC * HEAD_DIM, HKV_C * HEAD_DIM, HKV_C * HEAD_DIM, D_S5)
D_IN_ODD = sum(ODD_SPLITS)
D_MIX_ODD = HC * HEAD_DIM + D_S5

F32 = jnp.float32


def split_cols(x, sizes):
    idx = np.cumsum(sizes)[:-1].tolist()
    return jnp.split(x, idx, axis=-1)


def rms_norm(x, w):
    x32 = x.astype(F32)
    y = x32 * lax.rsqrt(jnp.mean(x32 * x32, axis=-1, keepdims=True) + EPS)
    return (y * w.astype(F32)).astype(x.dtype)


def l2norm(x):
    return x * lax.rsqrt(jnp.sum(x * x, axis=-1, keepdims=True) + EPS)


def ada_modulation(cvec, w, b):
    m = jax.nn.silu(cvec) @ w + b
    return [t[:, None, :] for t in jnp.split(m, 6, axis=-1)]


def modulate(x, w, shift, scale):
    return rms_norm(x, w) * (1 + scale) + shift


def axial_rope_tables(rows):
    row = jnp.repeat(jnp.arange(rows), GRID_W).astype(F32)
    col = jnp.tile(jnp.arange(GRID_W), rows).astype(F32)
    quarter = HEAD_DIM // 4
    freqs = ROPE_THETA ** (-jnp.arange(quarter, dtype=F32) / quarter)
    ang_r = row[:, None] * freqs
    ang_c = col[:, None] * freqs
    ang = jnp.concatenate([ang_r, ang_r, ang_c, ang_c], axis=-1)
    return jnp.cos(ang), jnp.sin(ang)


def apply_axial_rope(x, cos, sin):
    q4 = HEAD_DIM // 4
    half = HEAD_DIM // 2

    def rot(u):
        return jnp.concatenate([-u[..., q4:], u[..., :q4]], axis=-1)

    xrot = jnp.concatenate([rot(x[..., :half]), rot(x[..., half:])], axis=-1)
    return (x * cos[:, None, :] + xrot * sin[:, None, :]).astype(x.dtype)


def short_conv(x, w):
    pad = CONV_W // 2
    y = lax.conv_general_dilated(x, w[:, None, :].astype(x.dtype), window_strides=(1,), padding=[(pad, pad)],
                                 dimension_numbers=('NWC', 'WIO', 'NWC'), feature_group_count=x.shape[-1])
    return jax.nn.silu(y)


def gated_delta_chunked(q, k, v, g, beta, s0):
    b, l, h, d = q.shape
    n = l // DN_CHUNK

    def chunks(t):
        t = t.reshape((b, n, DN_CHUNK, h) + t.shape[3:])
        return jnp.moveaxis(t, (1, 3), (0, 2))

    q = chunks(q * (d ** -0.5))
    k = chunks(k)
    v = chunks(v)
    gc = jnp.cumsum(chunks(g), axis=-1)
    beta = chunks(beta)
    tri = jnp.tril(jnp.ones((DN_CHUNK, DN_CHUNK), bool))
    strict = jnp.tril(jnp.ones((DN_CHUNK, DN_CHUNK), bool), -1)
    decay = jnp.exp(jnp.where(tri, gc[..., :, None] - gc[..., None, :], -jnp.inf))
    kb = k * beta[..., None]
    m = jnp.where(strict, jnp.einsum('nbhid,nbhjd->nbhij', kb, k) * decay, 0.0)
    eye = jnp.eye(DN_CHUNK, dtype=F32)
    rhs = jnp.concatenate([v * beta[..., None], kb * jnp.exp(gc)[..., None]], axis=-1)
    sol = lax.linalg.triangular_solve(eye + m, rhs, left_side=True, lower=True, unit_diagonal=True)
    u, w = sol[..., :d], sol[..., d:]

    def step(s, inp):
        qi, ki, ui, wi, gci, di = inp
        attn = jnp.einsum('bhid,bhjd->bhij', qi, ki) * di
        v_new = ui - jnp.einsum('bhcd,bhde->bhce', wi, s)
        o = (jnp.einsum('bhcd,bhde->bhce', qi * jnp.exp(gci)[..., None], s)
             + jnp.einsum('bhij,bhje->bhie', attn, v_new))
        g_last = gci[..., -1]
        s = (s * jnp.exp(g_last)[..., None, None]
             + jnp.einsum('bhcd,bhce->bhde', ki * jnp.exp(g_last[..., None] - gci)[..., None], v_new))
        return s, o

    s_fin, o = lax.scan(step, s0, (q, k, u, w, gc, decay))
    o = jnp.moveaxis(o, (0, 2), (1, 3)).reshape(b, l, h, d)
    return o, s_fin


def deltanet_inputs(qkv, a, bb, ep):
    b, l, _ = qkv.shape
    qkv = short_conv(qkv, ep['conv_w']).astype(F32).reshape(b, l, 3, HA, DK)
    q = l2norm(qkv[:, :, 0])
    k = l2norm(qkv[:, :, 1])
    v = qkv[:, :, 2]
    a = a.astype(F32).reshape(b, l, 2, HA)
    bb = bb.astype(F32).reshape(b, l, 2, HA)
    g = -jnp.exp(ep['a_log'].astype(F32)) * jax.nn.softplus(a + ep['dt_bias'].astype(F32))
    beta = jax.nn.sigmoid(bb)
    return q, k, v, g, beta


def deltanet_bidir(q, k, v, g, beta, s0_f, s0_b):
    o_f, s_f = gated_delta_chunked(q, k, v, g[:, :, 0], beta[:, :, 0], s0_f)
    fl = lambda t: jnp.flip(t, axis=1)
    o_b, s_b = gated_delta_chunked(fl(q), fl(k), fl(v), fl(g[:, :, 1]), fl(beta[:, :, 1]), s0_b)
    return o_f + fl(o_b), s_f, s_b


def deltanet_out(o, z, norm_w):
    b, l = o.shape[:2]
    o = o * lax.rsqrt(jnp.mean(o * o, axis=-1, keepdims=True) + EPS) * norm_w.astype(F32)
    return (o.reshape(b, l, HA * DK) * jax.nn.silu(z.astype(F32))).astype(z.dtype)


def blocked_attention(q, k, v, sink=None):
    b, lq, h, d = q.shape
    hk = k.shape[2]
    grp = h // hk
    nb = lq // BLK
    qb = jnp.moveaxis(q.reshape(b, nb, BLK, hk, grp, d), 1, 0)
    scale = d ** -0.5

    def one_block(qblk):
        s = jnp.einsum('bqhgd,bkhd->bhgqk', qblk, k).astype(F32) * scale
        if sink is not None:
            s_sink = jnp.broadcast_to(sink.astype(F32).reshape(1, hk, grp, 1, 1), s.shape[:-1] + (1,))
            p = jax.nn.softmax(jnp.concatenate([s, s_sink], axis=-1), axis=-1)[..., :-1]
        else:
            p = jax.nn.softmax(s, axis=-1)
        return jnp.einsum('bhgqk,bkhd->bqhgd', p.astype(v.dtype), v)

    o = lax.map(one_block, qb)
    return jnp.moveaxis(o, 0, 1).reshape(b, lq, h * d)


def windowed_attention(q, k, v, kc, vc, sink):
    b, n, h, d = q.shape
    hk = k.shape[2]
    grp = h // hk
    nb = n // BLK
    pad = ((0, 0), (BLK, BLK), (0, 0), (0, 0))

    def windows(t):
        t = jnp.pad(t, pad).reshape(b, nb + 2, BLK, hk, d)
        return jnp.concatenate([t[:, :-2], t[:, 1:-1], t[:, 2:]], axis=2)

    kw, vw = windows(k), windows(v)
    qb = q.reshape(b, nb, BLK, hk, grp, d)
    scale = d ** -0.5
    s_loc = jnp.einsum('bnqhgd,bnkhd->bnhgqk', qb, kw.astype(q.dtype)).astype(F32) * scale
    qi = jnp.arange(BLK)[:, None] + BLK
    kj = jnp.arange(3 * BLK)[None, :]
    kabs = (jnp.arange(nb)[:, None, None] - 1) * BLK + kj
    mask = (jnp.abs(qi - kj) <= WINDOW)[None] & (kabs >= 0) & (kabs < n)
    s_loc = jnp.where(mask[None, :, None, None], s_loc, -jnp.inf)
    s_ctx = jnp.einsum('bnqhgd,bkhd->bnhgqk', qb, kc.astype(q.dtype)).astype(F32) * scale
    s_sink = jnp.broadcast_to(sink.astype(F32).reshape(1, 1, hk, grp, 1, 1), s_loc.shape[:-1] + (1,))
    p = jax.nn.softmax(jnp.concatenate([s_loc, s_ctx, s_sink], axis=-1), axis=-1)
    p_loc = p[..., :3 * BLK].astype(v.dtype)
    p_ctx = p[..., 3 * BLK:-1].astype(v.dtype)
    o = (jnp.einsum('bnhgqk,bnkhd->bnqhgd', p_loc, vw)
         + jnp.einsum('bnhgqk,bkhd->bnqhgd', p_ctx, vc.astype(v.dtype)))
    return o.reshape(b, n, h * d)


def _ssm_combine(e1, e2):
    a1, b1 = e1
    a2, b2 = e2
    return a2 * a1, a2 * b1 + b2


def s5_bidir(u, op, s0):
    b, l, _ = u.shape
    u32 = u.astype(F32).reshape(b, l, G_D, S5_CH)
    lam = lax.complex(op['lam_re'].astype(F32), op['lam_im'].astype(F32))
    dt = jnp.exp(op['log_dt'].astype(F32))[..., None]
    a_bar = jnp.exp(lam * dt)
    bmat = lax.complex(op['b_re'].astype(F32), op['b_im'].astype(F32))
    b_bar = ((a_bar - 1) / lam)[..., None] * bmat
    cmat = lax.complex(op['c_re'].astype(F32), op['c_im'].astype(F32))

    def scan_dir(dr, useq, s_init):
        bu = jnp.einsum('gpc,blgc->lbgp', b_bar[dr], useq.astype(jnp.complex64))
        bu = bu.at[0].add(a_bar[dr] * s_init)
        a_seq = jnp.broadcast_to(a_bar[dr], (l, 1, G_D, S5_P))
        _, xs = lax.associative_scan(_ssm_combine, (a_seq, bu), axis=0)
        y = jnp.einsum('gcp,lbgp->blgc', cmat[dr], xs).real
        return y, xs[-1]

    y_f, s_f = scan_dir(0, u32, s0[:, 0])
    y_b, s_b = scan_dir(1, jnp.flip(u32, axis=1), s0[:, 1])
    y = y_f + jnp.flip(y_b, axis=1) + u32 * op['d'].astype(F32).reshape(G_D, S5_CH)
    y = jax.nn.gelu(y.reshape(b, l, D_S5))
    y = y * jax.nn.sigmoid(y @ op['glu_w'].astype(F32) + op['glu_b'].astype(F32))
    return y.astype(u.dtype), jnp.stack([s_f, s_b], axis=1)


BF16 = jnp.bfloat16
SUBLANES = 8
VMEM_LIMIT = 56 * 1024 * 1024
S5_ROWS = 1024
S5_STATE = G_D * S5_P


def _s5_scan_kernel(u_ref, bmat_ref, cmat_ref, a_ref, s0_ref, y_ref, sfin_ref, xs_ref, st_ref, *, nblk, bsz):
    dr = pl.program_id(0)
    blk = pl.program_id(1)

    @pl.when(blk == 0)
    def _():
        st_ref[...] = s0_ref[0]

    xs_ref[...] = jnp.dot(u_ref[...].astype(BF16), bmat_ref[0], preferred_element_type=F32)
    a_re = jnp.broadcast_to(a_ref[0, 0:1, :], (SUBLANES, S5_STATE))
    a_im = jnp.broadcast_to(a_ref[0, 1:2, :], (SUBLANES, S5_STATE))
    steps = S5_ROWS // bsz
    for sg in range(bsz // SUBLANES):
        rows = pl.ds(sg * SUBLANES, SUBLANES)

        def body(i, carry, sg=sg):
            x_re, x_im = carry
            l = i + dr * (steps - 1 - 2 * i)
            r = pl.ds(pl.multiple_of(l * bsz + sg * SUBLANES, SUBLANES), SUBLANES)
            n_re = a_re * x_re - a_im * x_im + xs_ref[r, 0:S5_STATE]
            n_im = a_re * x_im + a_im * x_re + xs_ref[r, S5_STATE:2 * S5_STATE]
            xs_ref[r, 0:S5_STATE] = n_re
            xs_ref[r, S5_STATE:2 * S5_STATE] = n_im
            return n_re, n_im

        x_re, x_im = lax.fori_loop(0, steps, body, (st_ref[rows, 0:S5_STATE], st_ref[rows, S5_STATE:2 * S5_STATE]),
                                   unroll=4)
        st_ref[rows, 0:S5_STATE] = x_re
        st_ref[rows, S5_STATE:2 * S5_STATE] = x_im
    y_ref[0] = jnp.dot(xs_ref[...].astype(BF16), cmat_ref[0], preferred_element_type=F32)

    @pl.when(blk == nblk - 1)
    def _():
        sfin_ref[0] = st_ref[...]


def _s5_out_kernel(yf_ref, yb_ref, u_ref, d_ref, w_ref, b_ref, o_ref):
    y = yf_ref[0] + yb_ref[0] + u_ref[...] * d_ref[...]
    y = jax.nn.gelu(y)
    gate = jnp.dot(y.astype(BF16), w_ref[...], preferred_element_type=F32) + b_ref[...]
    o_ref[...] = y * jax.nn.sigmoid(gate)


def s5_operators(op):
    lam = lax.complex(op['lam_re'].astype(F32), op['lam_im'].astype(F32))
    dt = jnp.exp(op['log_dt'].astype(F32))[..., None]
    a_bar = jnp.exp(lam * dt)
    bmat = lax.complex(op['b_re'].astype(F32), op['b_im'].astype(F32))
    b_bar = ((a_bar - 1) / lam)[..., None] * bmat
    eye = jnp.eye(G_D, dtype=F32)

    def b_blocks(t):
        return jnp.einsum('dgpc,gh->dgchp', t, eye).reshape(2, D_S5, S5_STATE)

    def c_blocks(t):
        return jnp.einsum('dgcp,gh->dgphc', t, eye).reshape(2, S5_STATE, D_S5)

    b_blk = jnp.concatenate([b_blocks(b_bar.real), b_blocks(b_bar.imag)], axis=-1).astype(BF16)
    c_blk = jnp.concatenate([c_blocks(op['c_re'].astype(F32)), -c_blocks(op['c_im'].astype(F32))],
                            axis=1).astype(BF16)
    a_vec = jnp.stack([a_bar.real.reshape(2, S5_STATE), a_bar.imag.reshape(2, S5_STATE)], axis=1)
    return b_blk, c_blk, a_vec


def s5_pallas(u, op, ops, s0):
    b, l, _ = u.shape
    b_blk, c_blk, a_vec = ops
    n = b * l
    nblk = n // S5_ROWS
    u_t = jnp.swapaxes(u, 0, 1).reshape(n, D_S5)

    def rows_map(dr, blk):
        return (blk + dr * (nblk - 1 - 2 * blk), 0)

    y2, s_fin = pl.pallas_call(
        functools.partial(_s5_scan_kernel, nblk=nblk, bsz=b),
        grid=(2, nblk),
        in_specs=[
            pl.BlockSpec((S5_ROWS, D_S5), rows_map),
            pl.BlockSpec((1, D_S5, 2 * S5_STATE), lambda dr, blk: (dr, 0, 0)),
            pl.BlockSpec((1, 2 * S5_STATE, D_S5), lambda dr, blk: (dr, 0, 0)),
            pl.BlockSpec((1, 2, S5_STATE), lambda dr, blk: (dr, 0, 0)),
            pl.BlockSpec((1, b, 2 * S5_STATE), lambda dr, blk: (dr, 0, 0)),
        ],
        out_specs=[
            pl.BlockSpec((1, S5_ROWS, D_S5), lambda dr, blk: (dr,) + rows_map(dr, blk)),
            pl.BlockSpec((1, b, 2 * S5_STATE), lambda dr, blk: (dr, 0, 0)),
        ],
        out_shape=[jax.ShapeDtypeStruct((2, n, D_S5), F32), jax.ShapeDtypeStruct((2, b, 2 * S5_STATE), F32)],
        scratch_shapes=[pltpu.VMEM((S5_ROWS, 2 * S5_STATE), F32), pltpu.VMEM((b, 2 * S5_STATE), F32)],
        compiler_params=pltpu.CompilerParams(dimension_semantics=("arbitrary", "arbitrary"),
                                             vmem_limit_bytes=VMEM_LIMIT),
        name="s5_scan",
    )(u_t, b_blk, c_blk, a_vec, s0)
    tm = S5_ROWS
    y = pl.pallas_call(
        _s5_out_kernel,
        grid=(n // tm,),
        in_specs=[
            pl.BlockSpec((1, tm, D_S5), lambda i: (0, i, 0)),
            pl.BlockSpec((1, tm, D_S5), lambda i: (1, i, 0)),
            pl.BlockSpec((tm, D_S5), lambda i: (i, 0)),
            pl.BlockSpec((1, D_S5), lambda i: (0, 0)),
            pl.BlockSpec((D_S5, D_S5), lambda i: (0, 0)),
            pl.BlockSpec((1, D_S5), lambda i: (0, 0)),
        ],
        out_specs=pl.BlockSpec((tm, D_S5), lambda i: (i, 0)),
        out_shape=jax.ShapeDtypeStruct((n, D_S5), F32),
        name="s5_out",
    )(y2, y2, u_t, op['d'].astype(F32).reshape(1, D_S5), op['glu_w'].astype(BF16),
      op['glu_b'].astype(F32).reshape(1, D_S5))
    return jnp.swapaxes(y.reshape(l, b, D_S5), 0, 1), s_fin


LANES = 128
DN_PAIRS = HA * DK // LANES
NEG_BIG = -1e30


def _nt_dot(a, b):
    return lax.dot_general(a, b, (((1,), (1,)), ((), ())), preferred_element_type=F32)


def _bf16_dot(a, b):
    return jnp.dot(a.astype(BF16), b.astype(BF16), preferred_element_type=F32)


def _select_dot(x, sel):
    x1 = x.astype(BF16)
    r1 = x - x1.astype(F32)
    x2 = r1.astype(BF16)
    x3 = (r1 - x2.astype(F32)).astype(BF16)
    dot = functools.partial(jnp.dot, preferred_element_type=F32)
    return dot(x1, sel) + dot(x2, sel) + dot(x3, sel)


def _head_sums(x):
    lane = lax.broadcasted_iota(jnp.int32, x.shape, 1)
    lo = lane < HEAD_DIM
    s0 = jnp.sum(jnp.where(lo, x, 0.0), axis=-1, keepdims=True)
    s1 = jnp.sum(jnp.where(lo, 0.0, x), axis=-1, keepdims=True)
    return jnp.where(lo, s0, s1)


def _chunk_cumsum(x, reverse):
    n = x.shape[0]
    pos = lax.broadcasted_iota(jnp.int32, x.shape, 0) % DN_CHUNK
    s = 1
    while s < DN_CHUNK:
        if reverse:
            x = x + jnp.where(pos < DN_CHUNK - s, pltpu.roll(x, n - s, 0), 0.0)
        else:
            x = x + jnp.where(pos >= s, pltpu.roll(x, s, 0), 0.0)
        s *= 2
    return x


def _short_conv_silu(x, w):
    n = x.shape[0]
    row = lax.broadcasted_iota(jnp.int32, x.shape, 0)
    pad = CONV_W // 2
    y = x * w[pad:pad + 1, :]
    for t in range(CONV_W):
        s = t - pad
        if s == 0:
            continue
        shifted = pltpu.roll(x, (-s) % n, 0)
        valid = (row + s >= 0) & (row + s < n)
        y = y + jnp.where(valid, shifted, 0.0) * w[t:t + 1, :]
    return jax.nn.silu(y)


def _unit_tri_inverse(m, ri, ci):
    same16 = (ri // 16) == (ci // 16)
    same32 = (ri // 32) == (ci // 32)
    d1 = jnp.where(same16, m, 0.0).astype(BF16)
    d2 = jnp.dot(d1, d1, preferred_element_type=F32).astype(BF16)
    d4 = jnp.dot(d2, d2, preferred_element_type=F32).astype(BF16)
    d8 = jnp.dot(d4, d4, preferred_element_type=F32).astype(BF16)
    p = jnp.where(ri == ci, 1.0, 0.0) - jnp.where(same16, m, 0.0)
    for dk in (d2, d4, d8):
        p = p + jnp.dot(p.astype(BF16), dk, preferred_element_type=F32)
    for blk in (jnp.where(same16, 0.0, jnp.where(same32, m, 0.0)), jnp.where(same32, 0.0, m)):
        pb = p.astype(BF16)
        p = p - jnp.dot(pb, _bf16_dot(blk, pb).astype(BF16), preferred_element_type=F32)
    return p


def _deltanet_kernel(q_ref, k_ref, v_ref, z_ref, a_ref, bb_ref, cwq_ref, cwk_ref, cwv_ref, alog_ref, dtb_ref, nw_ref,
                     s0_ref, y_ref, sfin_ref,
                     qs_ref, ks_ref, vs_ref, o_ref, gc_ref, beta_ref, u_ref, w_ref, attn_ref, qp_ref, kpt_ref,
                     etot_ref, *, seq):
    hp = pl.program_id(1)
    nchunk = seq // DN_CHUNK
    c64 = DN_CHUNK

    q = _short_conv_silu(q_ref[0], cwq_ref[...])
    qs_ref[...] = q * lax.rsqrt(_head_sums(q * q) + EPS) * (DK ** -0.5)
    k = _short_conv_silu(k_ref[0], cwk_ref[...])
    ks_ref[...] = k * lax.rsqrt(_head_sums(k * k) + EPS)
    vs_ref[...] = _short_conv_silu(v_ref[0], cwv_ref[...])
    o_ref[...] = jnp.zeros_like(o_ref)

    g_col = -jnp.exp(alog_ref[...]) * jax.nn.softplus(a_ref[0] + dtb_ref[...])
    b_col = jax.nn.sigmoid(bb_ref[0])
    sel_row = lax.broadcasted_iota(jnp.int32, (2 * HA, 2 * LANES), 0)
    sel_lane = lax.broadcasted_iota(jnp.int32, (2 * HA, 2 * LANES), 1)
    sel = (sel_row == (sel_lane // LANES) * HA + 2 * hp + (sel_lane // HEAD_DIM) % 2).astype(BF16)
    g_b = _select_dot(g_col, sel)
    gc_ref[:, 0:LANES] = _chunk_cumsum(g_b[:, 0:LANES], reverse=False)
    gc_ref[:, LANES:2 * LANES] = _chunk_cumsum(g_b[:, LANES:2 * LANES], reverse=True)
    beta_ref[...] = _select_dot(b_col, sel)

    ri = lax.broadcasted_iota(jnp.int32, (c64, c64), 0)
    ci = lax.broadcasted_iota(jnp.int32, (c64, c64), 1)

    def intra_chunk(c, carry):
        rows = pl.ds(pl.multiple_of(c * c64, c64), c64)
        q2 = qs_ref[rows, :]
        k2 = ks_ref[rows, :]
        v2 = vs_ref[rows, :]
        for j in range(2):
            hl = slice(j * HEAD_DIM, (j + 1) * HEAD_DIM)
            qj, kj, vj = q2[:, hl], k2[:, hl], v2[:, hl]
            kjb = kj.astype(BF16)
            kk = _nt_dot(kjb, kjb)
            qk = _nt_dot(qj.astype(BF16), kjb)
            for d in range(2):
                src = slice(d * LANES + j * HEAD_DIM, d * LANES + (j + 1) * HEAD_DIM)
                dst = slice((2 * j + d) * HEAD_DIM, (2 * j + d + 1) * HEAD_DIM)
                gcb = gc_ref[rows, src]
                bet = beta_ref[rows, src]
                diff = gcb - gcb.T
                earlier = (ri >= ci) if d == 0 else (ri <= ci)
                strict = (ri > ci) if d == 0 else (ri < ci)
                dec = jnp.exp(jnp.where(earlier, diff, NEG_BIG))
                tinv = _unit_tri_inverse(jnp.where(strict, kk * bet * dec, 0.0), ri, ci)
                eg = jnp.exp(gcb)
                rhs = jnp.concatenate([vj * bet, kj * bet * eg], axis=-1)
                uw = _bf16_dot(tinv, rhs)
                gtot = gcb[c64 - 1:c64, :] if d == 0 else gcb[0:1, :]
                u_ref[rows, dst] = uw[:, 0:HEAD_DIM]
                w_ref[rows, dst] = uw[:, HEAD_DIM:2 * HEAD_DIM]
                attn_ref[rows, dst] = qk * dec
                qp_ref[rows, dst] = qj * eg
                kpt_ref[rows, dst] = (kj * jnp.exp(gtot - gcb)).T
                etot_ref[pl.ds(pl.multiple_of(c * SUBLANES, SUBLANES), SUBLANES), dst] = jnp.broadcast_to(
                    jnp.exp(gtot), (SUBLANES, HEAD_DIM))
        return carry

    lax.fori_loop(0, nchunk, intra_chunk, 0)

    def inter_chunk(c, states):
        new_states = []
        for j in range(2):
            for d in range(2):
                cc = c if d == 0 else nchunk - 1 - c
                rows = pl.ds(pl.multiple_of(cc * c64, c64), c64)
                dst = slice((2 * j + d) * HEAD_DIM, (2 * j + d + 1) * HEAD_DIM)
                s = states[2 * j + d]
                sb = s.astype(BF16)
                lhs = jnp.concatenate([w_ref[rows, dst], qp_ref[rows, dst]], axis=0).astype(BF16)
                ws_qs = jnp.dot(lhs, sb, preferred_element_type=F32)
                v_new = (u_ref[rows, dst] - ws_qs[0:c64]).astype(BF16)
                o = ws_qs[c64:2 * c64] + jnp.dot(attn_ref[rows, dst].astype(BF16), v_new, preferred_element_type=F32)
                e_tot = etot_ref[pl.ds(pl.multiple_of(cc * SUBLANES, SUBLANES), SUBLANES), dst][0:1, :]
                new_states.append(s * e_tot + jnp.dot(kpt_ref[rows, dst].astype(BF16), v_new,
                                                      preferred_element_type=F32))
                hl = slice(j * HEAD_DIM, (j + 1) * HEAD_DIM)
                o_ref[rows, hl] = o_ref[rows, hl] + o
        return tuple(new_states)

    init = tuple(s0_ref[0, d, j] for j in range(2) for d in range(2))
    fin = lax.fori_loop(0, nchunk, inter_chunk, init)
    for j in range(2):
        for d in range(2):
            sfin_ref[0, d, j] = fin[2 * j + d]

    o = o_ref[...]
    o = o * lax.rsqrt(_head_sums(o * o) * (1.0 / DK) + EPS) * nw_ref[...]
    y_ref[0] = o * jax.nn.silu(z_ref[0])


def deltanet_pallas(qkv, z, a, bb, ep, s0):
    b, l, _ = qkv.shape
    npair = DN_PAIRS

    def col_spec(off):
        return pl.BlockSpec((1, l, LANES), lambda i, hp: (i, 0, off + hp))

    def cw_spec(off):
        return pl.BlockSpec((CONV_W, LANES), lambda i, hp: (0, off + hp))

    def full(shape):
        return pl.BlockSpec(shape, lambda i, hp: (0,) * len(shape))

    state_spec = pl.BlockSpec((1, 2, 2, DK, DK), lambda i, hp: (i, 0, hp, 0, 0))
    row_f32 = pltpu.VMEM((l, LANES), F32)
    chain_f32 = pltpu.VMEM((l, 2 * LANES), F32)
    y, s_fin = pl.pallas_call(
        functools.partial(_deltanet_kernel, seq=l),
        grid=(b, npair),
        in_specs=[col_spec(0), col_spec(npair), col_spec(2 * npair), col_spec(0),
                  pl.BlockSpec((1, l, 2 * HA), lambda i, hp: (i, 0, 0)),
                  pl.BlockSpec((1, l, 2 * HA), lambda i, hp: (i, 0, 0)),
                  cw_spec(0), cw_spec(npair), cw_spec(2 * npair),
                  full((1, 2 * HA)), full((1, 2 * HA)), full((1, LANES)), state_spec],
        out_specs=[col_spec(0), state_spec],
        out_shape=[jax.ShapeDtypeStruct((b, l, HA * DK), F32), jax.ShapeDtypeStruct((b, 2, HA, DK, DK), F32)],
        scratch_shapes=[row_f32, row_f32, row_f32, row_f32, chain_f32, chain_f32,
                        chain_f32, chain_f32, chain_f32, chain_f32, chain_f32,
                        pltpu.VMEM((l // DN_CHUNK * SUBLANES, 2 * LANES), F32)],
        compiler_params=pltpu.CompilerParams(dimension_semantics=("arbitrary", "arbitrary"),
                                             vmem_limit_bytes=VMEM_LIMIT),
        name="deltanet",
    )(qkv, qkv, qkv, z, a, bb, ep['conv_w'], ep['conv_w'], ep['conv_w'],
      ep['a_log'].astype(F32).reshape(1, 2 * HA), ep['dt_bias'].astype(F32).reshape(1, 2 * HA),
      jnp.tile(ep['norm_w'].astype(F32), 2).reshape(1, LANES), s0)
    return y, s_fin


def even_mixer_context(h, ep):
    b, l, _ = h.shape
    qkv, z, a, bb, bq, bk, bv = split_cols(h @ ep['w_in'], EVEN_SPLITS)
    y_a, dn_fin = deltanet_pallas(qkv, z, a, bb, ep, jnp.zeros((b, 2, HA, DK, DK), F32))
    bq = bq.reshape(b, l, HB, HEAD_DIM)
    bk = bk.reshape(b, l, HKV_B, HEAD_DIM)
    bv = bv.reshape(b, l, HKV_B, HEAD_DIM)
    y_b = blocked_attention(bq, bk, bv, ep['sink'])
    y = jnp.concatenate([y_a, y_b.astype(y_a.dtype)], axis=-1) @ ep['w_out']
    return y, dn_fin, bk, bv


def even_mixer_latent(h, ep, dn_state, ctx_k, ctx_v, cos, sin):
    b, n, _ = h.shape
    qkv, z, a, bb, bq, bk, bv = split_cols(h @ ep['w_in'], EVEN_SPLITS)
    y_a, _ = deltanet_pallas(qkv, z, a, bb, ep, dn_state.astype(F32))
    bq = apply_axial_rope(bq.reshape(b, n, HB, HEAD_DIM), cos, sin)
    bk = apply_axial_rope(bk.reshape(b, n, HKV_B, HEAD_DIM), cos, sin)
    bv = bv.reshape(b, n, HKV_B, HEAD_DIM)
    y_b = windowed_attention(bq, bk, bv, ctx_k, ctx_v, ep['sink'])
    return jnp.concatenate([y_a, y_b.astype(y_a.dtype)], axis=-1) @ ep['w_out']


def odd_mixer_context(h, op):
    b, l, _ = h.shape
    cq, ck, cv, u = split_cols(h @ op['w_in'], ODD_SPLITS)
    cq = rms_norm(cq.reshape(b, l, HC, HEAD_DIM), op['q_norm'])
    ck = rms_norm(ck.reshape(b, l, HKV_C, HEAD_DIM), op['k_norm'])
    cv = cv.reshape(b, l, HKV_C, HEAD_DIM)
    y_c = blocked_attention(cq, ck, cv)
    y_d, s_fin = s5_pallas(u, op, s5_operators(op), jnp.zeros((2, b, 2 * S5_STATE), F32))
    y = jnp.concatenate([y_c, y_d.astype(y_c.dtype)], axis=-1) @ op['w_out']
    s_fin = jnp.transpose(s_fin.reshape(2, b, 2, G_D, S5_P), (1, 0, 3, 4, 2))
    return y, ck, cv, s_fin


def odd_mixer_latent(h, op, ctx_k, ctx_v, s5_state, cos, sin):
    b, n, _ = h.shape
    cq, ck, cv, u = split_cols(h @ op['w_in'], ODD_SPLITS)
    cq = apply_axial_rope(rms_norm(cq.reshape(b, n, HC, HEAD_DIM), op['q_norm']), cos, sin)
    ck = apply_axial_rope(rms_norm(ck.reshape(b, n, HKV_C, HEAD_DIM), op['k_norm']), cos, sin)
    cv = cv.reshape(b, n, HKV_C, HEAD_DIM)
    keys = jnp.concatenate([ck, ctx_k.astype(ck.dtype)], axis=1)
    vals = jnp.concatenate([cv, ctx_v.astype(cv.dtype)], axis=1)
    y_c = blocked_attention(cq, keys, vals)
    s0 = jnp.transpose(s5_state.astype(F32), (1, 0, 4, 2, 3)).reshape(2, b, 2 * S5_STATE)
    y_d, _ = s5_pallas(u, op, s5_operators(op), s0)
    return jnp.concatenate([y_c, y_d.astype(y_c.dtype)], axis=-1) @ op['w_out']


def hier_moe(x, mp):
    shp = x.shape
    t = x.reshape(-1, D_MODEL)
    lg = (t @ mp['rg'] + mp['rg_b']).astype(F32)
    pg = jax.nn.softmax(lg, axis=-1)
    gidx = jnp.argmax(lg, axis=-1)
    gate_g = jnp.take_along_axis(pg, gidx[:, None], axis=-1)
    le = (t @ mp['re'] + mp['re_b']).astype(F32).reshape(-1, N_GROUPS, EXP_PER_GROUP)
    le = jnp.take_along_axis(le, gidx[:, None, None], axis=1)[:, 0]
    top_v, top_i = lax.top_k(le, TOP_K)
    pe = jax.nn.softmax(top_v, axis=-1) * gate_g
    eid = gidx[:, None] * EXP_PER_GROUP + top_i
    combine = jnp.sum(jax.nn.one_hot(eid, N_EXP, dtype=F32) * pe[..., None], axis=1)
    hid = jax.nn.silu(t @ mp['wg']) * (t @ mp['wu'])
    hid = hid.reshape(-1, N_EXP, EXPERT_FF) * combine[..., None].astype(t.dtype)
    y = hid.reshape(-1, N_EXP * EXPERT_FF) @ mp['wd']
    return y.reshape(shp)


def _final_norm_kernel(x_ref, w_ref, o_ref):
    x = x_ref[...]
    y = x * lax.rsqrt(jnp.mean(x * x, axis=-1, keepdims=True) + EPS)
    o_ref[...] = y * w_ref[...]


def final_norm(x, w):
    t, d = x.shape
    tm = 512
    return pl.pallas_call(
        _final_norm_kernel,
        grid=(t // tm,),
        in_specs=[pl.BlockSpec((tm, d), lambda i: (i, 0)), pl.BlockSpec((1, d), lambda i: (0, 0))],
        out_specs=pl.BlockSpec((tm, d), lambda i: (i, 0)),
        out_shape=jax.ShapeDtypeStruct((t, d), F32),
        name="final_norm",
    )(x, w.reshape(1, d))


def kernel(x_prompt, x_sample, c, c_ctx, state_dn, cache_b_k, cache_b_v, cache_c_k, cache_c_v, state_s5,
           ada_w, ada_b, norm_w, w_in_e, dn_conv_w, dn_a_log, dn_dt_bias, dn_norm_w, b_sink, w_out_e,
           w_in_o, c_q_norm, c_k_norm, s5_lam_re, s5_lam_im, s5_log_dt, s5_b_re, s5_b_im, s5_c_re, s5_c_im,
           s5_d, s5_glu_w, s5_glu_b, w_out_o, moe_rg, moe_rg_b, moe_re, moe_re_b, moe_wg, moe_wu, moe_wd,
           final_norm_w):
    n_lat = x_sample.shape[1]
    rows = n_lat // GRID_W
    cos, sin = axial_rope_tables(rows)
    xp, xs = x_prompt, x_sample
    new_dn, new_bk, new_bv, new_ck, new_cv, new_s5 = [], [], [], [], [], []
    for layer in range(DEPTH):
        mod_p = ada_modulation(c_ctx[None, :], ada_w[layer], ada_b[layer])
        mod_s = ada_modulation(c, ada_w[layer], ada_b[layer])
        hp = modulate(xp, norm_w[layer, 0], mod_p[0], mod_p[1])
        hs = modulate(xs, norm_w[layer, 0], mod_s[0], mod_s[1])
        if layer % 2 == 0:
            i = layer // 2
            ep = {'w_in': w_in_e[i], 'conv_w': dn_conv_w[i], 'a_log': dn_a_log[i], 'dt_bias': dn_dt_bias[i],
                  'norm_w': dn_norm_w[i], 'sink': b_sink[i], 'w_out': w_out_e[i]}
            yp, dn, bk, bv = even_mixer_context(hp, ep)
            ys = even_mixer_latent(hs, ep, state_dn[:, i], cache_b_k[:, i], cache_b_v[:, i], cos, sin)
            new_dn.append(dn)
            new_bk.append(bk)
            new_bv.append(bv)
        else:
            i = layer // 2
            op = {'w_in': w_in_o[i], 'q_norm': c_q_norm[i], 'k_norm': c_k_norm[i], 'lam_re': s5_lam_re[i],
                  'lam_im': s5_lam_im[i], 'log_dt': s5_log_dt[i], 'b_re': s5_b_re[i], 'b_im': s5_b_im[i],
                  'c_re': s5_c_re[i], 'c_im': s5_c_im[i], 'd': s5_d[i], 'glu_w': s5_glu_w[i],
                  'glu_b': s5_glu_b[i], 'w_out': w_out_o[i]}
            yp, ck, cv, s5 = odd_mixer_context(hp, op)
            ys = odd_mixer_latent(hs, op, cache_c_k[:, i], cache_c_v[:, i], state_s5[:, i], cos, sin)
            new_ck.append(ck)
            new_cv.append(cv)
            new_s5.append(s5)
        xp = xp + mod_p[2] * yp
        xs = xs + mod_s[2] * ys
        mp = {'rg': moe_rg[layer], 'rg_b': moe_rg_b[layer], 're': moe_re[layer], 're_b': moe_re_b[layer],
              'wg': moe_wg[layer], 'wu': moe_wu[layer], 'wd': moe_wd[layer]}
        xp = xp + mod_p[5] * hier_moe(modulate(xp, norm_w[layer, 1], mod_p[3], mod_p[4]), mp)
        xs = xs + mod_s[5] * hier_moe(modulate(xs, norm_w[layer, 1], mod_s[3], mod_s[4]), mp)
    y_prompt = final_norm(xp.reshape(-1, D_MODEL), final_norm_w).reshape(xp.shape)
    y_sample = final_norm(xs.reshape(-1, D_MODEL), final_norm_w).reshape(xs.shape)
    return (y_prompt, y_sample, jnp.stack(new_dn, axis=1), jnp.stack(new_bk, axis=1), jnp.stack(new_bv, axis=1),
            jnp.stack(new_ck, axis=1), jnp.stack(new_cv, axis=1), jnp.stack(new_s5, axis=1))
```

```python
import functools

import numpy as np
import jax
import jax.numpy as jnp
from jax import lax
from jax.experimental import pallas as pl
from jax.experimental.pallas import tpu as pltpu

D_MODEL = 1024
BATCH = 32
SEQ = 256
DEPTH = 2
DEC_BATCH = 8
DEC_SEQ = 2048
PAST_LEN = 256

GRID_W = 64
HEAD_DIM = 64
BLK = 128
ROPE_THETA = 10000.0
EPS = 1e-6
N_EVEN = (DEPTH + 1) // 2
N_ODD = DEPTH // 2
HA = D_MODEL // 128
DK = HEAD_DIM
CONV_W = 5
DN_CHUNK = 64
HB = D_MODEL // 128
HKV_B = HB // 4
WINDOW = 128
HC = (3 * D_MODEL // 4) // HEAD_DIM
HKV_C = HC // 3
D_S5 = D_MODEL // 4
S5_CH = 16
G_D = D_S5 // S5_CH
S5_P = 64
N_GROUPS = 4
EXP_PER_GROUP = 8
N_EXP = N_GROUPS * EXP_PER_GROUP
EXPERT_FF = D_MODEL // 8
TOP_K = 2

EVEN_SPLITS = (3 * HA * DK, HA * DK, 2 * HA, 2 * HA, HB * HEAD_DIM, HKV_B * HEAD_DIM, HKV_B * HEAD_DIM)
D_IN_EVEN = sum(EVEN_SPLITS)
D_MIX_EVEN = HA * DK + HB * HEAD_DIM
ODD_SPLITS = (HC * HEAD_DIM, HKV_C * HEAD_DIM, HKV_C * HEAD_DIM, D_S5)
D_IN_ODD = sum(ODD_SPLITS)
D_MIX_ODD = HC * HEAD_DIM + D_S5

F32 = jnp.float32
BF16 = jnp.bfloat16


def split_cols(x, sizes):
    idx = np.cumsum(sizes)[:-1].tolist()
    return jnp.split(x, idx, axis=-1)


def rms_norm(x, w):
    x32 = x.astype(F32)
    y = x32 * lax.rsqrt(jnp.mean(x32 * x32, axis=-1, keepdims=True) + EPS)
    return (y * w.astype(F32)).astype(x.dtype)


def axial_rope_tables(rows):
    row = jnp.repeat(jnp.arange(rows), GRID_W).astype(F32)
    col = jnp.tile(jnp.arange(GRID_W), rows).astype(F32)
    quarter = HEAD_DIM // 4
    freqs = ROPE_THETA ** (-jnp.arange(quarter, dtype=F32) / quarter)
    ang_r = row[:, None] * freqs
    ang_c = col[:, None] * freqs
    ang = jnp.concatenate([ang_r, ang_r, ang_c, ang_c], axis=-1)
    return jnp.cos(ang), jnp.sin(ang)


def apply_axial_rope(x, cos, sin):
    q4 = HEAD_DIM // 4
    half = HEAD_DIM // 2

    def rot(u):
        return jnp.concatenate([-u[..., q4:], u[..., :q4]], axis=-1)

    xrot = jnp.concatenate([rot(x[..., :half]), rot(x[..., half:])], axis=-1)
    return (x * cos[:, None, :] + xrot * sin[:, None, :]).astype(x.dtype)


def blocked_attention(q, k, v, sink=None):
    b, lq, h, d = q.shape
    hk = k.shape[2]
    grp = h // hk
    nb = lq // BLK
    qb = jnp.moveaxis(q.reshape(b, nb, BLK, hk, grp, d), 1, 0)
    scale = d ** -0.5

    def one_block(qblk):
        s = jnp.einsum('bqhgd,bkhd->bhgqk', qblk, k).astype(F32) * scale
        if sink is not None:
            s_sink = jnp.broadcast_to(sink.astype(F32).reshape(1, hk, grp, 1, 1), s.shape[:-1] + (1,))
            p = jax.nn.softmax(jnp.concatenate([s, s_sink], axis=-1), axis=-1)[..., :-1]
        else:
            p = jax.nn.softmax(s, axis=-1)
        return jnp.einsum('bhgqk,bkhd->bqhgd', p.astype(v.dtype), v)

    o = lax.map(one_block, qb)
    return jnp.moveaxis(o, 0, 1).reshape(b, lq, h * d)


def windowed_attention(q, k, v, kc, vc, sink):
    b, n, h, d = q.shape
    hk = k.shape[2]
    grp = h // hk
    nb = n // BLK
    pad = ((0, 0), (BLK, BLK), (0, 0), (0, 0))

    def windows(t):
        t = jnp.pad(t, pad).reshape(b, nb + 2, BLK, hk, d)
        return jnp.concatenate([t[:, :-2], t[:, 1:-1], t[:, 2:]], axis=2)

    kw, vw = windows(k), windows(v)
    qb = q.reshape(b, nb, BLK, hk, grp, d)
    scale = d ** -0.5
    s_loc = jnp.einsum('bnqhgd,bnkhd->bnhgqk', qb, kw.astype(q.dtype)).astype(F32) * scale
    qi = jnp.arange(BLK)[:, None] + BLK
    kj = jnp.arange(3 * BLK)[None, :]
    kabs = (jnp.arange(nb)[:, None, None] - 1) * BLK + kj
    mask = (jnp.abs(qi - kj) <= WINDOW)[None] & (kabs >= 0) & (kabs < n)
    s_loc = jnp.where(mask[None, :, None, None], s_loc, -jnp.inf)
    s_ctx = jnp.einsum('bnqhgd,bkhd->bnhgqk', qb, kc.astype(q.dtype)).astype(F32) * scale
    s_sink = jnp.broadcast_to(sink.astype(F32).reshape(1, 1, hk, grp, 1, 1), s_loc.shape[:-1] + (1,))
    p = jax.nn.softmax(jnp.concatenate([s_loc, s_ctx, s_sink], axis=-1), axis=-1)
    p_loc = p[..., :3 * BLK].astype(v.dtype)
    p_ctx = p[..., 3 * BLK:-1].astype(v.dtype)
    o = (jnp.einsum('bnhgqk,bnkhd->bnqhgd', p_loc, vw)
         + jnp.einsum('bnhgqk,bkhd->bnqhgd', p_ctx, vc.astype(v.dtype)))
    return o.reshape(b, n, h * d)


SUBLANES = 8
VMEM_LIMIT = 56 * 1024 * 1024
S5_ROWS = 1024
S5_STATE = G_D * S5_P


def _s5_scan_kernel(u_ref, bmat_ref, cmat_ref, a_ref, s0_ref, y_ref, sfin_ref, xs_ref, st_ref, *, nblk, bsz):
    dr = pl.program_id(0)
    blk = pl.program_id(1)

    @pl.when(blk == 0)
    def _():
        st_ref[...] = s0_ref[0]

    xs_ref[...] = jnp.dot(u_ref[...].astype(BF16), bmat_ref[0], preferred_element_type=F32)
    a_re = jnp.broadcast_to(a_ref[0, 0:1, :], (SUBLANES, S5_STATE))
    a_im = jnp.broadcast_to(a_ref[0, 1:2, :], (SUBLANES, S5_STATE))
    steps = S5_ROWS // bsz
    for sg in range(bsz // SUBLANES):
        rows = pl.ds(sg * SUBLANES, SUBLANES)

        def body(i, carry, sg=sg):
            x_re, x_im = carry
            l = i + dr * (steps - 1 - 2 * i)
            r = pl.ds(pl.multiple_of(l * bsz + sg * SUBLANES, SUBLANES), SUBLANES)
            n_re = a_re * x_re - a_im * x_im + xs_ref[r, 0:S5_STATE]
            n_im = a_re * x_im + a_im * x_re + xs_ref[r, S5_STATE:2 * S5_STATE]
            xs_ref[r, 0:S5_STATE] = n_re
            xs_ref[r, S5_STATE:2 * S5_STATE] = n_im
            return n_re, n_im

        x_re, x_im = lax.fori_loop(0, steps, body, (st_ref[rows, 0:S5_STATE], st_ref[rows, S5_STATE:2 * S5_STATE]),
                                   unroll=4)
        st_ref[rows, 0:S5_STATE] = x_re
        st_ref[rows, S5_STATE:2 * S5_STATE] = x_im
    y_ref[0] = jnp.dot(xs_ref[...].astype(BF16), cmat_ref[0], preferred_element_type=F32)

    @pl.when(blk == nblk - 1)
    def _():
        sfin_ref[0] = st_ref[...]


def _s5_out_kernel(yf_ref, yb_ref, u_ref, d_ref, w_ref, b_ref, o_ref):
    y = yf_ref[0] + yb_ref[0] + u_ref[...] * d_ref[...]
    y = jax.nn.gelu(y)
    gate = jnp.dot(y.astype(BF16), w_ref[...], preferred_element_type=F32) + b_ref[...]
    o_ref[...] = y * jax.nn.sigmoid(gate)


def s5_operators(op):
    lam_re, lam_im = op['lam_re'].astype(F32), op['lam_im'].astype(F32)
    dt = jnp.exp(op['log_dt'].astype(F32))[..., None]
    mag = jnp.exp(lam_re * dt)
    a_re, a_im = mag * jnp.cos(lam_im * dt), mag * jnp.sin(lam_im * dt)
    den = lam_re * lam_re + lam_im * lam_im
    k_re = ((a_re - 1) * lam_re + a_im * lam_im) / den
    k_im = (a_im * lam_re - (a_re - 1) * lam_im) / den
    b_re, b_im = op['b_re'].astype(F32), op['b_im'].astype(F32)
    bb_re = k_re[..., None] * b_re - k_im[..., None] * b_im
    bb_im = k_re[..., None] * b_im + k_im[..., None] * b_re
    eye = jnp.eye(G_D, dtype=F32)

    def b_blocks(t):
        return jnp.einsum('dgpc,gh->dgchp', t, eye).reshape(2, D_S5, S5_STATE)

    def c_blocks(t):
        return jnp.einsum('dgcp,gh->dgphc', t, eye).reshape(2, S5_STATE, D_S5)

    b_blk = jnp.concatenate([b_blocks(bb_re), b_blocks(bb_im)], axis=-1).astype(BF16)
    c_blk = jnp.concatenate([c_blocks(op['c_re'].astype(F32)), -c_blocks(op['c_im'].astype(F32))],
                            axis=1).astype(BF16)
    a_vec = jnp.stack([a_re.reshape(2, S5_STATE), a_im.reshape(2, S5_STATE)], axis=1)
    return b_blk, c_blk, a_vec


def s5_pallas(u, op, ops, s0):
    b, l, _ = u.shape
    b_blk, c_blk, a_vec = ops
    n = b * l
    nblk = n // S5_ROWS
    u_t = jnp.swapaxes(u, 0, 1).reshape(n, D_S5)

    def rows_map(dr, blk):
        return (blk + dr * (nblk - 1 - 2 * blk), 0)

    y2, s_fin = pl.pallas_call(
        functools.partial(_s5_scan_kernel, nblk=nblk, bsz=b),
        grid=(2, nblk),
        in_specs=[
            pl.BlockSpec((S5_ROWS, D_S5), rows_map),
            pl.BlockSpec((1, D_S5, 2 * S5_STATE), lambda dr, blk: (dr, 0, 0)),
            pl.BlockSpec((1, 2 * S5_STATE, D_S5), lambda dr, blk: (dr, 0, 0)),
            pl.BlockSpec((1, 2, S5_STATE), lambda dr, blk: (dr, 0, 0)),
            pl.BlockSpec((1, b, 2 * S5_STATE), lambda dr, blk: (dr, 0, 0)),
        ],
        out_specs=[
            pl.BlockSpec((1, S5_ROWS, D_S5), lambda dr, blk: (dr,) + rows_map(dr, blk)),
            pl.BlockSpec((1, b, 2 * S5_STATE), lambda dr, blk: (dr, 0, 0)),
        ],
        out_shape=[jax.ShapeDtypeStruct((2, n, D_S5), F32), jax.ShapeDtypeStruct((2, b, 2 * S5_STATE), F32)],
        scratch_shapes=[pltpu.VMEM((S5_ROWS, 2 * S5_STATE), F32), pltpu.VMEM((b, 2 * S5_STATE), F32)],
        compiler_params=pltpu.CompilerParams(dimension_semantics=("arbitrary", "arbitrary"),
                                             vmem_limit_bytes=VMEM_LIMIT),
        name="s5_scan",
    )(u_t, b_blk, c_blk, a_vec, s0)
    tm = S5_ROWS
    y = pl.pallas_call(
        _s5_out_kernel,
        grid=(n // tm,),
        in_specs=[
            pl.BlockSpec((1, tm, D_S5), lambda i: (0, i, 0)),
            pl.BlockSpec((1, tm, D_S5), lambda i: (1, i, 0)),
            pl.BlockSpec((tm, D_S5), lambda i: (i, 0)),
            pl.BlockSpec((1, D_S5), lambda i: (0, 0)),
            pl.BlockSpec((D_S5, D_S5), lambda i: (0, 0)),
            pl.BlockSpec((1, D_S5), lambda i: (0, 0)),
        ],
        out_specs=pl.BlockSpec((tm, D_S5), lambda i: (i, 0)),
        out_shape=jax.ShapeDtypeStruct((n, D_S5), F32),
        name="s5_out",
    )(y2, y2, u_t, op['d'].astype(F32).reshape(1, D_S5), op['glu_w'].astype(BF16),
      op['glu_b'].astype(F32).reshape(1, D_S5))
    return jnp.swapaxes(y.reshape(l, b, D_S5), 0, 1), s_fin


LANES = 128
DN_PAIRS = HA * DK // LANES
NEG_BIG = -1e30
DN_UNROLL = 2


def _nt_dot(a, b):
    return lax.dot_general(a, b, (((1,), (1,)), ((), ())), preferred_element_type=F32)


def _bf16_dot(a, b):
    return jnp.dot(a.astype(BF16), b.astype(BF16), preferred_element_type=F32)


def _select_dot(x, sel):
    x1 = x.astype(BF16)
    r1 = x - x1.astype(F32)
    x2 = r1.astype(BF16)
    x3 = (r1 - x2.astype(F32)).astype(BF16)
    dot = functools.partial(jnp.dot, preferred_element_type=F32)
    return dot(x1, sel) + dot(x2, sel) + dot(x3, sel)


def _head_sums(x):
    lane = lax.broadcasted_iota(jnp.int32, x.shape, 1)
    lo = lane < HEAD_DIM
    s0 = jnp.sum(jnp.where(lo, x, 0.0), axis=-1, keepdims=True)
    s1 = jnp.sum(jnp.where(lo, 0.0, x), axis=-1, keepdims=True)
    return jnp.where(lo, s0, s1)


def _chunk_cumsum(x, reverse):
    n = x.shape[0]
    pos = lax.broadcasted_iota(jnp.int32, x.shape, 0) % DN_CHUNK
    s = 1
    while s < DN_CHUNK:
        if reverse:
            x = x + jnp.where(pos < DN_CHUNK - s, pltpu.roll(x, n - s, 0), 0.0)
        else:
            x = x + jnp.where(pos >= s, pltpu.roll(x, s, 0), 0.0)
        s *= 2
    return x


def _short_conv_silu(x, w):
    n = x.shape[0]
    row = lax.broadcasted_iota(jnp.int32, x.shape, 0)
    pad = CONV_W // 2
    y = x * w[pad:pad + 1, :]
    for t in range(CONV_W):
        s = t - pad
        if s == 0:
            continue
        shifted = pltpu.roll(x, (-s) % n, 0)
        valid = (row + s >= 0) & (row + s < n)
        y = y + jnp.where(valid, shifted, 0.0) * w[t:t + 1, :]
    return jax.nn.silu(y)


def _unit_tri_solve(ms, rhss, ri, ci):
    n = rhss[0].shape[-1]
    same16 = (ri // 16) == (ci // 16)
    dot = functools.partial(jnp.dot, preferred_element_type=F32)

    def apply(p, r):
        hi = r.astype(BF16)
        lo = (r - hi.astype(F32)).astype(BF16)
        x = dot(p, jnp.concatenate([hi, lo], axis=-1))
        return x[:, 0:n] + x[:, n:2 * n]

    diag = [jnp.where(same16, m, 0.0) for m in ms]
    rest = [jnp.where(same16, 0.0, m).astype(BF16) for m in ms]
    pows = [d.astype(BF16) for d in diag]
    tinv = [jnp.where(ri == ci, 1.0, 0.0) - d for d in diag]
    for _ in range(3):
        pows = [dot(p, p).astype(BF16) for p in pows]
        tinv = [t + dot(t.astype(BF16), p) for t, p in zip(tinv, pows)]
    tinv = [t.astype(BF16) for t in tinv]
    p1 = [dot(t, e).astype(BF16) for t, e in zip(tinv, rest)]
    sols = [apply(t, r) for t, r in zip(tinv, rhss)]
    p2 = [dot(p, p).astype(BF16) for p in p1]
    sols = [s - apply(p, s) for p, s in zip(p1, sols)]
    return [s + apply(p, s) for p, s in zip(p2, sols)]


def _deltanet_kernel(q_ref, k_ref, v_ref, z_ref, a_ref, bb_ref, cwq_ref, cwk_ref, cwv_ref, alog_ref, dtb_ref, nw_ref,
                     s0_ref, y_ref, sfin_ref,
                     qs_ref, ks_ref, vs_ref, o_ref, gc_ref, beta_ref, u_ref, w_ref, attn_ref, qp_ref, kpt_ref,
                     etot_ref, *, seq):
    hp = pl.program_id(1)
    nchunk = seq // DN_CHUNK
    c64 = DN_CHUNK

    q = _short_conv_silu(q_ref[0], cwq_ref[...])
    qs_ref[...] = q * lax.rsqrt(_head_sums(q * q) + EPS) * (DK ** -0.5)
    k = _short_conv_silu(k_ref[0], cwk_ref[...])
    ks_ref[...] = k * lax.rsqrt(_head_sums(k * k) + EPS)
    vs_ref[...] = _short_conv_silu(v_ref[0], cwv_ref[...])
    o_ref[...] = jnp.zeros_like(o_ref)

    g_col = -jnp.exp(alog_ref[...]) * jax.nn.softplus(a_ref[0] + dtb_ref[...])
    b_col = jax.nn.sigmoid(bb_ref[0])
    sel_row = lax.broadcasted_iota(jnp.int32, (2 * HA, 2 * LANES), 0)
    sel_lane = lax.broadcasted_iota(jnp.int32, (2 * HA, 2 * LANES), 1)
    sel = (sel_row == (sel_lane // LANES) * HA + 2 * hp + (sel_lane // HEAD_DIM) % 2).astype(BF16)
    g_b = _select_dot(g_col, sel)
    gc_ref[:, 0:LANES] = _chunk_cumsum(g_b[:, 0:LANES], reverse=False)
    gc_ref[:, LANES:2 * LANES] = _chunk_cumsum(g_b[:, LANES:2 * LANES], reverse=True)
    beta_ref[...] = _select_dot(b_col, sel)

    ri = lax.broadcasted_iota(jnp.int32, (c64, c64), 0)
    ci = lax.broadcasted_iota(jnp.int32, (c64, c64), 1)

    def intra_chunk(step, carry):
        chunk_ids = [step * DN_UNROLL + t for t in range(DN_UNROLL)]
        rows_c = [pl.ds(pl.multiple_of(c * c64, c64), c64) for c in chunk_ids]
        loads = [(qs_ref[r, :], ks_ref[r, :], vs_ref[r, :], gc_ref[r, :], beta_ref[r, :]) for r in rows_c]
        heads = []
        for t, (q2, k2, v2, _, _) in enumerate(loads):
            for j in range(2):
                hl = slice(j * HEAD_DIM, (j + 1) * HEAD_DIM)
                heads.append((t, j, q2[:, hl], k2[:, hl], v2[:, hl]))
        kbs = [h[3].astype(BF16) for h in heads]
        kks = [_nt_dot(kb, kb) for kb in kbs]
        qks = [_nt_dot(h[2].astype(BF16), kb) for h, kb in zip(heads, kbs)]
        chains, ms, rhss = [], [], []
        for (t, j, qj, kj, vj), kk, qk in zip(heads, kks, qks):
            for d in range(2):
                src = slice(d * LANES + j * HEAD_DIM, d * LANES + (j + 1) * HEAD_DIM)
                gcb = loads[t][3][:, src]
                bet = loads[t][4][:, src]
                diff = gcb - gcb.T
                earlier = (ri >= ci) if d == 0 else (ri <= ci)
                strict = (ri > ci) if d == 0 else (ri < ci)
                dec = jnp.exp(jnp.where(earlier, diff, NEG_BIG))
                eg = jnp.exp(gcb)
                gtot = gcb[c64 - 1:c64, :] if d == 0 else gcb[0:1, :]
                ms.append(jnp.where(strict, kk * bet * dec, 0.0))
                rhss.append(jnp.concatenate([vj * bet, kj * bet * eg], axis=-1))
                chains.append((t, 2 * j + d, qk * dec, qj * eg, (kj * jnp.exp(gtot - gcb)).T,
                               jnp.broadcast_to(jnp.exp(gtot), (SUBLANES, HEAD_DIM))))
        uws = _unit_tri_solve(ms, rhss, ri, ci)
        for t in range(DN_UNROLL):
            slots = [[None] * 4 for _ in range(6)]
            for (tc, lane_slot, attn, qp, kpt, etot), uw in zip(chains, uws):
                if tc == t:
                    vals = (uw[:, 0:HEAD_DIM], uw[:, HEAD_DIM:2 * HEAD_DIM], attn, qp, kpt, etot)
                    for slot, val in zip(slots, vals):
                        slot[lane_slot] = val
            for ref, slot in zip((u_ref, w_ref, attn_ref, qp_ref, kpt_ref), slots[:5]):
                ref[rows_c[t], :] = jnp.concatenate(slot, axis=-1)
            etot_ref[pl.ds(pl.multiple_of(chunk_ids[t] * SUBLANES, SUBLANES), SUBLANES), :] = jnp.concatenate(
                slots[5], axis=-1)
        return carry

    lax.fori_loop(0, nchunk // DN_UNROLL, intra_chunk, 0)

    def inter_chunk(c, states):
        rows_d = [pl.ds(pl.multiple_of(cc * c64, c64), c64) for cc in (c, nchunk - 1 - c)]
        erow_d = [pl.ds(pl.multiple_of(cc * SUBLANES, SUBLANES), SUBLANES) for cc in (c, nchunk - 1 - c)]
        loaded = []
        for j in range(2):
            for d in range(2):
                dst = slice((2 * j + d) * HEAD_DIM, (2 * j + d + 1) * HEAD_DIM)
                loaded.append((w_ref[rows_d[d], dst], qp_ref[rows_d[d], dst], u_ref[rows_d[d], dst],
                               attn_ref[rows_d[d], dst], kpt_ref[rows_d[d], dst], etot_ref[erow_d[d], dst]))
        o_prev = [o_ref[rows_d[0], :], o_ref[rows_d[1], :]]
        ws_qs = [jnp.dot(jnp.concatenate([ld[0], ld[1]], axis=0).astype(BF16), s.astype(BF16),
                         preferred_element_type=F32) for ld, s in zip(loaded, states)]
        v_new = [(ld[2] - r[0:c64]).astype(BF16) for ld, r in zip(loaded, ws_qs)]
        av = [jnp.dot(ld[3].astype(BF16), v, preferred_element_type=F32) for ld, v in zip(loaded, v_new)]
        kv = [jnp.dot(ld[4].astype(BF16), v, preferred_element_type=F32) for ld, v in zip(loaded, v_new)]
        new_states = [s * ld[5][0:1, :] + x for s, ld, x in zip(states, loaded, kv)]
        o_new = [[None, None], [None, None]]
        for j in range(2):
            for d in range(2):
                o_new[d][j] = ws_qs[2 * j + d][c64:2 * c64] + av[2 * j + d]
        for d in range(2):
            o_ref[rows_d[d], :] = o_prev[d] + jnp.concatenate(o_new[d], axis=-1)
        return tuple(new_states)

    init = tuple(s0_ref[0, d, j] for j in range(2) for d in range(2))
    fin = lax.fori_loop(0, nchunk, inter_chunk, init)
    for j in range(2):
        for d in range(2):
            sfin_ref[0, d, j] = fin[2 * j + d]

    o = o_ref[...]
    o = o * lax.rsqrt(_head_sums(o * o) * (1.0 / DK) + EPS) * nw_ref[...]
    y_ref[0] = o * jax.nn.silu(z_ref[0])


def deltanet_pallas(qkv, z, a, bb, ep, s0):
    b, l, _ = qkv.shape
    npair = DN_PAIRS

    def col_spec(off):
        return pl.BlockSpec((1, l, LANES), lambda i, hp: (i, 0, off + hp))

    def cw_spec(off):
        return pl.BlockSpec((CONV_W, LANES), lambda i, hp: (0, off + hp))

    def full(shape):
        return pl.BlockSpec(shape, lambda i, hp: (0,) * len(shape))

    state_spec = pl.BlockSpec((1, 2, 2, DK, DK), lambda i, hp: (i, 0, hp, 0, 0))
    row_f32 = pltpu.VMEM((l, LANES), F32)
    chain_f32 = pltpu.VMEM((l, 2 * LANES), F32)
    y, s_fin = pl.pallas_call(
        functools.partial(_deltanet_kernel, seq=l),
        grid=(b, npair),
        in_specs=[col_spec(0), col_spec(npair), col_spec(2 * npair), col_spec(0),
                  pl.BlockSpec((1, l, 2 * HA), lambda i, hp: (i, 0, 0)),
                  pl.BlockSpec((1, l, 2 * HA), lambda i, hp: (i, 0, 0)),
                  cw_spec(0), cw_spec(npair), cw_spec(2 * npair),
                  full((1, 2 * HA)), full((1, 2 * HA)), full((1, LANES)), state_spec],
        out_specs=[col_spec(0), state_spec],
        out_shape=[jax.ShapeDtypeStruct((b, l, HA * DK), F32), jax.ShapeDtypeStruct((b, 2, HA, DK, DK), F32)],
        scratch_shapes=[row_f32, row_f32, row_f32, row_f32, chain_f32, chain_f32,
                        chain_f32, chain_f32, chain_f32, chain_f32, chain_f32,
                        pltpu.VMEM((l // DN_CHUNK * SUBLANES, 2 * LANES), F32)],
        compiler_params=pltpu.CompilerParams(dimension_semantics=("arbitrary", "arbitrary"),
                                             vmem_limit_bytes=VMEM_LIMIT),
        name="deltanet",
    )(qkv, qkv, qkv, z, a, bb, ep['conv_w'], ep['conv_w'], ep['conv_w'],
      ep['a_log'].astype(F32).reshape(1, 2 * HA), ep['dt_bias'].astype(F32).reshape(1, 2 * HA),
      jnp.tile(ep['norm_w'].astype(F32), 2).reshape(1, LANES), s0)
    return y, s_fin


def even_mixer_context(proj, ep):
    qkv, z, ab, bq, bk, bv = [t.reshape(BATCH, SEQ, -1) for t in proj]
    a, bb = ab[..., 0:2 * HA], ab[..., 2 * HA:4 * HA]
    y_a, dn_fin = deltanet_pallas(qkv, z, a, bb, ep, jnp.zeros((BATCH, 2, HA, DK, DK), F32))
    bk = bk.reshape(BATCH, SEQ, HKV_B, HEAD_DIM)
    bv = bv.reshape(BATCH, SEQ, HKV_B, HEAD_DIM)
    y_b = blocked_attention(bq.reshape(BATCH, SEQ, HB, HEAD_DIM), bk, bv, ep['sink'])
    return y_a.reshape(-1, HA * DK), y_b.reshape(-1, HB * HEAD_DIM), dn_fin, bk, bv


def even_mixer_latent(proj, ep, dn_state, ctx_k, ctx_v, cos, sin):
    qkv, z, ab, bq, bk, bv = [t.reshape(DEC_BATCH, DEC_SEQ, -1) for t in proj]
    a, bb = ab[..., 0:2 * HA], ab[..., 2 * HA:4 * HA]
    y_a, _ = deltanet_pallas(qkv, z, a, bb, ep, dn_state.astype(F32))
    bq = apply_axial_rope(bq.reshape(DEC_BATCH, DEC_SEQ, HB, HEAD_DIM), cos, sin)
    bk = apply_axial_rope(bk.reshape(DEC_BATCH, DEC_SEQ, HKV_B, HEAD_DIM), cos, sin)
    bv = bv.reshape(DEC_BATCH, DEC_SEQ, HKV_B, HEAD_DIM)
    y_b = windowed_attention(bq, bk, bv, ctx_k, ctx_v, ep['sink'])
    return y_a.reshape(-1, HA * DK), y_b.reshape(-1, HB * HEAD_DIM)


def odd_mixer_context(proj, op):
    cq, ck, cv, u = [t.reshape(BATCH, SEQ, -1) for t in proj]
    cq = rms_norm(cq.reshape(BATCH, SEQ, HC, HEAD_DIM), op['q_norm'])
    ck = rms_norm(ck.reshape(BATCH, SEQ, HKV_C, HEAD_DIM), op['k_norm'])
    cv = cv.reshape(BATCH, SEQ, HKV_C, HEAD_DIM)
    y_c = blocked_attention(cq, ck, cv)
    y_d, s_fin = s5_pallas(u, op, s5_operators(op), jnp.zeros((2, BATCH, 2 * S5_STATE), F32))
    s_fin = jnp.transpose(s_fin.reshape(2, BATCH, 2, G_D, S5_P), (1, 0, 3, 4, 2))
    return y_c.reshape(-1, HC * HEAD_DIM), y_d.reshape(-1, D_S5), ck, cv, s_fin


def odd_mixer_latent(proj, op, ctx_k, ctx_v, s5_state, cos, sin):
    cq, ck, cv, u = [t.reshape(DEC_BATCH, DEC_SEQ, -1) for t in proj]
    cq = apply_axial_rope(rms_norm(cq.reshape(DEC_BATCH, DEC_SEQ, HC, HEAD_DIM), op['q_norm']), cos, sin)
    ck = apply_axial_rope(rms_norm(ck.reshape(DEC_BATCH, DEC_SEQ, HKV_C, HEAD_DIM), op['k_norm']), cos, sin)
    cv = cv.reshape(DEC_BATCH, DEC_SEQ, HKV_C, HEAD_DIM)
    keys = jnp.concatenate([ck, ctx_k.astype(ck.dtype)], axis=1)
    vals = jnp.concatenate([cv, ctx_v.astype(cv.dtype)], axis=1)
    y_c = blocked_attention(cq, keys, vals)
    s0 = jnp.transpose(s5_state.astype(F32), (1, 0, 4, 2, 3)).reshape(2, DEC_BATCH, 2 * S5_STATE)
    y_d, _ = s5_pallas(u, op, s5_operators(op), s0)
    return y_c.reshape(-1, HC * HEAD_DIM), y_d.reshape(-1, D_S5)


TM = 512
ADA_ROWS = 16
ADA_TN = 1024
MOD_ROWS = 8
ROUTER_LANES = LANES
MOE_FCHUNK = EXP_PER_GROUP * EXPERT_FF


def _ada_kernel(c_ref, w_ref, b_ref, o_ref):
    o_ref[0] = _bf16_dot(jax.nn.silu(c_ref[...]), w_ref[0]) + b_ref[0]


def ada_pallas(cvec, ada_w, ada_b):
    n = 6 * D_MODEL
    return pl.pallas_call(
        _ada_kernel,
        grid=(DEPTH, n // ADA_TN),
        in_specs=[pl.BlockSpec((ADA_ROWS, D_MODEL), lambda l, j: (0, 0)),
                  pl.BlockSpec((1, D_MODEL, ADA_TN), lambda l, j: (l, 0, j)),
                  pl.BlockSpec((1, 1, ADA_TN), lambda l, j: (l, 0, j))],
        out_specs=pl.BlockSpec((1, ADA_ROWS, ADA_TN), lambda l, j: (l, 0, j)),
        out_shape=jax.ShapeDtypeStruct((DEPTH, ADA_ROWS, n), F32),
        name="ada_modulation",
    )(cvec, ada_w, ada_b.reshape(DEPTH, 1, n))


def _modulated(x, nw, shift, scale):
    return x * lax.rsqrt(jnp.mean(x * x, axis=-1, keepdims=True) + EPS) * nw * (1 + scale) + shift


def _inproj_kernel(x_ref, mod_ref, nw_ref, *refs):
    nout = len(refs) // 2
    m = mod_ref[0]
    h = _modulated(x_ref[...], nw_ref[...], m[0:1], m[1:2]).astype(BF16)
    for w_ref, o_ref in zip(refs[:nout], refs[nout:]):
        o_ref[...] = jnp.dot(h, w_ref[...], preferred_element_type=F32)


def inproj_pallas(x, mod, nw, weights, rows_per_mod):
    t = x.shape[0]
    tiles_per_mod = rows_per_mod // TM
    return pl.pallas_call(
        _inproj_kernel,
        grid=(t // TM,),
        in_specs=[pl.BlockSpec((TM, D_MODEL), lambda i: (i, 0)),
                  pl.BlockSpec((1, MOD_ROWS, D_MODEL), lambda i: (i // tiles_per_mod, 0, 0)),
                  pl.BlockSpec((1, D_MODEL), lambda i: (0, 0))]
        + [pl.BlockSpec(w.shape, lambda i: (0, 0)) for w in weights],
        out_specs=[pl.BlockSpec((TM, w.shape[1]), lambda i: (i, 0)) for w in weights],
        out_shape=[jax.ShapeDtypeStruct((t, w.shape[1]), F32) for w in weights],
        compiler_params=pltpu.CompilerParams(dimension_semantics=("arbitrary",), vmem_limit_bytes=VMEM_LIMIT),
        name="modulated_in_proj",
    )(x, mod, nw.reshape(1, D_MODEL), *weights)


def _moe_kernel(x_ref, ya_ref, yb_ref, mod_ref, nw_ref, woa_ref, wob_ref, wr_ref, br_ref, wg_ref, wu_ref, wd_ref,
                o_ref, x1_ref, h_ref, comb_ref, acc_ref):
    f = pl.program_id(1)
    m = mod_ref[0]

    @pl.when(f == 0)
    def _():
        y = _bf16_dot(ya_ref[...], woa_ref[...]) + _bf16_dot(yb_ref[...], wob_ref[...])
        x1 = x_ref[...] + m[2:3] * y
        x1_ref[...] = x1
        h = _modulated(x1, nw_ref[...], m[3:4], m[4:5])
        h_ref[...] = h.astype(BF16)
        logits = jnp.dot(h, wr_ref[...], preferred_element_type=F32, precision=lax.Precision.HIGHEST) + br_ref[...]
        lane = lax.broadcasted_iota(jnp.int32, logits.shape, 1)
        lane_f = lane.astype(F32)
        is_group = lane < N_GROUPS
        lg = jnp.where(is_group, logits, -jnp.inf)
        g_max = jnp.max(lg, axis=-1, keepdims=True)
        g_idx = jnp.min(jnp.where(lg == g_max, lane_f, float(ROUTER_LANES)), axis=-1, keepdims=True)
        gate_g = 1.0 / jnp.sum(jnp.where(is_group, jnp.exp(logits - g_max), 0.0), axis=-1, keepdims=True)
        lane_group = ((lane + (EXP_PER_GROUP - N_GROUPS)) // EXP_PER_GROUP - 1).astype(F32)
        in_group = (lane >= N_GROUPS) & (lane < N_GROUPS + N_EXP) & (lane_group == g_idx)
        le = jnp.where(in_group, logits, -jnp.inf)
        v1 = jnp.max(le, axis=-1, keepdims=True)
        i1 = jnp.min(jnp.where(le == v1, lane_f, float(ROUTER_LANES)), axis=-1, keepdims=True)
        le2 = jnp.where(lane_f == i1, -jnp.inf, le)
        v2 = jnp.max(le2, axis=-1, keepdims=True)
        i2 = jnp.min(jnp.where(le2 == v2, lane_f, float(ROUTER_LANES)), axis=-1, keepdims=True)
        e2 = jnp.exp(v2 - v1)
        p1 = gate_g / (1.0 + e2)
        comb = jnp.where(lane_f == i1, p1, 0.0) + jnp.where(lane_f == i2, p1 * e2, 0.0)
        for grp in range(N_GROUPS):
            comb_ref[grp] = pltpu.roll(comb, ROUTER_LANES - (N_GROUPS + grp * EXP_PER_GROUP), 1)
        acc_ref[...] = jnp.zeros_like(acc_ref)

    h = h_ref[...]
    hid = jax.nn.silu(jnp.dot(h, wg_ref[...], preferred_element_type=F32)) * jnp.dot(
        h, wu_ref[...], preferred_element_type=F32)
    comb = comb_ref[f]
    hid = jnp.concatenate([hid[:, k * EXPERT_FF:(k + 1) * EXPERT_FF] * comb[:, k:k + 1]
                           for k in range(EXP_PER_GROUP)], axis=-1)
    acc_ref[...] += jnp.dot(hid.astype(BF16), wd_ref[...], preferred_element_type=F32)

    @pl.when(f == N_GROUPS - 1)
    def _():
        o_ref[...] = x1_ref[...] + m[5:6] * acc_ref[...]


def moe_pallas(x, ya, yb, mod, nw, wo_a, wo_b, wr, br, wg, wu, wd, rows_per_mod):
    t = x.shape[0]
    tiles_per_mod = rows_per_mod // TM
    row = lambda i, f: (i, 0)
    const = lambda i, f: (0, 0)
    return pl.pallas_call(
        _moe_kernel,
        grid=(t // TM, N_GROUPS),
        in_specs=[pl.BlockSpec((TM, D_MODEL), row),
                  pl.BlockSpec((TM, ya.shape[1]), row),
                  pl.BlockSpec((TM, yb.shape[1]), row),
                  pl.BlockSpec((1, MOD_ROWS, D_MODEL), lambda i, f: (i // tiles_per_mod, 0, 0)),
                  pl.BlockSpec((1, D_MODEL), const),
                  pl.BlockSpec(wo_a.shape, const),
                  pl.BlockSpec(wo_b.shape, const),
                  pl.BlockSpec((D_MODEL, ROUTER_LANES), const),
                  pl.BlockSpec((1, ROUTER_LANES), const),
                  pl.BlockSpec((D_MODEL, MOE_FCHUNK), lambda i, f: (0, f)),
                  pl.BlockSpec((D_MODEL, MOE_FCHUNK), lambda i, f: (0, f)),
                  pl.BlockSpec((MOE_FCHUNK, D_MODEL), lambda i, f: (f, 0))],
        out_specs=pl.BlockSpec((TM, D_MODEL), row),
        out_shape=jax.ShapeDtypeStruct((t, D_MODEL), F32),
        scratch_shapes=[pltpu.VMEM((TM, D_MODEL), F32), pltpu.VMEM((TM, D_MODEL), BF16),
                        pltpu.VMEM((N_GROUPS, TM, ROUTER_LANES), F32), pltpu.VMEM((TM, D_MODEL), F32)],
        compiler_params=pltpu.CompilerParams(dimension_semantics=("arbitrary", "arbitrary"),
                                             vmem_limit_bytes=VMEM_LIMIT),
        name="out_proj_moe",
    )(x, ya, yb, mod, nw.reshape(1, D_MODEL), wo_a, wo_b, wr, br, wg, wu, wd)


def _final_norm_kernel(x_ref, w_ref, o_ref):
    x = x_ref[...]
    y = x * lax.rsqrt(jnp.mean(x * x, axis=-1, keepdims=True) + EPS)
    o_ref[...] = y * w_ref[...]


def final_norm(x, w):
    t, d = x.shape
    return pl.pallas_call(
        _final_norm_kernel,
        grid=(t // TM,),
        in_specs=[pl.BlockSpec((TM, d), lambda i: (i, 0)), pl.BlockSpec((1, d), lambda i: (0, 0))],
        out_specs=pl.BlockSpec((TM, d), lambda i: (i, 0)),
        out_shape=jax.ShapeDtypeStruct((t, d), F32),
        name="final_norm",
    )(x, w.reshape(1, d))


def _pad_cols(w, n):
    return jnp.pad(w, ((0, 0), (0, n - w.shape[1])))


def _mod_table(m):
    m = m.reshape(m.shape[0], 6, D_MODEL)
    return jnp.pad(m, ((0, 0), (0, MOD_ROWS - 6), (0, 0)))


def kernel(x_prompt, x_sample, c, c_ctx, state_dn, cache_b_k, cache_b_v, cache_c_k, cache_c_v, state_s5,
           ada_w, ada_b, norm_w, w_in_e, dn_conv_w, dn_a_log, dn_dt_bias, dn_norm_w, b_sink, w_out_e,
           w_in_o, c_q_norm, c_k_norm, s5_lam_re, s5_lam_im, s5_log_dt, s5_b_re, s5_b_im, s5_c_re, s5_c_im,
           s5_d, s5_glu_w, s5_glu_b, w_out_o, moe_rg, moe_rg_b, moe_re, moe_re_b, moe_wg, moe_wu, moe_wd,
           final_norm_w):
    cos, sin = axial_rope_tables(DEC_SEQ // GRID_W)
    cvec = jnp.concatenate([c_ctx[None, :], c, jnp.zeros((ADA_ROWS - 1 - DEC_BATCH, D_MODEL), F32)], axis=0)
    mods = ada_pallas(cvec, ada_w, ada_b)
    xp = x_prompt.reshape(BATCH * SEQ, D_MODEL)
    xs = x_sample.reshape(DEC_BATCH * DEC_SEQ, D_MODEL)
    new_dn, new_bk, new_bv, new_ck, new_cv, new_s5 = [], [], [], [], [], []
    for layer in range(DEPTH):
        mod_p = _mod_table(mods[layer, 0:1])
        mod_s = _mod_table(mods[layer, 1:1 + DEC_BATCH])
        i = layer // 2
        if layer % 2 == 0:
            ep = {'conv_w': dn_conv_w[i], 'a_log': dn_a_log[i], 'dt_bias': dn_dt_bias[i],
                  'norm_w': dn_norm_w[i], 'sink': b_sink[i]}
            qkv_w, z_w, a_w, bb_w, bq_w, bk_w, bv_w = split_cols(w_in_e[i].astype(BF16), EVEN_SPLITS)
            w_in = [qkv_w, z_w, _pad_cols(jnp.concatenate([a_w, bb_w], axis=1), LANES), bq_w, bk_w, bv_w]
            w_out = w_out_e[i].astype(BF16)
            wo_a, wo_b = w_out[:HA * DK], w_out[HA * DK:]
            proj_p = inproj_pallas(xp, mod_p, norm_w[layer, 0], w_in, BATCH * SEQ)
            proj_s = inproj_pallas(xs, mod_s, norm_w[layer, 0], w_in, DEC_SEQ)
            ya_p, yb_p, dn, bk, bv = even_mixer_context(proj_p, ep)
            ya_s, yb_s = even_mixer_latent(proj_s, ep, state_dn[:, i], cache_b_k[:, i], cache_b_v[:, i], cos, sin)
            new_dn.append(dn)
            new_bk.append(bk)
            new_bv.append(bv)
        else:
            op = {'q_norm': c_q_norm[i], 'k_norm': c_k_norm[i], 'lam_re': s5_lam_re[i],
                  'lam_im': s5_lam_im[i], 'log_dt': s5_log_dt[i], 'b_re': s5_b_re[i], 'b_im': s5_b_im[i],
                  'c_re': s5_c_re[i], 'c_im': s5_c_im[i], 'd': s5_d[i], 'glu_w': s5_glu_w[i],
                  'glu_b': s5_glu_b[i]}
            w_in = split_cols(w_in_o[i].astype(BF16), ODD_SPLITS)
            w_out = w_out_o[i].astype(BF16)
            wo_a, wo_b = w_out[:HC * HEAD_DIM], w_out[HC * HEAD_DIM:]
            proj_p = inproj_pallas(xp, mod_p, norm_w[layer, 0], w_in, BATCH * SEQ)
            proj_s = inproj_pallas(xs, mod_s, norm_w[layer, 0], w_in, DEC_SEQ)
            ya_p, yb_p, ck, cv, s5 = odd_mixer_context(proj_p, op)
            ya_s, yb_s = odd_mixer_latent(proj_s, op, cache_c_k[:, i], cache_c_v[:, i], state_s5[:, i], cos, sin)
            new_ck.append(ck)
            new_cv.append(cv)
            new_s5.append(s5)
        wr = _pad_cols(jnp.concatenate([moe_rg[layer], moe_re[layer]], axis=1), ROUTER_LANES)
        br = _pad_cols(jnp.concatenate([moe_rg_b[layer], moe_re_b[layer]])[None, :], ROUTER_LANES)
        moe_w = (wo_a, wo_b, wr, br, moe_wg[layer].astype(BF16), moe_wu[layer].astype(BF16),
                 moe_wd[layer].astype(BF16))
        xp = moe_pallas(xp, ya_p, yb_p, mod_p, norm_w[layer, 1], *moe_w, BATCH * SEQ)
        xs = moe_pallas(xs, ya_s, yb_s, mod_s, norm_w[layer, 1], *moe_w, DEC_SEQ)
    y_prompt = final_norm(xp, final_norm_w).reshape(x_prompt.shape)
    y_sample = final_norm(xs, final_norm_w).reshape(x_sample.shape)
    return (y_prompt, y_sample, jnp.stack(new_dn, axis=1), jnp.stack(new_bk, axis=1), jnp.stack(new_bv, axis=1),
            jnp.stack(new_ck, axis=1), jnp.stack(new_cv, axis=1), jnp.stack(new_s5, axis=1))
```

```python
import functools

import numpy as np
import jax
import jax.numpy as jnp
from jax import lax
from jax.experimental import pallas as pl
from jax.experimental.pallas import tpu as pltpu

D_MODEL = 1024
BATCH = 32
SEQ = 256
DEPTH = 2
DEC_BATCH = 8
DEC_SEQ = 2048
PAST_LEN = 256

GRID_W = 64
HEAD_DIM = 64
BLK = 128
ROPE_THETA = 10000.0
EPS = 1e-6
N_EVEN = (DEPTH + 1) // 2
N_ODD = DEPTH // 2
HA = D_MODEL // 128
DK = HEAD_DIM
CONV_W = 5
DN_CHUNK = 64
HB = D_MODEL // 128
HKV_B = HB // 4
WINDOW = 128
HC = (3 * D_MODEL // 4) // HEAD_DIM
HKV_C = HC // 3
D_S5 = D_MODEL // 4
S5_CH = 16
G_D = D_S5 // S5_CH
S5_P = 64
N_GROUPS = 4
EXP_PER_GROUP = 8
N_EXP = N_GROUPS * EXP_PER_GROUP
EXPERT_FF = D_MODEL // 8
TOP_K = 2

EVEN_SPLITS = (3 * HA * DK, HA * DK, 2 * HA, 2 * HA, HB * HEAD_DIM, HKV_B * HEAD_DIM, HKV_B * HEAD_DIM)
D_IN_EVEN = sum(EVEN_SPLITS)
D_MIX_EVEN = HA * DK + HB * HEAD_DIM
ODD_SPLITS = (HC * HEAD_DIM, HKV_C * HEAD_DIM, HKV_C * HEAD_DIM, D_S5)
D_IN_ODD = sum(ODD_SPLITS)
D_MIX_ODD = HC * HEAD_DIM + D_S5

F32 = jnp.float32
BF16 = jnp.bfloat16


def split_cols(x, sizes):
    idx = np.cumsum(sizes)[:-1].tolist()
    return jnp.split(x, idx, axis=-1)


def axial_rope_tables(rows):
    row = jnp.repeat(jnp.arange(rows), GRID_W).astype(F32)
    col = jnp.tile(jnp.arange(GRID_W), rows).astype(F32)
    quarter = HEAD_DIM // 4
    freqs = ROPE_THETA ** (-jnp.arange(quarter, dtype=F32) / quarter)
    ang_r = row[:, None] * freqs
    ang_c = col[:, None] * freqs
    ang = jnp.concatenate([ang_r, ang_r, ang_c, ang_c], axis=-1)
    return jnp.cos(ang), jnp.sin(ang)


SUBLANES = 8
VMEM_LIMIT = 56 * 1024 * 1024
S5_ROWS = 1024
S5_STATE = G_D * S5_P


def _s5_scan_kernel(u_ref, bmat_ref, cmat_ref, a_ref, s0_ref, y_ref, sfin_ref, xs_ref, st_ref, *, nblk, bsz):
    dr = pl.program_id(0)
    blk = pl.program_id(1)

    @pl.when(blk == 0)
    def _():
        st_ref[...] = s0_ref[0]

    xs_ref[...] = jnp.dot(u_ref[...].astype(BF16), bmat_ref[0], preferred_element_type=F32)
    a_re = jnp.broadcast_to(a_ref[0, 0:1, :], (SUBLANES, S5_STATE))
    a_im = jnp.broadcast_to(a_ref[0, 1:2, :], (SUBLANES, S5_STATE))
    steps = S5_ROWS // bsz
    for sg in range(bsz // SUBLANES):
        rows = pl.ds(sg * SUBLANES, SUBLANES)

        def body(i, carry, sg=sg):
            x_re, x_im = carry
            l = i + dr * (steps - 1 - 2 * i)
            r = pl.ds(pl.multiple_of(l * bsz + sg * SUBLANES, SUBLANES), SUBLANES)
            n_re = a_re * x_re - a_im * x_im + xs_ref[r, 0:S5_STATE]
            n_im = a_re * x_im + a_im * x_re + xs_ref[r, S5_STATE:2 * S5_STATE]
            xs_ref[r, 0:S5_STATE] = n_re
            xs_ref[r, S5_STATE:2 * S5_STATE] = n_im
            return n_re, n_im

        x_re, x_im = lax.fori_loop(0, steps, body, (st_ref[rows, 0:S5_STATE], st_ref[rows, S5_STATE:2 * S5_STATE]),
                                   unroll=4)
        st_ref[rows, 0:S5_STATE] = x_re
        st_ref[rows, S5_STATE:2 * S5_STATE] = x_im
    y_ref[0] = jnp.dot(xs_ref[...].astype(BF16), cmat_ref[0], preferred_element_type=F32)

    @pl.when(blk == nblk - 1)
    def _():
        sfin_ref[0] = st_ref[...]


def _s5_out_kernel(yf_ref, yb_ref, u_ref, d_ref, w_ref, b_ref, o_ref):
    y = yf_ref[0] + yb_ref[0] + u_ref[...] * d_ref[...]
    y = jax.nn.gelu(y)
    gate = jnp.dot(y.astype(BF16), w_ref[...], preferred_element_type=F32) + b_ref[...]
    o_ref[...] = y * jax.nn.sigmoid(gate)


def s5_operators(op):
    lam_re, lam_im = op['lam_re'].astype(F32), op['lam_im'].astype(F32)
    dt = jnp.exp(op['log_dt'].astype(F32))[..., None]
    mag = jnp.exp(lam_re * dt)
    a_re, a_im = mag * jnp.cos(lam_im * dt), mag * jnp.sin(lam_im * dt)
    den = lam_re * lam_re + lam_im * lam_im
    k_re = ((a_re - 1) * lam_re + a_im * lam_im) / den
    k_im = (a_im * lam_re - (a_re - 1) * lam_im) / den
    b_re, b_im = op['b_re'].astype(F32), op['b_im'].astype(F32)
    bb_re = k_re[..., None] * b_re - k_im[..., None] * b_im
    bb_im = k_re[..., None] * b_im + k_im[..., None] * b_re
    eye = jnp.eye(G_D, dtype=F32)

    def b_blocks(t):
        return jnp.einsum('dgpc,gh->dgchp', t, eye).reshape(2, D_S5, S5_STATE)

    def c_blocks(t):
        return jnp.einsum('dgcp,gh->dgphc', t, eye).reshape(2, S5_STATE, D_S5)

    b_blk = jnp.concatenate([b_blocks(bb_re), b_blocks(bb_im)], axis=-1).astype(BF16)
    c_blk = jnp.concatenate([c_blocks(op['c_re'].astype(F32)), -c_blocks(op['c_im'].astype(F32))],
                            axis=1).astype(BF16)
    a_vec = jnp.stack([a_re.reshape(2, S5_STATE), a_im.reshape(2, S5_STATE)], axis=1)
    return b_blk, c_blk, a_vec


def s5_pallas(u, op, ops, s0):
    b, l, _ = u.shape
    b_blk, c_blk, a_vec = ops
    n = b * l
    nblk = n // S5_ROWS
    u_t = jnp.swapaxes(u, 0, 1).reshape(n, D_S5)

    def rows_map(dr, blk):
        return (blk + dr * (nblk - 1 - 2 * blk), 0)

    y2, s_fin = pl.pallas_call(
        functools.partial(_s5_scan_kernel, nblk=nblk, bsz=b),
        grid=(2, nblk),
        in_specs=[
            pl.BlockSpec((S5_ROWS, D_S5), rows_map),
            pl.BlockSpec((1, D_S5, 2 * S5_STATE), lambda dr, blk: (dr, 0, 0)),
            pl.BlockSpec((1, 2 * S5_STATE, D_S5), lambda dr, blk: (dr, 0, 0)),
            pl.BlockSpec((1, 2, S5_STATE), lambda dr, blk: (dr, 0, 0)),
            pl.BlockSpec((1, b, 2 * S5_STATE), lambda dr, blk: (dr, 0, 0)),
        ],
        out_specs=[
            pl.BlockSpec((1, S5_ROWS, D_S5), lambda dr, blk: (dr,) + rows_map(dr, blk)),
            pl.BlockSpec((1, b, 2 * S5_STATE), lambda dr, blk: (dr, 0, 0)),
        ],
        out_shape=[jax.ShapeDtypeStruct((2, n, D_S5), F32), jax.ShapeDtypeStruct((2, b, 2 * S5_STATE), F32)],
        scratch_shapes=[pltpu.VMEM((S5_ROWS, 2 * S5_STATE), F32), pltpu.VMEM((b, 2 * S5_STATE), F32)],
        compiler_params=pltpu.CompilerParams(dimension_semantics=("arbitrary", "arbitrary"),
                                             vmem_limit_bytes=VMEM_LIMIT),
        name="s5_scan",
    )(u_t, b_blk, c_blk, a_vec, s0)
    tm = S5_ROWS
    y = pl.pallas_call(
        _s5_out_kernel,
        grid=(n // tm,),
        in_specs=[
            pl.BlockSpec((1, tm, D_S5), lambda i: (0, i, 0)),
            pl.BlockSpec((1, tm, D_S5), lambda i: (1, i, 0)),
            pl.BlockSpec((tm, D_S5), lambda i: (i, 0)),
            pl.BlockSpec((1, D_S5), lambda i: (0, 0)),
            pl.BlockSpec((D_S5, D_S5), lambda i: (0, 0)),
            pl.BlockSpec((1, D_S5), lambda i: (0, 0)),
        ],
        out_specs=pl.BlockSpec((tm, D_S5), lambda i: (i, 0)),
        out_shape=jax.ShapeDtypeStruct((n, D_S5), F32),
        name="s5_out",
    )(y2, y2, u_t, op['d'].astype(F32).reshape(1, D_S5), op['glu_w'].astype(BF16),
      op['glu_b'].astype(F32).reshape(1, D_S5))
    return jnp.swapaxes(y.reshape(l, b, D_S5), 0, 1), s_fin


LANES = 128
DN_PAIRS = HA * DK // LANES
NEG_BIG = -1e30
DN_UNROLL = 4


def _nt_dot(a, b):
    return lax.dot_general(a, b, (((1,), (1,)), ((), ())), preferred_element_type=F32)


def _bf16_dot(a, b):
    return jnp.dot(a.astype(BF16), b.astype(BF16), preferred_element_type=F32)


def _select_dot(x, sel):
    x1 = x.astype(BF16)
    r1 = x - x1.astype(F32)
    x2 = r1.astype(BF16)
    x3 = (r1 - x2.astype(F32)).astype(BF16)
    dot = functools.partial(jnp.dot, preferred_element_type=F32)
    return dot(x1, sel) + dot(x2, sel) + dot(x3, sel)


def _head_sums(x):
    lane = lax.broadcasted_iota(jnp.int32, x.shape, 1)
    lo = lane < HEAD_DIM
    s0 = jnp.sum(jnp.where(lo, x, 0.0), axis=-1, keepdims=True)
    s1 = jnp.sum(jnp.where(lo, 0.0, x), axis=-1, keepdims=True)
    return jnp.where(lo, s0, s1)


def _chunk_cumsum(x, reverse):
    n = x.shape[0]
    pos = lax.broadcasted_iota(jnp.int32, x.shape, 0) % DN_CHUNK
    s = 1
    while s < DN_CHUNK:
        if reverse:
            x = x + jnp.where(pos < DN_CHUNK - s, pltpu.roll(x, n - s, 0), 0.0)
        else:
            x = x + jnp.where(pos >= s, pltpu.roll(x, s, 0), 0.0)
        s *= 2
    return x


def _short_conv_silu(x, w):
    n = x.shape[0]
    row = lax.broadcasted_iota(jnp.int32, x.shape, 0)
    pad = CONV_W // 2
    y = x * w[pad:pad + 1, :]
    for t in range(CONV_W):
        s = t - pad
        if s == 0:
            continue
        shifted = pltpu.roll(x, (-s) % n, 0)
        valid = (row + s >= 0) & (row + s < n)
        y = y + jnp.where(valid, shifted, 0.0) * w[t:t + 1, :]
    return jax.nn.silu(y)


def _unit_tri_solve(ms, rhss, ri, ci):
    n = rhss[0].shape[-1]
    same16 = (ri // 16) == (ci // 16)
    dot = functools.partial(jnp.dot, preferred_element_type=F32)

    def apply(p, r):
        hi = r.astype(BF16)
        lo = (r - hi.astype(F32)).astype(BF16)
        x = dot(p, jnp.concatenate([hi, lo], axis=-1))
        return x[:, 0:n] + x[:, n:2 * n]

    diag = [jnp.where(same16, m, 0.0) for m in ms]
    rest = [jnp.where(same16, 0.0, m).astype(BF16) for m in ms]
    pows = [d.astype(BF16) for d in diag]
    tinv = [jnp.where(ri == ci, 1.0, 0.0) - d for d in diag]
    for _ in range(3):
        pows = [dot(p, p).astype(BF16) for p in pows]
        tinv = [t + dot(t.astype(BF16), p) for t, p in zip(tinv, pows)]
    tinv = [t.astype(BF16) for t in tinv]
    p1 = [dot(t, e).astype(BF16) for t, e in zip(tinv, rest)]
    sols = [apply(t, r) for t, r in zip(tinv, rhss)]
    p2 = [dot(p, p).astype(BF16) for p in p1]
    sols = [s - apply(p, s) for p, s in zip(p1, sols)]
    return [s + apply(p, s) for p, s in zip(p2, sols)]


def _deltanet_kernel(q_ref, k_ref, v_ref, z_ref, a_ref, bb_ref, cwq_ref, cwk_ref, cwv_ref, alog_ref, dtb_ref, nw_ref,
                     s0_ref, y_ref, sfin_ref,
                     qs_ref, ks_ref, vs_ref, o_ref, gc_ref, beta_ref, u_ref, w_ref, attn_ref, qp_ref, kpt_ref,
                     etot_ref, *, seq):
    hp = pl.program_id(1)
    nchunk = seq // DN_CHUNK
    c64 = DN_CHUNK

    q = _short_conv_silu(q_ref[0], cwq_ref[...])
    qs_ref[...] = q * lax.rsqrt(_head_sums(q * q) + EPS) * (DK ** -0.5)
    k = _short_conv_silu(k_ref[0], cwk_ref[...])
    ks_ref[...] = k * lax.rsqrt(_head_sums(k * k) + EPS)
    vs_ref[...] = _short_conv_silu(v_ref[0], cwv_ref[...])
    o_ref[...] = jnp.zeros_like(o_ref)

    g_col = -jnp.exp(alog_ref[...]) * jax.nn.softplus(a_ref[0] + dtb_ref[...])
    b_col = jax.nn.sigmoid(bb_ref[0])
    sel_row = lax.broadcasted_iota(jnp.int32, (2 * HA, 2 * LANES), 0)
    sel_lane = lax.broadcasted_iota(jnp.int32, (2 * HA, 2 * LANES), 1)
    sel = (sel_row == (sel_lane // LANES) * HA + 2 * hp + (sel_lane // HEAD_DIM) % 2).astype(BF16)
    g_b = _select_dot(g_col, sel)
    gc_ref[:, 0:LANES] = _chunk_cumsum(g_b[:, 0:LANES], reverse=False)
    gc_ref[:, LANES:2 * LANES] = _chunk_cumsum(g_b[:, LANES:2 * LANES], reverse=True)
    beta_ref[...] = _select_dot(b_col, sel)

    ri = lax.broadcasted_iota(jnp.int32, (c64, c64), 0)
    ci = lax.broadcasted_iota(jnp.int32, (c64, c64), 1)

    def intra_chunk(step, carry):
        chunk_ids = [step * DN_UNROLL + t for t in range(DN_UNROLL)]
        rows_c = [pl.ds(pl.multiple_of(c * c64, c64), c64) for c in chunk_ids]
        loads = [(qs_ref[r, :], ks_ref[r, :], vs_ref[r, :], gc_ref[r, :], beta_ref[r, :]) for r in rows_c]
        heads = []
        for t, (q2, k2, v2, _, _) in enumerate(loads):
            for j in range(2):
                hl = slice(j * HEAD_DIM, (j + 1) * HEAD_DIM)
                heads.append((t, j, q2[:, hl], k2[:, hl], v2[:, hl]))
        kbs = [h[3].astype(BF16) for h in heads]
        kks = [_nt_dot(kb, kb) for kb in kbs]
        qks = [_nt_dot(h[2].astype(BF16), kb) for h, kb in zip(heads, kbs)]
        chains, ms, rhss = [], [], []
        for (t, j, qj, kj, vj), kk, qk in zip(heads, kks, qks):
            for d in range(2):
                src = slice(d * LANES + j * HEAD_DIM, d * LANES + (j + 1) * HEAD_DIM)
                gcb = loads[t][3][:, src]
                bet = loads[t][4][:, src]
                diff = gcb - gcb.T
                earlier = (ri >= ci) if d == 0 else (ri <= ci)
                strict = (ri > ci) if d == 0 else (ri < ci)
                dec = jnp.exp(jnp.where(earlier, diff, NEG_BIG))
                eg = jnp.exp(gcb)
                gtot = gcb[c64 - 1:c64, :] if d == 0 else gcb[0:1, :]
                ms.append(jnp.where(strict, kk * bet * dec, 0.0))
                rhss.append(jnp.concatenate([vj * bet, kj * bet * eg], axis=-1))
                chains.append((t, 2 * j + d, qk * dec, qj * eg, (kj * jnp.exp(gtot - gcb)).T,
                               jnp.broadcast_to(jnp.exp(gtot), (SUBLANES, HEAD_DIM))))
        uws = _unit_tri_solve(ms, rhss, ri, ci)
        for t in range(DN_UNROLL):
            slots = [[None] * 4 for _ in range(6)]
            for (tc, lane_slot, attn, qp, kpt, etot), uw in zip(chains, uws):
                if tc == t:
                    vals = (uw[:, 0:HEAD_DIM], uw[:, HEAD_DIM:2 * HEAD_DIM], attn, qp, kpt, etot)
                    for slot, val in zip(slots, vals):
                        slot[lane_slot] = val
            for ref, slot in zip((u_ref, w_ref, attn_ref, qp_ref, kpt_ref), slots[:5]):
                ref[rows_c[t], :] = jnp.concatenate(slot, axis=-1)
            etot_ref[pl.ds(pl.multiple_of(chunk_ids[t] * SUBLANES, SUBLANES), SUBLANES), :] = jnp.concatenate(
                slots[5], axis=-1)
        return carry

    lax.fori_loop(0, nchunk // DN_UNROLL, intra_chunk, 0)

    def inter_chunk(c, states):
        rows_d = [pl.ds(pl.multiple_of(cc * c64, c64), c64) for cc in (c, nchunk - 1 - c)]
        erow_d = [pl.ds(pl.multiple_of(cc * SUBLANES, SUBLANES), SUBLANES) for cc in (c, nchunk - 1 - c)]
        loaded = []
        for j in range(2):
            for d in range(2):
                dst = slice((2 * j + d) * HEAD_DIM, (2 * j + d + 1) * HEAD_DIM)
                loaded.append((w_ref[rows_d[d], dst], qp_ref[rows_d[d], dst], u_ref[rows_d[d], dst],
                               attn_ref[rows_d[d], dst], kpt_ref[rows_d[d], dst], etot_ref[erow_d[d], dst]))
        o_prev = [o_ref[rows_d[0], :], o_ref[rows_d[1], :]]
        ws_qs = [jnp.dot(jnp.concatenate([ld[0], ld[1]], axis=0).astype(BF16), s.astype(BF16),
                         preferred_element_type=F32) for ld, s in zip(loaded, states)]
        v_new = [(ld[2] - r[0:c64]).astype(BF16) for ld, r in zip(loaded, ws_qs)]
        av = [jnp.dot(ld[3].astype(BF16), v, preferred_element_type=F32) for ld, v in zip(loaded, v_new)]
        kv = [jnp.dot(ld[4].astype(BF16), v, preferred_element_type=F32) for ld, v in zip(loaded, v_new)]
        new_states = [s * ld[5][0:1, :] + x for s, ld, x in zip(states, loaded, kv)]
        o_new = [[None, None], [None, None]]
        for j in range(2):
            for d in range(2):
                o_new[d][j] = ws_qs[2 * j + d][c64:2 * c64] + av[2 * j + d]
        for d in range(2):
            o_ref[rows_d[d], :] = o_prev[d] + jnp.concatenate(o_new[d], axis=-1)
        return tuple(new_states)

    init = tuple(s0_ref[0, d, j] for j in range(2) for d in range(2))
    fin = lax.fori_loop(0, nchunk, inter_chunk, init)
    for j in range(2):
        for d in range(2):
            sfin_ref[0, d, j] = fin[2 * j + d]

    o = o_ref[...]
    o = o * lax.rsqrt(_head_sums(o * o) * (1.0 / DK) + EPS) * nw_ref[...]
    y_ref[0] = o * jax.nn.silu(z_ref[0])


def deltanet_pallas(qkv, z, a, bb, ep, s0):
    b, l, _ = qkv.shape
    npair = DN_PAIRS

    def col_spec(off):
        return pl.BlockSpec((1, l, LANES), lambda i, hp: (i, 0, off + hp))

    def cw_spec(off):
        return pl.BlockSpec((CONV_W, LANES), lambda i, hp: (0, off + hp))

    def full(shape):
        return pl.BlockSpec(shape, lambda i, hp: (0,) * len(shape))

    state_spec = pl.BlockSpec((1, 2, 2, DK, DK), lambda i, hp: (i, 0, hp, 0, 0))
    row_f32 = pltpu.VMEM((l, LANES), F32)
    chain_f32 = pltpu.VMEM((l, 2 * LANES), F32)
    y, s_fin = pl.pallas_call(
        functools.partial(_deltanet_kernel, seq=l),
        grid=(b, npair),
        in_specs=[col_spec(0), col_spec(npair), col_spec(2 * npair), col_spec(0),
                  pl.BlockSpec((1, l, 2 * HA), lambda i, hp: (i, 0, 0)),
                  pl.BlockSpec((1, l, 2 * HA), lambda i, hp: (i, 0, 0)),
                  cw_spec(0), cw_spec(npair), cw_spec(2 * npair),
                  full((1, 2 * HA)), full((1, 2 * HA)), full((1, LANES)), state_spec],
        out_specs=[col_spec(0), state_spec],
        out_shape=[jax.ShapeDtypeStruct((b, l, HA * DK), F32), jax.ShapeDtypeStruct((b, 2, HA, DK, DK), F32)],
        scratch_shapes=[row_f32, row_f32, row_f32, row_f32, chain_f32, chain_f32,
                        chain_f32, chain_f32, chain_f32, chain_f32, chain_f32,
                        pltpu.VMEM((l // DN_CHUNK * SUBLANES, 2 * LANES), F32)],
        compiler_params=pltpu.CompilerParams(dimension_semantics=("arbitrary", "arbitrary"),
                                             vmem_limit_bytes=VMEM_LIMIT),
        name="deltanet",
    )(qkv, qkv, qkv, z, a, bb, ep['conv_w'], ep['conv_w'], ep['conv_w'],
      ep['a_log'].astype(F32).reshape(1, 2 * HA), ep['dt_bias'].astype(F32).reshape(1, 2 * HA),
      jnp.tile(ep['norm_w'].astype(F32), 2).reshape(1, LANES), s0)
    return y, s_fin


TM = 512
ATT_TQ = BLK
ROT = HEAD_DIM // 4


def rope_tables():
    cos, sin = axial_rope_tables(DEC_SEQ // GRID_W)
    cos2 = jnp.tile(cos, (1, LANES // HEAD_DIM))
    sin2 = jnp.tile(sin, (1, LANES // HEAD_DIM))
    first = (jnp.arange(LANES) % (2 * ROT)) < ROT
    return cos2, jnp.where(first, -sin2, 0.0), jnp.where(first, 0.0, sin2)


def _prep_kernel(q_ref, k_ref, v_ref, *refs, norm, rope, k_f32):
    refs = list(refs)
    qw_ref, kw_ref = (refs.pop(0), refs.pop(0)) if norm else (None, None)
    cos_ref, sup_ref, sdn_ref = (refs.pop(0), refs.pop(0), refs.pop(0)) if rope else (None, None, None)
    qo_ref, ko_ref, vo_ref = refs[0:3]

    def process(x_ref, w_ref, scale):
        outs, normed = [], []
        for c in range(x_ref.shape[1] // LANES):
            xc = x_ref[:, c * LANES:(c + 1) * LANES]
            if norm:
                xc = xc * lax.rsqrt(_head_sums(xc * xc) * (1.0 / HEAD_DIM) + EPS) * w_ref[...]
                normed.append(xc)
            if rope:
                xc = (xc * cos_ref[...] + pltpu.roll(xc, LANES - ROT, 1) * sup_ref[...]
                      + pltpu.roll(xc, ROT, 1) * sdn_ref[...])
            outs.append((xc * scale).astype(BF16))
        cat = lambda parts: jnp.concatenate(parts, axis=-1) if len(parts) > 1 else parts[0]
        return cat(outs), (cat(normed) if norm else None)

    qo_ref[...] = process(q_ref, qw_ref, HEAD_DIM ** -0.5)[0]
    ko, kn = process(k_ref, kw_ref, 1.0)
    ko_ref[...] = ko
    if k_f32:
        refs[3][...] = kn
    vo_ref[...] = v_ref[...].astype(BF16)


def prep_pallas(q, k, v, norm_ws, rope_tabs, k_f32):
    t = q.shape[0]
    row = lambda i: (i, 0)
    ins = [q, k, v]
    in_specs = [pl.BlockSpec((TM, a.shape[1]), row) for a in ins]
    if norm_ws is not None:
        ins += [jnp.tile(w.astype(F32), LANES // HEAD_DIM).reshape(1, LANES) for w in norm_ws]
        in_specs += [pl.BlockSpec((1, LANES), lambda i: (0, 0))] * 2
    if rope_tabs is not None:
        tiles = DEC_SEQ // TM
        ins += list(rope_tabs)
        in_specs += [pl.BlockSpec((TM, LANES), lambda i: (i % tiles, 0))] * 3
    outs = [(a.shape, BF16) for a in (q, k, v)] + ([(k.shape, F32)] if k_f32 else [])
    return pl.pallas_call(
        functools.partial(_prep_kernel, norm=norm_ws is not None, rope=rope_tabs is not None, k_f32=k_f32),
        grid=(t // TM,),
        in_specs=in_specs,
        out_specs=[pl.BlockSpec((TM, s[1]), row) for s, _ in outs],
        out_shape=[jax.ShapeDtypeStruct(s, dt) for s, dt in outs],
        name="qkv_prep",
    )(*ins)


def _attn_kernel(q_ref, k_ref, v_ref, *refs, grp, hk, windowed, has_ctx, has_sink):
    refs = list(refs)
    kc_ref, vc_ref = (refs.pop(0), refs.pop(0)) if has_ctx else (None, None)
    sink_ref = refs.pop(0) if has_sink else None
    o_ref = refs[0]
    tq = q_ref.shape[1]
    seq_k = k_ref.shape[1]
    if windowed:
        qi = pl.program_id(1)
        start = jnp.clip((qi - 1) * tq, 0, seq_k - 3 * tq)
        krows = pl.ds(pl.multiple_of(start, tq), 3 * tq)
        qpos = qi * tq + lax.broadcasted_iota(jnp.int32, (grp * tq, 3 * tq), 0) % tq
        kpos = start + lax.broadcasted_iota(jnp.int32, (grp * tq, 3 * tq), 1)
        keep = jnp.abs(qpos - kpos) <= WINDOW
    else:
        krows = pl.ds(0, seq_k)
    q = q_ref[0]
    outs = [None] * (grp * hk)
    for g in range(hk):
        gl = slice(g * HEAD_DIM, (g + 1) * HEAD_DIM)
        qg = jnp.concatenate([q[:, (g * grp + j) * HEAD_DIM:(g * grp + j + 1) * HEAD_DIM] for j in range(grp)],
                             axis=0)
        s = _nt_dot(qg, k_ref[0, krows, gl])
        if windowed:
            s = jnp.where(keep, s, NEG_BIG)
        m = jnp.max(s, axis=-1, keepdims=True)
        if has_ctx:
            sc = _nt_dot(qg, kc_ref[0][:, gl].astype(BF16))
            m = jnp.maximum(m, jnp.max(sc, axis=-1, keepdims=True))
        if has_sink:
            sk = jnp.concatenate([jnp.broadcast_to(sink_ref[0:1, g * grp + j:g * grp + j + 1], (tq, 1))
                                  for j in range(grp)], axis=0)
            m = jnp.maximum(m, sk)
        p = jnp.exp(s - m)
        den = jnp.sum(p, axis=-1, keepdims=True)
        o = jnp.dot(p.astype(BF16), v_ref[0, krows, gl], preferred_element_type=F32)
        if has_ctx:
            pc = jnp.exp(sc - m)
            den = den + jnp.sum(pc, axis=-1, keepdims=True)
            o = o + jnp.dot(pc.astype(BF16), vc_ref[0][:, gl].astype(BF16), preferred_element_type=F32)
        if has_sink:
            den = den + jnp.exp(sk - m)
        o = o / den
        for j in range(grp):
            outs[g * grp + j] = o[j * tq:(j + 1) * tq]
    o_ref[0] = jnp.concatenate(outs, axis=-1)


def attention_pallas(q, k, v, ctx=None, sink=None, windowed=False):
    b, lq, hd = q.shape
    lk, kd = k.shape[1], k.shape[2]
    hk = kd // HEAD_DIM
    grp = hd // kd
    tq = ATT_TQ
    ins = [q, k, v]
    in_specs = [pl.BlockSpec((1, tq, hd), lambda i, t: (i, t, 0)),
                pl.BlockSpec((1, lk, kd), lambda i, t: (i, 0, 0)),
                pl.BlockSpec((1, lk, kd), lambda i, t: (i, 0, 0))]
    if ctx is not None:
        ins += list(ctx)
        in_specs += [pl.BlockSpec((1, ctx[0].shape[1], kd), lambda i, t: (i, 0, 0))] * 2
    if sink is not None:
        ins.append(sink.astype(F32).reshape(1, -1))
        in_specs.append(pl.BlockSpec((1, sink.shape[0]), lambda i, t: (0, 0)))
    return pl.pallas_call(
        functools.partial(_attn_kernel, grp=grp, hk=hk, windowed=windowed, has_ctx=ctx is not None,
                          has_sink=sink is not None),
        grid=(b, lq // tq),
        in_specs=in_specs,
        out_specs=pl.BlockSpec((1, tq, hd), lambda i, t: (i, t, 0)),
        out_shape=jax.ShapeDtypeStruct((b, lq, hd), F32),
        compiler_params=pltpu.CompilerParams(dimension_semantics=("arbitrary", "arbitrary"),
                                             vmem_limit_bytes=VMEM_LIMIT),
        name="attention",
    )(*ins)


def even_mixer_context(proj, ep):
    qkv, z, ab, bq, bk, bv = proj
    qkv, z, ab = [t.reshape(BATCH, SEQ, -1) for t in (qkv, z, ab)]
    a, bb = ab[..., 0:2 * HA], ab[..., 2 * HA:4 * HA]
    y_a, dn_fin = deltanet_pallas(qkv, z, a, bb, ep, jnp.zeros((BATCH, 2, HA, DK, DK), F32))
    qh, kh, vh = [t.reshape(BATCH, SEQ, -1) for t in prep_pallas(bq, bk, bv, None, None, False)]
    y_b = attention_pallas(qh, kh, vh, sink=ep['sink'])
    return (y_a.reshape(-1, HA * DK), y_b.reshape(-1, HB * HEAD_DIM), dn_fin,
            bk.reshape(BATCH, SEQ, HKV_B, HEAD_DIM), bv.reshape(BATCH, SEQ, HKV_B, HEAD_DIM))


def even_mixer_latent(proj, ep, dn_state, ctx_k, ctx_v, rope_tabs):
    qkv, z, ab, bq, bk, bv = proj
    qkv, z, ab = [t.reshape(DEC_BATCH, DEC_SEQ, -1) for t in (qkv, z, ab)]
    a, bb = ab[..., 0:2 * HA], ab[..., 2 * HA:4 * HA]
    y_a, _ = deltanet_pallas(qkv, z, a, bb, ep, dn_state.astype(F32))
    qh, kh, vh = [t.reshape(DEC_BATCH, DEC_SEQ, -1) for t in prep_pallas(bq, bk, bv, None, rope_tabs, False)]
    ctx = (ctx_k.reshape(DEC_BATCH, PAST_LEN, -1), ctx_v.reshape(DEC_BATCH, PAST_LEN, -1))
    y_b = attention_pallas(qh, kh, vh, ctx=ctx, sink=ep['sink'], windowed=True)
    return y_a.reshape(-1, HA * DK), y_b.reshape(-1, HB * HEAD_DIM)


def odd_mixer_context(proj, op):
    cq, ck, cv, u = proj
    qh, kh, vh, k_normed = prep_pallas(cq, ck, cv, (op['q_norm'], op['k_norm']), None, True)
    y_c = attention_pallas(*[t.reshape(BATCH, SEQ, -1) for t in (qh, kh, vh)])
    y_d, s_fin = s5_pallas(u.reshape(BATCH, SEQ, -1), op, s5_operators(op), jnp.zeros((2, BATCH, 2 * S5_STATE), F32))
    s_fin = jnp.transpose(s_fin.reshape(2, BATCH, 2, G_D, S5_P), (1, 0, 3, 4, 2))
    return (y_c.reshape(-1, HC * HEAD_DIM), y_d.reshape(-1, D_S5), k_normed.reshape(BATCH, SEQ, HKV_C, HEAD_DIM),
            cv.reshape(BATCH, SEQ, HKV_C, HEAD_DIM), s_fin)


def odd_mixer_latent(proj, op, ctx_k, ctx_v, s5_state, rope_tabs):
    cq, ck, cv, u = proj
    qh, kh, vh = prep_pallas(cq, ck, cv, (op['q_norm'], op['k_norm']), rope_tabs, False)
    ctx = (ctx_k.reshape(DEC_BATCH, PAST_LEN, -1), ctx_v.reshape(DEC_BATCH, PAST_LEN, -1))
    y_c = attention_pallas(*[t.reshape(DEC_BATCH, DEC_SEQ, -1) for t in (qh, kh, vh)], ctx=ctx)
    s0 = jnp.transpose(s5_state.astype(F32), (1, 0, 4, 2, 3)).reshape(2, DEC_BATCH, 2 * S5_STATE)
    y_d, _ = s5_pallas(u.reshape(DEC_BATCH, DEC_SEQ, -1), op, s5_operators(op), s0)
    return y_c.reshape(-1, HC * HEAD_DIM), y_d.reshape(-1, D_S5)


ADA_ROWS = 16
ADA_TN = 1024
MOD_ROWS = 8
ROUTER_LANES = LANES
MOE_FCHUNK = EXP_PER_GROUP * EXPERT_FF


def _ada_kernel(c_ref, w_ref, b_ref, o_ref):
    o_ref[0] = _bf16_dot(jax.nn.silu(c_ref[...]), w_ref[0]) + b_ref[0]


def ada_pallas(cvec, ada_w, ada_b):
    n = 6 * D_MODEL
    return pl.pallas_call(
        _ada_kernel,
        grid=(DEPTH, n // ADA_TN),
        in_specs=[pl.BlockSpec((ADA_ROWS, D_MODEL), lambda l, j: (0, 0)),
                  pl.BlockSpec((1, D_MODEL, ADA_TN), lambda l, j: (l, 0, j)),
                  pl.BlockSpec((1, 1, ADA_TN), lambda l, j: (l, 0, j))],
        out_specs=pl.BlockSpec((1, ADA_ROWS, ADA_TN), lambda l, j: (l, 0, j)),
        out_shape=jax.ShapeDtypeStruct((DEPTH, ADA_ROWS, n), F32),
        name="ada_modulation",
    )(cvec, ada_w, ada_b.reshape(DEPTH, 1, n))


def _modulated(x, nw, shift, scale):
    return x * lax.rsqrt(jnp.mean(x * x, axis=-1, keepdims=True) + EPS) * nw * (1 + scale) + shift


def _inproj_kernel(x_ref, mod_ref, nw_ref, *refs):
    nout = len(refs) // 2
    m = mod_ref[0]
    h = _modulated(x_ref[...], nw_ref[...], m[0:1], m[1:2]).astype(BF16)
    for w_ref, o_ref in zip(refs[:nout], refs[nout:]):
        o_ref[...] = jnp.dot(h, w_ref[...], preferred_element_type=F32)


def inproj_pallas(x, mod, nw, weights, rows_per_mod):
    t = x.shape[0]
    tiles_per_mod = rows_per_mod // TM
    return pl.pallas_call(
        _inproj_kernel,
        grid=(t // TM,),
        in_specs=[pl.BlockSpec((TM, D_MODEL), lambda i: (i, 0)),
                  pl.BlockSpec((1, MOD_ROWS, D_MODEL), lambda i: (i // tiles_per_mod, 0, 0)),
                  pl.BlockSpec((1, D_MODEL), lambda i: (0, 0))]
        + [pl.BlockSpec(w.shape, lambda i: (0, 0)) for w in weights],
        out_specs=[pl.BlockSpec((TM, w.shape[1]), lambda i: (i, 0)) for w in weights],
        out_shape=[jax.ShapeDtypeStruct((t, w.shape[1]), F32) for w in weights],
        compiler_params=pltpu.CompilerParams(dimension_semantics=("arbitrary",), vmem_limit_bytes=VMEM_LIMIT),
        name="modulated_in_proj",
    )(x, mod, nw.reshape(1, D_MODEL), *weights)


def _moe_kernel(x_ref, ya_ref, yb_ref, mod_ref, nw_ref, woa_ref, wob_ref, wr_ref, br_ref, wg_ref, wu_ref, wd_ref,
                o_ref, x1_ref, h_ref, comb_ref, acc_ref):
    f = pl.program_id(1)
    m = mod_ref[0]

    @pl.when(f == 0)
    def _():
        y = _bf16_dot(ya_ref[...], woa_ref[...]) + _bf16_dot(yb_ref[...], wob_ref[...])
        x1 = x_ref[...] + m[2:3] * y
        x1_ref[...] = x1
        h = _modulated(x1, nw_ref[...], m[3:4], m[4:5])
        h_ref[...] = h.astype(BF16)
        logits = jnp.dot(h, wr_ref[...], preferred_element_type=F32, precision=lax.Precision.HIGHEST) + br_ref[...]
        lane = lax.broadcasted_iota(jnp.int32, logits.shape, 1)
        lane_f = lane.astype(F32)
        is_group = lane < N_GROUPS
        lg = jnp.where(is_group, logits, -jnp.inf)
        g_max = jnp.max(lg, axis=-1, keepdims=True)
        g_idx = jnp.min(jnp.where(lg == g_max, lane_f, float(ROUTER_LANES)), axis=-1, keepdims=True)
        gate_g = 1.0 / jnp.sum(jnp.where(is_group, jnp.exp(logits - g_max), 0.0), axis=-1, keepdims=True)
        lane_group = ((lane + (EXP_PER_GROUP - N_GROUPS)) // EXP_PER_GROUP - 1).astype(F32)
        in_group = (lane >= N_GROUPS) & (lane < N_GROUPS + N_EXP) & (lane_group == g_idx)
        le = jnp.where(in_group, logits, -jnp.inf)
        v1 = jnp.max(le, axis=-1, keepdims=True)
        i1 = jnp.min(jnp.where(le == v1, lane_f, float(ROUTER_LANES)), axis=-1, keepdims=True)
        le2 = jnp.where(lane_f == i1, -jnp.inf, le)
        v2 = jnp.max(le2, axis=-1, keepdims=True)
        i2 = jnp.min(jnp.where(le2 == v2, lane_f, float(ROUTER_LANES)), axis=-1, keepdims=True)
        e2 = jnp.exp(v2 - v1)
        p1 = gate_g / (1.0 + e2)
        comb = jnp.where(lane_f == i1, p1, 0.0) + jnp.where(lane_f == i2, p1 * e2, 0.0)
        for grp in range(N_GROUPS):
            comb_ref[grp] = pltpu.roll(comb, ROUTER_LANES - (N_GROUPS + grp * EXP_PER_GROUP), 1)
        acc_ref[...] = jnp.zeros_like(acc_ref)

    h = h_ref[...]
    hid = jax.nn.silu(jnp.dot(h, wg_ref[...], preferred_element_type=F32)) * jnp.dot(
        h, wu_ref[...], preferred_element_type=F32)
    comb = comb_ref[f]
    hid = jnp.concatenate([hid[:, k * EXPERT_FF:(k + 1) * EXPERT_FF] * comb[:, k:k + 1]
                           for k in range(EXP_PER_GROUP)], axis=-1)
    acc_ref[...] += jnp.dot(hid.astype(BF16), wd_ref[...], preferred_element_type=F32)

    @pl.when(f == N_GROUPS - 1)
    def _():
        o_ref[...] = x1_ref[...] + m[5:6] * acc_ref[...]


def moe_pallas(x, ya, yb, mod, nw, wo_a, wo_b, wr, br, wg, wu, wd, rows_per_mod):
    t = x.shape[0]
    tiles_per_mod = rows_per_mod // TM
    row = lambda i, f: (i, 0)
    const = lambda i, f: (0, 0)
    return pl.pallas_call(
        _moe_kernel,
        grid=(t // TM, N_GROUPS),
        in_specs=[pl.BlockSpec((TM, D_MODEL), row),
                  pl.BlockSpec((TM, ya.shape[1]), row),
                  pl.BlockSpec((TM, yb.shape[1]), row),
                  pl.BlockSpec((1, MOD_ROWS, D_MODEL), lambda i, f: (i // tiles_per_mod, 0, 0)),
                  pl.BlockSpec((1, D_MODEL), const),
                  pl.BlockSpec(wo_a.shape, const),
                  pl.BlockSpec(wo_b.shape, const),
                  pl.BlockSpec((D_MODEL, ROUTER_LANES), const),
                  pl.BlockSpec((1, ROUTER_LANES), const),
                  pl.BlockSpec((D_MODEL, MOE_FCHUNK), lambda i, f: (0, f)),
                  pl.BlockSpec((D_MODEL, MOE_FCHUNK), lambda i, f: (0, f)),
                  pl.BlockSpec((MOE_FCHUNK, D_MODEL), lambda i, f: (f, 0))],
        out_specs=pl.BlockSpec((TM, D_MODEL), row),
        out_shape=jax.ShapeDtypeStruct((t, D_MODEL), F32),
        scratch_shapes=[pltpu.VMEM((TM, D_MODEL), F32), pltpu.VMEM((TM, D_MODEL), BF16),
                        pltpu.VMEM((N_GROUPS, TM, ROUTER_LANES), F32), pltpu.VMEM((TM, D_MODEL), F32)],
        compiler_params=pltpu.CompilerParams(dimension_semantics=("arbitrary", "arbitrary"),
                                             vmem_limit_bytes=VMEM_LIMIT),
        name="out_proj_moe",
    )(x, ya, yb, mod, nw.reshape(1, D_MODEL), wo_a, wo_b, wr, br, wg, wu, wd)


def _final_norm_kernel(x_ref, w_ref, o_ref):
    x = x_ref[...]
    y = x * lax.rsqrt(jnp.mean(x * x, axis=-1, keepdims=True) + EPS)
    o_ref[...] = y * w_ref[...]


def final_norm(x, w):
    t, d = x.shape
    return pl.pallas_call(
        _final_norm_kernel,
        grid=(t // TM,),
        in_specs=[pl.BlockSpec((TM, d), lambda i: (i, 0)), pl.BlockSpec((1, d), lambda i: (0, 0))],
        out_specs=pl.BlockSpec((TM, d), lambda i: (i, 0)),
        out_shape=jax.ShapeDtypeStruct((t, d), F32),
        name="final_norm",
    )(x, w.reshape(1, d))


def _pad_cols(w, n):
    return jnp.pad(w, ((0, 0), (0, n - w.shape[1])))


def _mod_table(m):
    m = m.reshape(m.shape[0], 6, D_MODEL)
    return jnp.pad(m, ((0, 0), (0, MOD_ROWS - 6), (0, 0)))


def kernel(x_prompt, x_sample, c, c_ctx, state_dn, cache_b_k, cache_b_v, cache_c_k, cache_c_v, state_s5,
           ada_w, ada_b, norm_w, w_in_e, dn_conv_w, dn_a_log, dn_dt_bias, dn_norm_w, b_sink, w_out_e,
           w_in_o, c_q_norm, c_k_norm, s5_lam_re, s5_lam_im, s5_log_dt, s5_b_re, s5_b_im, s5_c_re, s5_c_im,
           s5_d, s5_glu_w, s5_glu_b, w_out_o, moe_rg, moe_rg_b, moe_re, moe_re_b, moe_wg, moe_wu, moe_wd,
           final_norm_w):
    rope_tabs = rope_tables()
    cvec = jnp.concatenate([c_ctx[None, :], c, jnp.zeros((ADA_ROWS - 1 - DEC_BATCH, D_MODEL), F32)], axis=0)
    mods = ada_pallas(cvec, ada_w, ada_b)
    xp = x_prompt.reshape(BATCH * SEQ, D_MODEL)
    xs = x_sample.reshape(DEC_BATCH * DEC_SEQ, D_MODEL)
    new_dn, new_bk, new_bv, new_ck, new_cv, new_s5 = [], [], [], [], [], []
    for layer in range(DEPTH):
        mod_p = _mod_table(mods[layer, 0:1])
        mod_s = _mod_table(mods[layer, 1:1 + DEC_BATCH])
        i = layer // 2
        if layer % 2 == 0:
            ep = {'conv_w': dn_conv_w[i], 'a_log': dn_a_log[i], 'dt_bias': dn_dt_bias[i],
                  'norm_w': dn_norm_w[i], 'sink': b_sink[i]}
            qkv_w, z_w, a_w, bb_w, bq_w, bk_w, bv_w = split_cols(w_in_e[i].astype(BF16), EVEN_SPLITS)
            w_in = [qkv_w, z_w, _pad_cols(jnp.concatenate([a_w, bb_w], axis=1), LANES), bq_w, bk_w, bv_w]
            w_out = w_out_e[i].astype(BF16)
            wo_a, wo_b = w_out[:HA * DK], w_out[HA * DK:]
            proj_p = inproj_pallas(xp, mod_p, norm_w[layer, 0], w_in, BATCH * SEQ)
            proj_s = inproj_pallas(xs, mod_s, norm_w[layer, 0], w_in, DEC_SEQ)
            ya_p, yb_p, dn, bk, bv = even_mixer_context(proj_p, ep)
            ya_s, yb_s = even_mixer_latent(proj_s, ep, state_dn[:, i], cache_b_k[:, i], cache_b_v[:, i], rope_tabs)
            new_dn.append(dn)
            new_bk.append(bk)
            new_bv.append(bv)
        else:
            op = {'q_norm': c_q_norm[i], 'k_norm': c_k_norm[i], 'lam_re': s5_lam_re[i],
                  'lam_im': s5_lam_im[i], 'log_dt': s5_log_dt[i], 'b_re': s5_b_re[i], 'b_im': s5_b_im[i],
                  'c_re': s5_c_re[i], 'c_im': s5_c_im[i], 'd': s5_d[i], 'glu_w': s5_glu_w[i],
                  'glu_b': s5_glu_b[i]}
            w_in = split_cols(w_in_o[i].astype(BF16), ODD_SPLITS)
            w_out = w_out_o[i].astype(BF16)
            wo_a, wo_b = w_out[:HC * HEAD_DIM], w_out[HC * HEAD_DIM:]
            proj_p = inproj_pallas(xp, mod_p, norm_w[layer, 0], w_in, BATCH * SEQ)
            proj_s = inproj_pallas(xs, mod_s, norm_w[layer, 0], w_in, DEC_SEQ)
            ya_p, yb_p, ck, cv, s5 = odd_mixer_context(proj_p, op)
            ya_s, yb_s = odd_mixer_latent(proj_s, op, cache_c_k[:, i], cache_c_v[:, i], state_s5[:, i], rope_tabs)
            new_ck.append(ck)
            new_cv.append(cv)
            new_s5.append(s5)
        wr = _pad_cols(jnp.concatenate([moe_rg[layer], moe_re[layer]], axis=1), ROUTER_LANES)
        br = _pad_cols(jnp.concatenate([moe_rg_b[layer], moe_re_b[layer]])[None, :], ROUTER_LANES)
        moe_w = (wo_a, wo_b, wr, br, moe_wg[layer].astype(BF16), moe_wu[layer].astype(BF16),
                 moe_wd[layer].astype(BF16))
        xp = moe_pallas(xp, ya_p, yb_p, mod_p, norm_w[layer, 1], *moe_w, BATCH * SEQ)
        xs = moe_pallas(xs, ya_s, yb_s, mod_s, norm_w[layer, 1], *moe_w, DEC_SEQ)
    y_prompt = final_norm(xp, final_norm_w).reshape(x_prompt.shape)
    y_sample = final_norm(xs, final_norm_w).reshape(x_sample.shape)
    return (y_prompt, y_sample, jnp.stack(new_dn, axis=1), jnp.stack(new_bk, axis=1), jnp.stack(new_bv, axis=1),
            jnp.stack(new_ck, axis=1), jnp.stack(new_cv, axis=1), jnp.stack(new_s5, axis=1))
```

```python
import functools

import numpy as np
import jax
import jax.numpy as jnp
from jax import lax
from jax.experimental import pallas as pl
from jax.experimental.pallas import tpu as pltpu

D_MODEL = 1024
BATCH = 32
SEQ = 256
DEPTH = 2
DEC_BATCH = 8
DEC_SEQ = 2048
PAST_LEN = 256

GRID_W = 64
HEAD_DIM = 64
BLK = 128
ROPE_THETA = 10000.0
EPS = 1e-6
N_EVEN = (DEPTH + 1) // 2
N_ODD = DEPTH // 2
HA = D_MODEL // 128
DK = HEAD_DIM
CONV_W = 5
DN_CHUNK = 64
HB = D_MODEL // 128
HKV_B = HB // 4
WINDOW = 128
HC = (3 * D_MODEL // 4) // HEAD_DIM
HKV_C = HC // 3
D_S5 = D_MODEL // 4
S5_CH = 16
G_D = D_S5 // S5_CH
S5_P = 64
N_GROUPS = 4
EXP_PER_GROUP = 8
N_EXP = N_GROUPS * EXP_PER_GROUP
EXPERT_FF = D_MODEL // 8
TOP_K = 2

EVEN_SPLITS = (3 * HA * DK, HA * DK, 2 * HA, 2 * HA, HB * HEAD_DIM, HKV_B * HEAD_DIM, HKV_B * HEAD_DIM)
D_IN_EVEN = sum(EVEN_SPLITS)
D_MIX_EVEN = HA * DK + HB * HEAD_DIM
ODD_SPLITS = (HC * HEAD_DIM, HKV_C * HEAD_DIM, HKV_C * HEAD_DIM, D_S5)
D_IN_ODD = sum(ODD_SPLITS)
D_MIX_ODD = HC * HEAD_DIM + D_S5

F32 = jnp.float32
BF16 = jnp.bfloat16


def split_cols(x, sizes):
    idx = np.cumsum(sizes)[:-1].tolist()
    return jnp.split(x, idx, axis=-1)


def axial_rope_tables(rows):
    row = jnp.repeat(jnp.arange(rows), GRID_W).astype(F32)
    col = jnp.tile(jnp.arange(GRID_W), rows).astype(F32)
    quarter = HEAD_DIM // 4
    freqs = ROPE_THETA ** (-jnp.arange(quarter, dtype=F32) / quarter)
    ang_r = row[:, None] * freqs
    ang_c = col[:, None] * freqs
    ang = jnp.concatenate([ang_r, ang_r, ang_c, ang_c], axis=-1)
    return jnp.cos(ang), jnp.sin(ang)


SUBLANES = 8
VMEM_LIMIT = 56 * 1024 * 1024
S5_ROWS = 1024
S5_STATE = G_D * S5_P


def _s5_scan_kernel(u_ref, bmat_ref, cmat_ref, a_ref, s0_ref, y_ref, sfin_ref, xs_ref, st_ref, *, nblk, bsz):
    dr = pl.program_id(0)
    blk = pl.program_id(1)

    @pl.when(blk == 0)
    def _():
        st_ref[...] = s0_ref[0]

    xs_ref[...] = jnp.dot(u_ref[...].astype(BF16), bmat_ref[0], preferred_element_type=F32)
    a_re = jnp.broadcast_to(a_ref[0, 0:1, :], (SUBLANES, S5_STATE))
    a_im = jnp.broadcast_to(a_ref[0, 1:2, :], (SUBLANES, S5_STATE))
    steps = S5_ROWS // bsz
    for sg in range(bsz // SUBLANES):
        rows = pl.ds(sg * SUBLANES, SUBLANES)

        def body(i, carry, sg=sg):
            x_re, x_im = carry
            l = i + dr * (steps - 1 - 2 * i)
            r = pl.ds(pl.multiple_of(l * bsz + sg * SUBLANES, SUBLANES), SUBLANES)
            n_re = a_re * x_re - a_im * x_im + xs_ref[r, 0:S5_STATE]
            n_im = a_re * x_im + a_im * x_re + xs_ref[r, S5_STATE:2 * S5_STATE]
            xs_ref[r, 0:S5_STATE] = n_re
            xs_ref[r, S5_STATE:2 * S5_STATE] = n_im
            return n_re, n_im

        x_re, x_im = lax.fori_loop(0, steps, body, (st_ref[rows, 0:S5_STATE], st_ref[rows, S5_STATE:2 * S5_STATE]),
                                   unroll=4)
        st_ref[rows, 0:S5_STATE] = x_re
        st_ref[rows, S5_STATE:2 * S5_STATE] = x_im
    y_ref[0] = jnp.dot(xs_ref[...].astype(BF16), cmat_ref[0], preferred_element_type=F32)

    @pl.when(blk == nblk - 1)
    def _():
        sfin_ref[0] = st_ref[...]


def _s5_out_kernel(yf_ref, yb_ref, u_ref, d_ref, w_ref, b_ref, o_ref):
    y = yf_ref[0] + yb_ref[0] + u_ref[...] * d_ref[...]
    y = jax.nn.gelu(y)
    gate = jnp.dot(y.astype(BF16), w_ref[...], preferred_element_type=F32) + b_ref[...]
    o_ref[...] = y * jax.nn.sigmoid(gate)


def s5_operators(op):
    lam_re, lam_im = op['lam_re'].astype(F32), op['lam_im'].astype(F32)
    dt = jnp.exp(op['log_dt'].astype(F32))[..., None]
    mag = jnp.exp(lam_re * dt)
    a_re, a_im = mag * jnp.cos(lam_im * dt), mag * jnp.sin(lam_im * dt)
    den = lam_re * lam_re + lam_im * lam_im
    k_re = ((a_re - 1) * lam_re + a_im * lam_im) / den
    k_im = (a_im * lam_re - (a_re - 1) * lam_im) / den
    b_re, b_im = op['b_re'].astype(F32), op['b_im'].astype(F32)
    bb_re = k_re[..., None] * b_re - k_im[..., None] * b_im
    bb_im = k_re[..., None] * b_im + k_im[..., None] * b_re
    eye = jnp.eye(G_D, dtype=F32)

    def b_blocks(t):
        return jnp.einsum('dgpc,gh->dgchp', t, eye).reshape(2, D_S5, S5_STATE)

    def c_blocks(t):
        return jnp.einsum('dgcp,gh->dgphc', t, eye).reshape(2, S5_STATE, D_S5)

    b_blk = jnp.concatenate([b_blocks(bb_re), b_blocks(bb_im)], axis=-1).astype(BF16)
    c_blk = jnp.concatenate([c_blocks(op['c_re'].astype(F32)), -c_blocks(op['c_im'].astype(F32))],
                            axis=1).astype(BF16)
    a_vec = jnp.stack([a_re.reshape(2, S5_STATE), a_im.reshape(2, S5_STATE)], axis=1)
    return b_blk, c_blk, a_vec


def s5_pallas(u, op, ops, s0):
    b, l, _ = u.shape
    b_blk, c_blk, a_vec = ops
    n = b * l
    nblk = n // S5_ROWS
    u_t = jnp.swapaxes(u, 0, 1).reshape(n, D_S5)

    def rows_map(dr, blk):
        return (blk + dr * (nblk - 1 - 2 * blk), 0)

    y2, s_fin = pl.pallas_call(
        functools.partial(_s5_scan_kernel, nblk=nblk, bsz=b),
        grid=(2, nblk),
        in_specs=[
            pl.BlockSpec((S5_ROWS, D_S5), rows_map),
            pl.BlockSpec((1, D_S5, 2 * S5_STATE), lambda dr, blk: (dr, 0, 0)),
            pl.BlockSpec((1, 2 * S5_STATE, D_S5), lambda dr, blk: (dr, 0, 0)),
            pl.BlockSpec((1, 2, S5_STATE), lambda dr, blk: (dr, 0, 0)),
            pl.BlockSpec((1, b, 2 * S5_STATE), lambda dr, blk: (dr, 0, 0)),
        ],
        out_specs=[
            pl.BlockSpec((1, S5_ROWS, D_S5), lambda dr, blk: (dr,) + rows_map(dr, blk)),
            pl.BlockSpec((1, b, 2 * S5_STATE), lambda dr, blk: (dr, 0, 0)),
        ],
        out_shape=[jax.ShapeDtypeStruct((2, n, D_S5), F32), jax.ShapeDtypeStruct((2, b, 2 * S5_STATE), F32)],
        scratch_shapes=[pltpu.VMEM((S5_ROWS, 2 * S5_STATE), F32), pltpu.VMEM((b, 2 * S5_STATE), F32)],
        compiler_params=pltpu.CompilerParams(dimension_semantics=("arbitrary", "arbitrary"),
                                             vmem_limit_bytes=VMEM_LIMIT),
        name="s5_scan",
    )(u_t, b_blk, c_blk, a_vec, s0)
    tm = S5_ROWS
    y = pl.pallas_call(
        _s5_out_kernel,
        grid=(n // tm,),
        in_specs=[
            pl.BlockSpec((1, tm, D_S5), lambda i: (0, i, 0)),
            pl.BlockSpec((1, tm, D_S5), lambda i: (1, i, 0)),
            pl.BlockSpec((tm, D_S5), lambda i: (i, 0)),
            pl.BlockSpec((1, D_S5), lambda i: (0, 0)),
            pl.BlockSpec((D_S5, D_S5), lambda i: (0, 0)),
            pl.BlockSpec((1, D_S5), lambda i: (0, 0)),
        ],
        out_specs=pl.BlockSpec((tm, D_S5), lambda i: (i, 0)),
        out_shape=jax.ShapeDtypeStruct((n, D_S5), F32),
        name="s5_out",
    )(y2, y2, u_t, op['d'].astype(F32).reshape(1, D_S5), op['glu_w'].astype(BF16),
      op['glu_b'].astype(F32).reshape(1, D_S5))
    return jnp.swapaxes(y.reshape(l, b, D_S5), 0, 1), s_fin


LANES = 128
DN_PAIRS = HA * DK // LANES
NEG_BIG = -1e30
DN_UNROLL = 4


def _nt_dot(a, b):
    return lax.dot_general(a, b, (((1,), (1,)), ((), ())), preferred_element_type=F32)


def _bf16_dot(a, b):
    return jnp.dot(a.astype(BF16), b.astype(BF16), preferred_element_type=F32)


def _select_dot(x, sel):
    x1 = x.astype(BF16)
    r1 = x - x1.astype(F32)
    x2 = r1.astype(BF16)
    x3 = (r1 - x2.astype(F32)).astype(BF16)
    dot = functools.partial(jnp.dot, preferred_element_type=F32)
    return dot(x1, sel) + dot(x2, sel) + dot(x3, sel)


def _head_sums(x):
    lane = lax.broadcasted_iota(jnp.int32, x.shape, 1)
    lo = lane < HEAD_DIM
    s0 = jnp.sum(jnp.where(lo, x, 0.0), axis=-1, keepdims=True)
    s1 = jnp.sum(jnp.where(lo, 0.0, x), axis=-1, keepdims=True)
    return jnp.where(lo, s0, s1)


def _chunk_cumsum(x, reverse):
    n = x.shape[0]
    pos = lax.broadcasted_iota(jnp.int32, x.shape, 0) % DN_CHUNK
    s = 1
    while s < DN_CHUNK:
        if reverse:
            x = x + jnp.where(pos < DN_CHUNK - s, pltpu.roll(x, n - s, 0), 0.0)
        else:
            x = x + jnp.where(pos >= s, pltpu.roll(x, s, 0), 0.0)
        s *= 2
    return x


def _short_conv_silu(x, w):
    n = x.shape[0]
    row = lax.broadcasted_iota(jnp.int32, x.shape, 0)
    pad = CONV_W // 2
    y = x * w[pad:pad + 1, :]
    for t in range(CONV_W):
        s = t - pad
        if s == 0:
            continue
        shifted = pltpu.roll(x, (-s) % n, 0)
        valid = (row + s >= 0) & (row + s < n)
        y = y + jnp.where(valid, shifted, 0.0) * w[t:t + 1, :]
    return jax.nn.silu(y)


def _unit_tri_solve(ms, rhss, ri, ci):
    n = rhss[0].shape[-1]
    same16 = (ri // 16) == (ci // 16)
    dot = functools.partial(jnp.dot, preferred_element_type=F32)

    def apply(p, r):
        hi = r.astype(BF16)
        lo = (r - hi.astype(F32)).astype(BF16)
        x = dot(p, jnp.concatenate([hi, lo], axis=-1))
        return x[:, 0:n] + x[:, n:2 * n]

    diag = [jnp.where(same16, m, 0.0) for m in ms]
    rest = [jnp.where(same16, 0.0, m).astype(BF16) for m in ms]
    pows = [d.astype(BF16) for d in diag]
    tinv = [jnp.where(ri == ci, 1.0, 0.0) - d for d in diag]
    for _ in range(3):
        pows = [dot(p, p).astype(BF16) for p in pows]
        tinv = [t + dot(t.astype(BF16), p) for t, p in zip(tinv, pows)]
    tinv = [t.astype(BF16) for t in tinv]
    p1 = [dot(t, e).astype(BF16) for t, e in zip(tinv, rest)]
    sols = [apply(t, r) for t, r in zip(tinv, rhss)]
    p2 = [dot(p, p).astype(BF16) for p in p1]
    sols = [s - apply(p, s) for p, s in zip(p1, sols)]
    return [s + apply(p, s) for p, s in zip(p2, sols)]


def _deltanet_kernel(q_ref, k_ref, v_ref, z_ref, a_ref, bb_ref, cwq_ref, cwk_ref, cwv_ref, alog_ref, dtb_ref, nw_ref,
                     s0_ref, y_ref, sfin_ref,
                     qs_ref, ks_ref, vs_ref, o_ref, gc_ref, beta_ref, u_ref, w_ref, attn_ref, qp_ref, kpt_ref,
                     etot_ref, *, seq):
    hp = pl.program_id(1)
    nchunk = seq // DN_CHUNK
    c64 = DN_CHUNK
    head_lanes = [slice(j * HEAD_DIM, (j + 1) * HEAD_DIM) for j in range(2)]

    q = _short_conv_silu(q_ref[0], cwq_ref[...])
    q = q * lax.rsqrt(_head_sums(q * q) + EPS) * (DK ** -0.5)
    k = _short_conv_silu(k_ref[0], cwk_ref[...])
    k = k * lax.rsqrt(_head_sums(k * k) + EPS)
    v = _short_conv_silu(v_ref[0], cwv_ref[...])
    qs_ref[...] = q
    ks_ref[...] = k
    vs_ref[...] = v
    o_ref[...] = jnp.zeros_like(o_ref)

    g_col = -jnp.exp(alog_ref[...]) * jax.nn.softplus(a_ref[0] + dtb_ref[...])
    b_col = jax.nn.sigmoid(bb_ref[0])
    sel_row = lax.broadcasted_iota(jnp.int32, (2 * HA, 2 * LANES), 0)
    sel_lane = lax.broadcasted_iota(jnp.int32, (2 * HA, 2 * LANES), 1)
    sel = (sel_row == (sel_lane // LANES) * HA + 2 * hp + (sel_lane // HEAD_DIM) % 2).astype(BF16)
    g_b = _select_dot(g_col, sel)
    gc_ref[:, 0:LANES] = _chunk_cumsum(g_b[:, 0:LANES], reverse=False)
    gc_ref[:, LANES:2 * LANES] = _chunk_cumsum(g_b[:, LANES:2 * LANES], reverse=True)
    beta_ref[...] = _select_dot(b_col, sel)

    ri = lax.broadcasted_iota(jnp.int32, (c64, c64), 0)
    ci = lax.broadcasted_iota(jnp.int32, (c64, c64), 1)

    def intra_chunk(step, carry):
        chunk_ids = [step * DN_UNROLL + t for t in range(DN_UNROLL)]
        rows_c = [pl.ds(pl.multiple_of(c * c64, c64), c64) for c in chunk_ids]
        loads = [(qs_ref[r, :], ks_ref[r, :], vs_ref[r, :], gc_ref[r, :], beta_ref[r, :]) for r in rows_c]
        heads = [(t, j, ld[0][:, head_lanes[j]], ld[1][:, head_lanes[j]], ld[2][:, head_lanes[j]])
                 for t, ld in enumerate(loads) for j in range(2)]
        gate_lanes = [slice((c % 2) * LANES + (c // 2) * HEAD_DIM, (c % 2) * LANES + (c // 2 + 1) * HEAD_DIM)
                      for c in range(4)]
        gates = {(t, c): (ld[3][:, gate_lanes[c]], ld[4][:, gate_lanes[c]])
                 for t, ld in enumerate(loads) for c in range(4)}
        kbs = [h[3].astype(BF16) for h in heads]
        kks = [_nt_dot(kb, kb) for kb in kbs]
        qks = [_nt_dot(h[2].astype(BF16), kb) for h, kb in zip(heads, kbs)]
        chains, ms, rhss = [], [], []
        for (t, j, qj, kj, vj), kk, qk in zip(heads, kks, qks):
            for d in range(2):
                gcb, bet = gates[(t, 2 * j + d)]
                diff = gcb - gcb.T
                earlier = (ri >= ci) if d == 0 else (ri <= ci)
                strict = (ri > ci) if d == 0 else (ri < ci)
                dec = jnp.exp(jnp.where(earlier, diff, NEG_BIG))
                eg = jnp.exp(gcb)
                gtot = gcb[c64 - 1:c64, :] if d == 0 else gcb[0:1, :]
                ms.append(jnp.where(strict, kk * bet * dec, 0.0))
                rhss.append(jnp.concatenate([vj * bet, kj * bet * eg], axis=-1))
                chains.append((t, 2 * j + d, qk * dec, qj * eg, (kj * jnp.exp(gtot - gcb)).T,
                               jnp.broadcast_to(jnp.exp(gtot), (SUBLANES, HEAD_DIM))))
        uws = _unit_tri_solve(ms, rhss, ri, ci)
        for (t, c, attn, qp, kpt, etot), uw in zip(chains, uws):
            u_ref[c, rows_c[t], :] = uw[:, 0:HEAD_DIM]
            w_ref[c, rows_c[t], :] = uw[:, HEAD_DIM:2 * HEAD_DIM].astype(BF16)
            attn_ref[c, rows_c[t], :] = attn.astype(BF16)
            qp_ref[c, rows_c[t], :] = qp.astype(BF16)
            kpt_ref[c, rows_c[t], :] = kpt.astype(BF16)
            etot_ref[c, pl.ds(pl.multiple_of(chunk_ids[t] * SUBLANES, SUBLANES), SUBLANES), :] = etot
        return carry

    lax.fori_loop(0, nchunk // DN_UNROLL, intra_chunk, 0)

    def inter_chunk(i, states):
        rows_d = [pl.ds(pl.multiple_of(cc * c64, c64), c64) for cc in (i, nchunk - 1 - i)]
        erow_d = [pl.ds(pl.multiple_of(cc * SUBLANES, SUBLANES), SUBLANES) for cc in (i, nchunk - 1 - i)]
        loaded = [(w_ref[c, rows_d[c % 2], :], qp_ref[c, rows_d[c % 2], :], u_ref[c, rows_d[c % 2], :],
                   attn_ref[c, rows_d[c % 2], :], kpt_ref[c, rows_d[c % 2], :], etot_ref[c, erow_d[c % 2], :],
                   o_ref[c // 2, rows_d[c % 2], :]) for c in range(4)]
        ws_qs = [jnp.dot(jnp.concatenate([ld[0], ld[1]], axis=0), s.astype(BF16), preferred_element_type=F32)
                 for ld, s in zip(loaded, states)]
        v_new = [(ld[2] - r[0:c64]).astype(BF16) for ld, r in zip(loaded, ws_qs)]
        av = [jnp.dot(ld[3], vn, preferred_element_type=F32) for ld, vn in zip(loaded, v_new)]
        kv = [jnp.dot(ld[4], vn, preferred_element_type=F32) for ld, vn in zip(loaded, v_new)]
        for c in range(4):
            o_ref[c // 2, rows_d[c % 2], :] = loaded[c][6] + ws_qs[c][c64:2 * c64] + av[c]
        return tuple(s * ld[5][0:1, :] + x for s, ld, x in zip(states, loaded, kv))

    init = tuple(s0_ref[0, c % 2, c // 2] for c in range(4))
    fin = lax.fori_loop(0, nchunk, inter_chunk, init)
    for c in range(4):
        sfin_ref[0, c % 2, c // 2] = fin[c]

    o = jnp.concatenate([o_ref[0], o_ref[1]], axis=-1)
    o = o * lax.rsqrt(_head_sums(o * o) * (1.0 / DK) + EPS) * nw_ref[...]
    y_ref[0] = o * jax.nn.silu(z_ref[0])


def deltanet_pallas(qkv, z, a, bb, ep, s0):
    b, l, _ = qkv.shape
    npair = DN_PAIRS

    def col_spec(off):
        return pl.BlockSpec((1, l, LANES), lambda i, hp: (i, 0, off + hp))

    def cw_spec(off):
        return pl.BlockSpec((CONV_W, LANES), lambda i, hp: (0, off + hp))

    def full(shape):
        return pl.BlockSpec(shape, lambda i, hp: (0,) * len(shape))

    state_spec = pl.BlockSpec((1, 2, 2, DK, DK), lambda i, hp: (i, 0, hp, 0, 0))
    row_f32 = pltpu.VMEM((l, LANES), F32)
    gate_f32 = pltpu.VMEM((l, 2 * LANES), F32)
    chain_f32 = pltpu.VMEM((4, l, HEAD_DIM), F32)
    chain_bf16 = pltpu.VMEM((4, l, HEAD_DIM), BF16)
    y, s_fin = pl.pallas_call(
        functools.partial(_deltanet_kernel, seq=l),
        grid=(b, npair),
        in_specs=[col_spec(0), col_spec(npair), col_spec(2 * npair), col_spec(0),
                  pl.BlockSpec((1, l, 2 * HA), lambda i, hp: (i, 0, 0)),
                  pl.BlockSpec((1, l, 2 * HA), lambda i, hp: (i, 0, 0)),
                  cw_spec(0), cw_spec(npair), cw_spec(2 * npair),
                  full((1, 2 * HA)), full((1, 2 * HA)), full((1, LANES)), state_spec],
        out_specs=[col_spec(0), state_spec],
        out_shape=[jax.ShapeDtypeStruct((b, l, HA * DK), F32), jax.ShapeDtypeStruct((b, 2, HA, DK, DK), F32)],
        scratch_shapes=[row_f32, row_f32, row_f32, pltpu.VMEM((2, l, HEAD_DIM), F32), gate_f32, gate_f32,
                        chain_f32, chain_bf16, chain_bf16, chain_bf16, chain_bf16,
                        pltpu.VMEM((4, l // DN_CHUNK * SUBLANES, HEAD_DIM), F32)],
        compiler_params=pltpu.CompilerParams(dimension_semantics=("arbitrary", "arbitrary"),
                                             vmem_limit_bytes=VMEM_LIMIT),
        name="deltanet",
    )(qkv, qkv, qkv, z, a, bb, ep['conv_w'], ep['conv_w'], ep['conv_w'],
      ep['a_log'].astype(F32).reshape(1, 2 * HA), ep['dt_bias'].astype(F32).reshape(1, 2 * HA),
      jnp.tile(ep['norm_w'].astype(F32), 2).reshape(1, LANES), s0)
    return y, s_fin


TM = 512
ATT_TQ = BLK
ROT = HEAD_DIM // 4


def rope_tables():
    cos, sin = axial_rope_tables(DEC_SEQ // GRID_W)
    cos2 = jnp.tile(cos, (1, LANES // HEAD_DIM))
    sin2 = jnp.tile(sin, (1, LANES // HEAD_DIM))
    first = (jnp.arange(LANES) % (2 * ROT)) < ROT
    return cos2, jnp.where(first, -sin2, 0.0), jnp.where(first, 0.0, sin2)


def _prep_kernel(q_ref, k_ref, v_ref, *refs, norm, rope, k_f32):
    refs = list(refs)
    qw_ref, kw_ref = (refs.pop(0), refs.pop(0)) if norm else (None, None)
    cos_ref, sup_ref, sdn_ref = (refs.pop(0), refs.pop(0), refs.pop(0)) if rope else (None, None, None)
    qo_ref, ko_ref, vo_ref = refs[0:3]

    def process(x_ref, w_ref, scale):
        outs, normed = [], []
        for c in range(x_ref.shape[1] // LANES):
            xc = x_ref[:, c * LANES:(c + 1) * LANES]
            if norm:
                xc = xc * lax.rsqrt(_head_sums(xc * xc) * (1.0 / HEAD_DIM) + EPS) * w_ref[...]
                normed.append(xc)
            if rope:
                xc = (xc * cos_ref[...] + pltpu.roll(xc, LANES - ROT, 1) * sup_ref[...]
                      + pltpu.roll(xc, ROT, 1) * sdn_ref[...])
            outs.append((xc * scale).astype(BF16))
        cat = lambda parts: jnp.concatenate(parts, axis=-1) if len(parts) > 1 else parts[0]
        return cat(outs), (cat(normed) if norm else None)

    qo_ref[...] = process(q_ref, qw_ref, HEAD_DIM ** -0.5)[0]
    ko, kn = process(k_ref, kw_ref, 1.0)
    ko_ref[...] = ko
    if k_f32:
        refs[3][...] = kn
    vo_ref[...] = v_ref[...].astype(BF16)


def prep_pallas(q, k, v, norm_ws, rope_tabs, k_f32):
    t = q.shape[0]
    row = lambda i: (i, 0)
    ins = [q, k, v]
    in_specs = [pl.BlockSpec((TM, a.shape[1]), row) for a in ins]
    if norm_ws is not None:
        ins += [jnp.tile(w.astype(F32), LANES // HEAD_DIM).reshape(1, LANES) for w in norm_ws]
        in_specs += [pl.BlockSpec((1, LANES), lambda i: (0, 0))] * 2
    if rope_tabs is not None:
        tiles = DEC_SEQ // TM
        ins += list(rope_tabs)
        in_specs += [pl.BlockSpec((TM, LANES), lambda i: (i % tiles, 0))] * 3
    outs = [(a.shape, BF16) for a in (q, k, v)] + ([(k.shape, F32)] if k_f32 else [])
    return pl.pallas_call(
        functools.partial(_prep_kernel, norm=norm_ws is not None, rope=rope_tabs is not None, k_f32=k_f32),
        grid=(t // TM,),
        in_specs=in_specs,
        out_specs=[pl.BlockSpec((TM, s[1]), row) for s, _ in outs],
        out_shape=[jax.ShapeDtypeStruct(s, dt) for s, dt in outs],
        name="qkv_prep",
    )(*ins)


def _attn_kernel(q_ref, k_ref, v_ref, *refs, grp, hk, windowed, has_ctx, has_sink):
    refs = list(refs)
    kc_ref, vc_ref = (refs.pop(0), refs.pop(0)) if has_ctx else (None, None)
    sink_ref = refs.pop(0) if has_sink else None
    o_ref = refs[0]
    tq = q_ref.shape[1]
    seq_k = k_ref.shape[1]
    if windowed:
        qi = pl.program_id(1)
        start = jnp.clip((qi - 1) * tq, 0, seq_k - 3 * tq)
        krows = pl.ds(pl.multiple_of(start, tq), 3 * tq)
        qpos = qi * tq + lax.broadcasted_iota(jnp.int32, (grp * tq, 3 * tq), 0) % tq
        kpos = start + lax.broadcasted_iota(jnp.int32, (grp * tq, 3 * tq), 1)
        keep = jnp.abs(qpos - kpos) <= WINDOW
    else:
        krows = pl.ds(0, seq_k)
    q = q_ref[0]
    outs = [None] * (grp * hk)
    for g in range(hk):
        gl = slice(g * HEAD_DIM, (g + 1) * HEAD_DIM)
        qg = jnp.concatenate([q[:, (g * grp + j) * HEAD_DIM:(g * grp + j + 1) * HEAD_DIM] for j in range(grp)],
                             axis=0)
        s = _nt_dot(qg, k_ref[0, krows, gl])
        if windowed:
            s = jnp.where(keep, s, NEG_BIG)
        m = jnp.max(s, axis=-1, keepdims=True)
        if has_ctx:
            sc = _nt_dot(qg, kc_ref[0][:, gl].astype(BF16))
            m = jnp.maximum(m, jnp.max(sc, axis=-1, keepdims=True))
        if has_sink:
            sk = jnp.concatenate([jnp.broadcast_to(sink_ref[0:1, g * grp + j:g * grp + j + 1], (tq, 1))
                                  for j in range(grp)], axis=0)
            m = jnp.maximum(m, sk)
        p = jnp.exp(s - m)
        den = jnp.sum(p, axis=-1, keepdims=True)
        o = jnp.dot(p.astype(BF16), v_ref[0, krows, gl], preferred_element_type=F32)
        if has_ctx:
            pc = jnp.exp(sc - m)
            den = den + jnp.sum(pc, axis=-1, keepdims=True)
            o = o + jnp.dot(pc.astype(BF16), vc_ref[0][:, gl].astype(BF16), preferred_element_type=F32)
        if has_sink:
            den = den + jnp.exp(sk - m)
        o = o / den
        for j in range(grp):
            outs[g * grp + j] = o[j * tq:(j + 1) * tq]
    o_ref[0] = jnp.concatenate(outs, axis=-1)


def attention_pallas(q, k, v, ctx=None, sink=None, windowed=False):
    b, lq, hd = q.shape
    lk, kd = k.shape[1], k.shape[2]
    hk = kd // HEAD_DIM
    grp = hd // kd
    tq = ATT_TQ
    ins = [q, k, v]
    in_specs = [pl.BlockSpec((1, tq, hd), lambda i, t: (i, t, 0)),
                pl.BlockSpec((1, lk, kd), lambda i, t: (i, 0, 0)),
                pl.BlockSpec((1, lk, kd), lambda i, t: (i, 0, 0))]
    if ctx is not None:
        ins += list(ctx)
        in_specs += [pl.BlockSpec((1, ctx[0].shape[1], kd), lambda i, t: (i, 0, 0))] * 2
    if sink is not None:
        ins.append(sink.astype(F32).reshape(1, -1))
        in_specs.append(pl.BlockSpec((1, sink.shape[0]), lambda i, t: (0, 0)))
    return pl.pallas_call(
        functools.partial(_attn_kernel, grp=grp, hk=hk, windowed=windowed, has_ctx=ctx is not None,
                          has_sink=sink is not None),
        grid=(b, lq // tq),
        in_specs=in_specs,
        out_specs=pl.BlockSpec((1, tq, hd), lambda i, t: (i, t, 0)),
        out_shape=jax.ShapeDtypeStruct((b, lq, hd), F32),
        compiler_params=pltpu.CompilerParams(dimension_semantics=("arbitrary", "arbitrary"),
                                             vmem_limit_bytes=VMEM_LIMIT),
        name="attention",
    )(*ins)


def even_mixer_context(proj, ep):
    qkv, z, ab, bq, bk, bv = proj
    qkv, z, ab = [t.reshape(BATCH, SEQ, -1) for t in (qkv, z, ab)]
    a, bb = ab[..., 0:2 * HA], ab[..., 2 * HA:4 * HA]
    y_a, dn_fin = deltanet_pallas(qkv, z, a, bb, ep, jnp.zeros((BATCH, 2, HA, DK, DK), F32))
    qh, kh, vh = [t.reshape(BATCH, SEQ, -1) for t in prep_pallas(bq, bk, bv, None, None, False)]
    y_b = attention_pallas(qh, kh, vh, sink=ep['sink'])
    return (y_a.reshape(-1, HA * DK), y_b.reshape(-1, HB * HEAD_DIM), dn_fin,
            bk.reshape(BATCH, SEQ, HKV_B, HEAD_DIM), bv.reshape(BATCH, SEQ, HKV_B, HEAD_DIM))


def even_mixer_latent(proj, ep, dn_state, ctx_k, ctx_v, rope_tabs):
    qkv, z, ab, bq, bk, bv = proj
    qkv, z, ab = [t.reshape(DEC_BATCH, DEC_SEQ, -1) for t in (qkv, z, ab)]
    a, bb = ab[..., 0:2 * HA], ab[..., 2 * HA:4 * HA]
    y_a, _ = deltanet_pallas(qkv, z, a, bb, ep, dn_state.astype(F32))
    qh, kh, vh = [t.reshape(DEC_BATCH, DEC_SEQ, -1) for t in prep_pallas(bq, bk, bv, None, rope_tabs, False)]
    ctx = (ctx_k.reshape(DEC_BATCH, PAST_LEN, -1), ctx_v.reshape(DEC_BATCH, PAST_LEN, -1))
    y_b = attention_pallas(qh, kh, vh, ctx=ctx, sink=ep['sink'], windowed=True)
    return y_a.reshape(-1, HA * DK), y_b.reshape(-1, HB * HEAD_DIM)


def odd_mixer_context(proj, op):
    cq, ck, cv, u = proj
    qh, kh, vh, k_normed = prep_pallas(cq, ck, cv, (op['q_norm'], op['k_norm']), None, True)
    y_c = attention_pallas(*[t.reshape(BATCH, SEQ, -1) for t in (qh, kh, vh)])
    y_d, s_fin = s5_pallas(u.reshape(BATCH, SEQ, -1), op, s5_operators(op), jnp.zeros((2, BATCH, 2 * S5_STATE), F32))
    s_fin = jnp.transpose(s_fin.reshape(2, BATCH, 2, G_D, S5_P), (1, 0, 3, 4, 2))
    return (y_c.reshape(-1, HC * HEAD_DIM), y_d.reshape(-1, D_S5), k_normed.reshape(BATCH, SEQ, HKV_C, HEAD_DIM),
            cv.reshape(BATCH, SEQ, HKV_C, HEAD_DIM), s_fin)


def odd_mixer_latent(proj, op, ctx_k, ctx_v, s5_state, rope_tabs):
    cq, ck, cv, u = proj
    qh, kh, vh = prep_pallas(cq, ck, cv, (op['q_norm'], op['k_norm']), rope_tabs, False)
    ctx = (ctx_k.reshape(DEC_BATCH, PAST_LEN, -1), ctx_v.reshape(DEC_BATCH, PAST_LEN, -1))
    y_c = attention_pallas(*[t.reshape(DEC_BATCH, DEC_SEQ, -1) for t in (qh, kh, vh)], ctx=ctx)
    s0 = jnp.transpose(s5_state.astype(F32), (1, 0, 4, 2, 3)).reshape(2, DEC_BATCH, 2 * S5_STATE)
    y_d, _ = s5_pallas(u.reshape(DEC_BATCH, DEC_SEQ, -1), op, s5_operators(op), s0)
    return y_c.reshape(-1, HC * HEAD_DIM), y_d.reshape(-1, D_S5)


ADA_ROWS = 16
ADA_TN = 1024
MOD_ROWS = 8
ROUTER_LANES = LANES
MOE_FCHUNK = EXP_PER_GROUP * EXPERT_FF


def _ada_kernel(c_ref, w_ref, b_ref, o_ref):
    o_ref[0] = _bf16_dot(jax.nn.silu(c_ref[...]), w_ref[0]) + b_ref[0]


def ada_pallas(cvec, ada_w, ada_b):
    n = 6 * D_MODEL
    return pl.pallas_call(
        _ada_kernel,
        grid=(DEPTH, n // ADA_TN),
        in_specs=[pl.BlockSpec((ADA_ROWS, D_MODEL), lambda l, j: (0, 0)),
                  pl.BlockSpec((1, D_MODEL, ADA_TN), lambda l, j: (l, 0, j)),
                  pl.BlockSpec((1, 1, ADA_TN), lambda l, j: (l, 0, j))],
        out_specs=pl.BlockSpec((1, ADA_ROWS, ADA_TN), lambda l, j: (l, 0, j)),
        out_shape=jax.ShapeDtypeStruct((DEPTH, ADA_ROWS, n), F32),
        name="ada_modulation",
    )(cvec, ada_w, ada_b.reshape(DEPTH, 1, n))


def _modulated(x, nw, shift, scale):
    return x * lax.rsqrt(jnp.mean(x * x, axis=-1, keepdims=True) + EPS) * nw * (1 + scale) + shift


def _inproj_kernel(x_ref, mod_ref, nw_ref, *refs):
    nout = len(refs) // 2
    m = mod_ref[0]
    h = _modulated(x_ref[...], nw_ref[...], m[0:1], m[1:2]).astype(BF16)
    for w_ref, o_ref in zip(refs[:nout], refs[nout:]):
        o_ref[...] = jnp.dot(h, w_ref[...], preferred_element_type=F32)


def inproj_pallas(x, mod, nw, weights, rows_per_mod):
    t = x.shape[0]
    tiles_per_mod = rows_per_mod // TM
    return pl.pallas_call(
        _inproj_kernel,
        grid=(t // TM,),
        in_specs=[pl.BlockSpec((TM, D_MODEL), lambda i: (i, 0)),
                  pl.BlockSpec((1, MOD_ROWS, D_MODEL), lambda i: (i // tiles_per_mod, 0, 0)),
                  pl.BlockSpec((1, D_MODEL), lambda i: (0, 0))]
        + [pl.BlockSpec(w.shape, lambda i: (0, 0)) for w in weights],
        out_specs=[pl.BlockSpec((TM, w.shape[1]), lambda i: (i, 0)) for w in weights],
        out_shape=[jax.ShapeDtypeStruct((t, w.shape[1]), F32) for w in weights],
        compiler_params=pltpu.CompilerParams(dimension_semantics=("arbitrary",), vmem_limit_bytes=VMEM_LIMIT),
        name="modulated_in_proj",
    )(x, mod, nw.reshape(1, D_MODEL), *weights)


def _moe_kernel(x_ref, ya_ref, yb_ref, mod_ref, nw_ref, woa_ref, wob_ref, wr_ref, br_ref, wg_ref, wu_ref, wd_ref,
                o_ref, x1_ref, h_ref, comb_ref, acc_ref):
    f = pl.program_id(1)
    m = mod_ref[0]

    @pl.when(f == 0)
    def _():
        y = _bf16_dot(ya_ref[...], woa_ref[...]) + _bf16_dot(yb_ref[...], wob_ref[...])
        x1 = x_ref[...] + m[2:3] * y
        x1_ref[...] = x1
        h = _modulated(x1, nw_ref[...], m[3:4], m[4:5])
        h_hi = h.astype(BF16)
        h_ref[...] = h_hi
        h_lo = (h - h_hi.astype(F32)).astype(BF16)
        logits = (jnp.dot(jnp.concatenate([h_hi, h_lo], axis=-1), wr_ref[0:2 * D_MODEL, :], preferred_element_type=F32)
                  + jnp.dot(h_hi, wr_ref[2 * D_MODEL:3 * D_MODEL, :], preferred_element_type=F32) + br_ref[...])
        lane = lax.broadcasted_iota(jnp.int32, logits.shape, 1)
        lane_f = lane.astype(F32)
        is_group = lane < N_GROUPS
        lg = jnp.where(is_group, logits, -jnp.inf)
        g_max = jnp.max(lg, axis=-1, keepdims=True)
        g_idx = jnp.min(jnp.where(lg == g_max, lane_f, float(ROUTER_LANES)), axis=-1, keepdims=True)
        gate_g = 1.0 / jnp.sum(jnp.where(is_group, jnp.exp(logits - g_max), 0.0), axis=-1, keepdims=True)
        lane_group = ((lane + (EXP_PER_GROUP - N_GROUPS)) // EXP_PER_GROUP - 1).astype(F32)
        in_group = (lane >= N_GROUPS) & (lane < N_GROUPS + N_EXP) & (lane_group == g_idx)
        le = jnp.where(in_group, logits, -jnp.inf)
        v1 = jnp.max(le, axis=-1, keepdims=True)
        i1 = jnp.min(jnp.where(le == v1, lane_f, float(ROUTER_LANES)), axis=-1, keepdims=True)
        le2 = jnp.where(lane_f == i1, -jnp.inf, le)
        v2 = jnp.max(le2, axis=-1, keepdims=True)
        i2 = jnp.min(jnp.where(le2 == v2, lane_f, float(ROUTER_LANES)), axis=-1, keepdims=True)
        e2 = jnp.exp(v2 - v1)
        p1 = gate_g / (1.0 + e2)
        comb = jnp.where(lane_f == i1, p1, 0.0) + jnp.where(lane_f == i2, p1 * e2, 0.0)
        for grp in range(N_GROUPS):
            comb_ref[grp] = pltpu.roll(comb, ROUTER_LANES - (N_GROUPS + grp * EXP_PER_GROUP), 1)
        acc_ref[...] = jnp.zeros_like(acc_ref)

    h = h_ref[...]
    hid = jax.nn.silu(jnp.dot(h, wg_ref[...], preferred_element_type=F32)) * jnp.dot(
        h, wu_ref[...], preferred_element_type=F32)
    comb = comb_ref[f]
    hid = jnp.concatenate([hid[:, k * EXPERT_FF:(k + 1) * EXPERT_FF] * comb[:, k:k + 1]
                           for k in range(EXP_PER_GROUP)], axis=-1)
    acc_ref[...] += jnp.dot(hid.astype(BF16), wd_ref[...], preferred_element_type=F32)

    @pl.when(f == N_GROUPS - 1)
    def _():
        o_ref[...] = x1_ref[...] + m[5:6] * acc_ref[...]


def moe_pallas(x, ya, yb, mod, nw, wo_a, wo_b, wr, br, wg, wu, wd, rows_per_mod):
    t = x.shape[0]
    tiles_per_mod = rows_per_mod // TM
    row = lambda i, f: (i, 0)
    const = lambda i, f: (0, 0)
    return pl.pallas_call(
        _moe_kernel,
        grid=(t // TM, N_GROUPS),
        in_specs=[pl.BlockSpec((TM, D_MODEL), row),
                  pl.BlockSpec((TM, ya.shape[1]), row),
                  pl.BlockSpec((TM, yb.shape[1]), row),
                  pl.BlockSpec((1, MOD_ROWS, D_MODEL), lambda i, f: (i // tiles_per_mod, 0, 0)),
                  pl.BlockSpec((1, D_MODEL), const),
                  pl.BlockSpec(wo_a.shape, const),
                  pl.BlockSpec(wo_b.shape, const),
                  pl.BlockSpec((3 * D_MODEL, ROUTER_LANES), const),
                  pl.BlockSpec((1, ROUTER_LANES), const),
                  pl.BlockSpec((D_MODEL, MOE_FCHUNK), lambda i, f: (0, f)),
                  pl.BlockSpec((D_MODEL, MOE_FCHUNK), lambda i, f: (0, f)),
                  pl.BlockSpec((MOE_FCHUNK, D_MODEL), lambda i, f: (f, 0))],
        out_specs=pl.BlockSpec((TM, D_MODEL), row),
        out_shape=jax.ShapeDtypeStruct((t, D_MODEL), F32),
        scratch_shapes=[pltpu.VMEM((TM, D_MODEL), F32), pltpu.VMEM((TM, D_MODEL), BF16),
                        pltpu.VMEM((N_GROUPS, TM, ROUTER_LANES), F32), pltpu.VMEM((TM, D_MODEL), F32)],
        compiler_params=pltpu.CompilerParams(dimension_semantics=("arbitrary", "arbitrary"),
                                             vmem_limit_bytes=VMEM_LIMIT),
        name="out_proj_moe",
    )(x, ya, yb, mod, nw.reshape(1, D_MODEL), wo_a, wo_b, wr, br, wg, wu, wd)


def _final_norm_kernel(x_ref, w_ref, o_ref):
    x = x_ref[...]
    y = x * lax.rsqrt(jnp.mean(x * x, axis=-1, keepdims=True) + EPS)
    o_ref[...] = y * w_ref[...]


def final_norm(x, w):
    t, d = x.shape
    return pl.pallas_call(
        _final_norm_kernel,
        grid=(t // TM,),
        in_specs=[pl.BlockSpec((TM, d), lambda i: (i, 0)), pl.BlockSpec((1, d), lambda i: (0, 0))],
        out_specs=pl.BlockSpec((TM, d), lambda i: (i, 0)),
        out_shape=jax.ShapeDtypeStruct((t, d), F32),
        name="final_norm",
    )(x, w.reshape(1, d))


def _pad_cols(w, n):
    return jnp.pad(w, ((0, 0), (0, n - w.shape[1])))


def _router_weights(rg, re):
    w = _pad_cols(jnp.concatenate([rg, re], axis=1).astype(F32), ROUTER_LANES)
    hi = w.astype(BF16)
    lo = (w - hi.astype(F32)).astype(BF16)
    return jnp.concatenate([hi, hi, lo], axis=0)


def _mod_table(m):
    m = m.reshape(m.shape[0], 6, D_MODEL)
    return jnp.pad(m, ((0, 0), (0, MOD_ROWS - 6), (0, 0)))


def kernel(x_prompt, x_sample, c, c_ctx, state_dn, cache_b_k, cache_b_v, cache_c_k, cache_c_v, state_s5,
           ada_w, ada_b, norm_w, w_in_e, dn_conv_w, dn_a_log, dn_dt_bias, dn_norm_w, b_sink, w_out_e,
           w_in_o, c_q_norm, c_k_norm, s5_lam_re, s5_lam_im, s5_log_dt, s5_b_re, s5_b_im, s5_c_re, s5_c_im,
           s5_d, s5_glu_w, s5_glu_b, w_out_o, moe_rg, moe_rg_b, moe_re, moe_re_b, moe_wg, moe_wu, moe_wd,
           final_norm_w):
    rope_tabs = rope_tables()
    cvec = jnp.concatenate([c_ctx[None, :], c, jnp.zeros((ADA_ROWS - 1 - DEC_BATCH, D_MODEL), F32)], axis=0)
    mods = ada_pallas(cvec, ada_w, ada_b)
    xp = x_prompt.reshape(BATCH * SEQ, D_MODEL)
    xs = x_sample.reshape(DEC_BATCH * DEC_SEQ, D_MODEL)
    new_dn, new_bk, new_bv, new_ck, new_cv, new_s5 = [], [], [], [], [], []
    for layer in range(DEPTH):
        mod_p = _mod_table(mods[layer, 0:1])
        mod_s = _mod_table(mods[layer, 1:1 + DEC_BATCH])
        i = layer // 2
        if layer % 2 == 0:
            ep = {'conv_w': dn_conv_w[i], 'a_log': dn_a_log[i], 'dt_bias': dn_dt_bias[i],
                  'norm_w': dn_norm_w[i], 'sink': b_sink[i]}
            qkv_w, z_w, a_w, bb_w, bq_w, bk_w, bv_w = split_cols(w_in_e[i].astype(BF16), EVEN_SPLITS)
            w_in = [qkv_w, z_w, _pad_cols(jnp.concatenate([a_w, bb_w], axis=1), LANES), bq_w, bk_w, bv_w]
            w_out = w_out_e[i].astype(BF16)
            wo_a, wo_b = w_out[:HA * DK], w_out[HA * DK:]
            proj_p = inproj_pallas(xp, mod_p, norm_w[layer, 0], w_in, BATCH * SEQ)
            proj_s = inproj_pallas(xs, mod_s, norm_w[layer, 0], w_in, DEC_SEQ)
            ya_p, yb_p, dn, bk, bv = even_mixer_context(proj_p, ep)
            ya_s, yb_s = even_mixer_latent(proj_s, ep, state_dn[:, i], cache_b_k[:, i], cache_b_v[:, i], rope_tabs)
            new_dn.append(dn)
            new_bk.append(bk)
            new_bv.append(bv)
        else:
            op = {'q_norm': c_q_norm[i], 'k_norm': c_k_norm[i], 'lam_re': s5_lam_re[i],
                  'lam_im': s5_lam_im[i], 'log_dt': s5_log_dt[i], 'b_re': s5_b_re[i], 'b_im': s5_b_im[i],
                  'c_re': s5_c_re[i], 'c_im': s5_c_im[i], 'd': s5_d[i], 'glu_w': s5_glu_w[i],
                  'glu_b': s5_glu_b[i]}
            w_in = split_cols(w_in_o[i].astype(BF16), ODD_SPLITS)
            w_out = w_out_o[i].astype(BF16)
            wo_a, wo_b = w_out[:HC * HEAD_DIM], w_out[HC * HEAD_DIM:]
            proj_p = inproj_pallas(xp, mod_p, norm_w[layer, 0], w_in, BATCH * SEQ)
            proj_s = inproj_pallas(xs, mod_s, norm_w[layer, 0], w_in, DEC_SEQ)
            ya_p, yb_p, ck, cv, s5 = odd_mixer_context(proj_p, op)
            ya_s, yb_s = odd_mixer_latent(proj_s, op, cache_c_k[:, i], cache_c_v[:, i], state_s5[:, i], rope_tabs)
            new_ck.append(ck)
            new_cv.append(cv)
            new_s5.append(s5)
        wr = _router_weights(moe_rg[layer], moe_re[layer])
        br = _pad_cols(jnp.concatenate([moe_rg_b[layer], moe_re_b[layer]])[None, :], ROUTER_LANES)
        moe_w = (wo_a, wo_b, wr, br, moe_wg[layer].astype(BF16), moe_wu[layer].astype(BF16),
                 moe_wd[layer].astype(BF16))
        xp = moe_pallas(xp, ya_p, yb_p, mod_p, norm_w[layer, 1], *moe_w, BATCH * SEQ)
        xs = moe_pallas(xs, ya_s, yb_s, mod_s, norm_w[layer, 1], *moe_w, DEC_SEQ)
    y_prompt = final_norm(xp, final_norm_w).reshape(x_prompt.shape)
    y_sample = final_norm(xs, final_norm_w).reshape(x_sample.shape)
    return (y_prompt, y_sample, jnp.stack(new_dn, axis=1), jnp.stack(new_bk, axis=1), jnp.stack(new_bv, axis=1),
            jnp.stack(new_ck, axis=1), jnp.stack(new_cv, axis=1), jnp.stack(new_s5, axis=1))
```

```python
import functools

import numpy as np
import jax
import jax.numpy as jnp
from jax import lax
from jax.experimental import pallas as pl
from jax.experimental.pallas import tpu as pltpu

D_MODEL = 1024
BATCH = 32
SEQ = 256
DEPTH = 2
DEC_BATCH = 8
DEC_SEQ = 2048
PAST_LEN = 256

GRID_W = 64
HEAD_DIM = 64
BLK = 128
ROPE_THETA = 10000.0
EPS = 1e-6
N_EVEN = (DEPTH + 1) // 2
N_ODD = DEPTH // 2
HA = D_MODEL // 128
DK = HEAD_DIM
CONV_W = 5
DN_CHUNK = 64
HB = D_MODEL // 128
HKV_B = HB // 4
WINDOW = 128
HC = (3 * D_MODEL // 4) // HEAD_DIM
HKV_C = HC // 3
D_S5 = D_MODEL // 4
S5_CH = 16
G_D = D_S5 // S5_CH
S5_P = 64
N_GROUPS = 4
EXP_PER_GROUP = 8
N_EXP = N_GROUPS * EXP_PER_GROUP
EXPERT_FF = D_MODEL // 8
TOP_K = 2

EVEN_SPLITS = (3 * HA * DK, HA * DK, 2 * HA, 2 * HA, HB * HEAD_DIM, HKV_B * HEAD_DIM, HKV_B * HEAD_DIM)
D_IN_EVEN = sum(EVEN_SPLITS)
D_MIX_EVEN = HA * DK + HB * HEAD_DIM
ODD_SPLITS = (HC * HEAD_DIM, HKV_C * HEAD_DIM, HKV_C * HEAD_DIM, D_S5)
D_IN_ODD = sum(ODD_SPLITS)
D_MIX_ODD = HC * HEAD_DIM + D_S5

F32 = jnp.float32
BF16 = jnp.bfloat16


def split_cols(x, sizes):
    idx = np.cumsum(sizes)[:-1].tolist()
    return jnp.split(x, idx, axis=-1)


def axial_rope_tables(rows):
    row = jnp.repeat(jnp.arange(rows), GRID_W).astype(F32)
    col = jnp.tile(jnp.arange(GRID_W), rows).astype(F32)
    quarter = HEAD_DIM // 4
    freqs = ROPE_THETA ** (-jnp.arange(quarter, dtype=F32) / quarter)
    ang_r = row[:, None] * freqs
    ang_c = col[:, None] * freqs
    ang = jnp.concatenate([ang_r, ang_r, ang_c, ang_c], axis=-1)
    return jnp.cos(ang), jnp.sin(ang)


SUBLANES = 8
VMEM_LIMIT = 56 * 1024 * 1024
S5_ROWS = 1024
S5_STATE = G_D * S5_P


def _s5_scan_kernel(u_ref, bmat_ref, cmat_ref, a_ref, s0_ref, y_ref, sfin_ref, xs_ref, st_ref, *, nblk, bsz):
    dr = pl.program_id(0)
    blk = pl.program_id(1)

    @pl.when(blk == 0)
    def _():
        st_ref[...] = s0_ref[0]

    xs_ref[...] = jnp.dot(u_ref[...].astype(BF16), bmat_ref[0], preferred_element_type=F32)
    a_re = jnp.broadcast_to(a_ref[0, 0:1, :], (SUBLANES, S5_STATE))
    a_im = jnp.broadcast_to(a_ref[0, 1:2, :], (SUBLANES, S5_STATE))
    steps = S5_ROWS // bsz
    for sg in range(bsz // SUBLANES):
        rows = pl.ds(sg * SUBLANES, SUBLANES)

        def body(i, carry, sg=sg):
            x_re, x_im = carry
            l = i + dr * (steps - 1 - 2 * i)
            r = pl.ds(pl.multiple_of(l * bsz + sg * SUBLANES, SUBLANES), SUBLANES)
            n_re = a_re * x_re - a_im * x_im + xs_ref[r, 0:S5_STATE]
            n_im = a_re * x_im + a_im * x_re + xs_ref[r, S5_STATE:2 * S5_STATE]
            xs_ref[r, 0:S5_STATE] = n_re
            xs_ref[r, S5_STATE:2 * S5_STATE] = n_im
            return n_re, n_im

        x_re, x_im = lax.fori_loop(0, steps, body, (st_ref[rows, 0:S5_STATE], st_ref[rows, S5_STATE:2 * S5_STATE]),
                                   unroll=4)
        st_ref[rows, 0:S5_STATE] = x_re
        st_ref[rows, S5_STATE:2 * S5_STATE] = x_im
    y_ref[0] = jnp.dot(xs_ref[...].astype(BF16), cmat_ref[0], preferred_element_type=F32)

    @pl.when(blk == nblk - 1)
    def _():
        sfin_ref[0] = st_ref[...]


def _s5_out_kernel(yf_ref, yb_ref, u_ref, d_ref, w_ref, b_ref, o_ref):
    y = yf_ref[0] + yb_ref[0] + u_ref[...] * d_ref[...]
    y = jax.nn.gelu(y)
    gate = jnp.dot(y.astype(BF16), w_ref[...], preferred_element_type=F32) + b_ref[...]
    o_ref[...] = y * jax.nn.sigmoid(gate)


def s5_operators(op):
    lam_re, lam_im = op['lam_re'].astype(F32), op['lam_im'].astype(F32)
    dt = jnp.exp(op['log_dt'].astype(F32))[..., None]
    mag = jnp.exp(lam_re * dt)
    a_re, a_im = mag * jnp.cos(lam_im * dt), mag * jnp.sin(lam_im * dt)
    den = lam_re * lam_re + lam_im * lam_im
    k_re = ((a_re - 1) * lam_re + a_im * lam_im) / den
    k_im = (a_im * lam_re - (a_re - 1) * lam_im) / den
    b_re, b_im = op['b_re'].astype(F32), op['b_im'].astype(F32)
    bb_re = k_re[..., None] * b_re - k_im[..., None] * b_im
    bb_im = k_re[..., None] * b_im + k_im[..., None] * b_re
    eye = jnp.eye(G_D, dtype=F32)

    def b_blocks(t):
        return jnp.einsum('dgpc,gh->dgchp', t, eye).reshape(2, D_S5, S5_STATE)

    def c_blocks(t):
        return jnp.einsum('dgcp,gh->dgphc', t, eye).reshape(2, S5_STATE, D_S5)

    b_blk = jnp.concatenate([b_blocks(bb_re), b_blocks(bb_im)], axis=-1).astype(BF16)
    c_blk = jnp.concatenate([c_blocks(op['c_re'].astype(F32)), -c_blocks(op['c_im'].astype(F32))],
                            axis=1).astype(BF16)
    a_vec = jnp.stack([a_re.reshape(2, S5_STATE), a_im.reshape(2, S5_STATE)], axis=1)
    return b_blk, c_blk, a_vec


def s5_pallas(u, op, ops, s0):
    b, l, _ = u.shape
    b_blk, c_blk, a_vec = ops
    n = b * l
    nblk = n // S5_ROWS
    u_t = jnp.swapaxes(u, 0, 1).reshape(n, D_S5)

    def rows_map(dr, blk):
        return (blk + dr * (nblk - 1 - 2 * blk), 0)

    y2, s_fin = pl.pallas_call(
        functools.partial(_s5_scan_kernel, nblk=nblk, bsz=b),
        grid=(2, nblk),
        in_specs=[
            pl.BlockSpec((S5_ROWS, D_S5), rows_map),
            pl.BlockSpec((1, D_S5, 2 * S5_STATE), lambda dr, blk: (dr, 0, 0)),
            pl.BlockSpec((1, 2 * S5_STATE, D_S5), lambda dr, blk: (dr, 0, 0)),
            pl.BlockSpec((1, 2, S5_STATE), lambda dr, blk: (dr, 0, 0)),
            pl.BlockSpec((1, b, 2 * S5_STATE), lambda dr, blk: (dr, 0, 0)),
        ],
        out_specs=[
            pl.BlockSpec((1, S5_ROWS, D_S5), lambda dr, blk: (dr,) + rows_map(dr, blk)),
            pl.BlockSpec((1, b, 2 * S5_STATE), lambda dr, blk: (dr, 0, 0)),
        ],
        out_shape=[jax.ShapeDtypeStruct((2, n, D_S5), F32), jax.ShapeDtypeStruct((2, b, 2 * S5_STATE), F32)],
        scratch_shapes=[pltpu.VMEM((S5_ROWS, 2 * S5_STATE), F32), pltpu.VMEM((b, 2 * S5_STATE), F32)],
        compiler_params=pltpu.CompilerParams(dimension_semantics=("arbitrary", "arbitrary"),
                                             vmem_limit_bytes=VMEM_LIMIT),
        name="s5_scan",
    )(u_t, b_blk, c_blk, a_vec, s0)
    tm = S5_ROWS
    y = pl.pallas_call(
        _s5_out_kernel,
        grid=(n // tm,),
        in_specs=[
            pl.BlockSpec((1, tm, D_S5), lambda i: (0, i, 0)),
            pl.BlockSpec((1, tm, D_S5), lambda i: (1, i, 0)),
            pl.BlockSpec((tm, D_S5), lambda i: (i, 0)),
            pl.BlockSpec((1, D_S5), lambda i: (0, 0)),
            pl.BlockSpec((D_S5, D_S5), lambda i: (0, 0)),
            pl.BlockSpec((1, D_S5), lambda i: (0, 0)),
        ],
        out_specs=pl.BlockSpec((tm, D_S5), lambda i: (i, 0)),
        out_shape=jax.ShapeDtypeStruct((n, D_S5), F32),
        name="s5_out",
    )(y2, y2, u_t, op['d'].astype(F32).reshape(1, D_S5), op['glu_w'].astype(BF16),
      op['glu_b'].astype(F32).reshape(1, D_S5))
    return jnp.swapaxes(y.reshape(l, b, D_S5), 0, 1), s_fin


LANES = 128
DN_PAIRS = HA * DK // LANES
NEG_BIG = -1e30
DN_UNROLL = 4


def _nt_dot(a, b):
    return lax.dot_general(a, b, (((1,), (1,)), ((), ())), preferred_element_type=F32)


def _bf16_dot(a, b):
    return jnp.dot(a.astype(BF16), b.astype(BF16), preferred_element_type=F32)


def _select_dot(x, sel):
    x1 = x.astype(BF16)
    r1 = x - x1.astype(F32)
    x2 = r1.astype(BF16)
    x3 = (r1 - x2.astype(F32)).astype(BF16)
    dot = functools.partial(jnp.dot, preferred_element_type=F32)
    return dot(x1, sel) + dot(x2, sel) + dot(x3, sel)


def _head_sums(x):
    lane = lax.broadcasted_iota(jnp.int32, x.shape, 1)
    lo = lane < HEAD_DIM
    s0 = jnp.sum(jnp.where(lo, x, 0.0), axis=-1, keepdims=True)
    s1 = jnp.sum(jnp.where(lo, 0.0, x), axis=-1, keepdims=True)
    return jnp.where(lo, s0, s1)


def _chunk_cumsum(x, reverse):
    n = x.shape[0]
    pos = lax.broadcasted_iota(jnp.int32, x.shape, 0) % DN_CHUNK
    s = 1
    while s < DN_CHUNK:
        if reverse:
            x = x + jnp.where(pos < DN_CHUNK - s, pltpu.roll(x, n - s, 0), 0.0)
        else:
            x = x + jnp.where(pos >= s, pltpu.roll(x, s, 0), 0.0)
        s *= 2
    return x


def _short_conv_silu(x, w):
    n = x.shape[0]
    row = lax.broadcasted_iota(jnp.int32, x.shape, 0)
    pad = CONV_W // 2
    y = x * w[pad:pad + 1, :]
    for t in range(CONV_W):
        s = t - pad
        if s == 0:
            continue
        shifted = pltpu.roll(x, (-s) % n, 0)
        valid = (row + s >= 0) & (row + s < n)
        y = y + jnp.where(valid, shifted, 0.0) * w[t:t + 1, :]
    return jax.nn.silu(y)


def _unit_tri_solve(ms, rhss, ri, ci):
    n = rhss[0].shape[-1]
    same16 = (ri // 16) == (ci // 16)
    dot = functools.partial(jnp.dot, preferred_element_type=F32)

    def apply(p, r):
        hi = r.astype(BF16)
        lo = (r - hi.astype(F32)).astype(BF16)
        x = dot(p, jnp.concatenate([hi, lo], axis=-1))
        return x[:, 0:n] + x[:, n:2 * n]

    diag = [jnp.where(same16, m, 0.0) for m in ms]
    rest = [jnp.where(same16, 0.0, m).astype(BF16) for m in ms]
    pows = [d.astype(BF16) for d in diag]
    tinv = [jnp.where(ri == ci, 1.0, 0.0) - d for d in diag]
    for _ in range(3):
        pows = [dot(p, p).astype(BF16) for p in pows]
        tinv = [t + dot(t.astype(BF16), p) for t, p in zip(tinv, pows)]
    tinv = [t.astype(BF16) for t in tinv]
    p1 = [dot(t, e).astype(BF16) for t, e in zip(tinv, rest)]
    sols = [apply(t, r) for t, r in zip(tinv, rhss)]
    p2 = [dot(p, p).astype(BF16) for p in p1]
    sols = [s - apply(p, s) for p, s in zip(p1, sols)]
    return [s + apply(p, s) for p, s in zip(p2, sols)]


def _deltanet_kernel(q_ref, k_ref, v_ref, z_ref, a_ref, bb_ref, cwq_ref, cwk_ref, cwv_ref, alog_ref, dtb_ref, nw_ref,
                     s0_ref, y_ref, sfin_ref,
                     qs_ref, ks_ref, vs_ref, o_ref, gc_ref, beta_ref, u_ref, w_ref, attn_ref, qp_ref, kpt_ref,
                     etot_ref, *, seq):
    hp = pl.program_id(1)
    nchunk = seq // DN_CHUNK
    c64 = DN_CHUNK
    head_lanes = [slice(j * HEAD_DIM, (j + 1) * HEAD_DIM) for j in range(2)]

    q = _short_conv_silu(q_ref[0], cwq_ref[...])
    q = q * lax.rsqrt(_head_sums(q * q) + EPS) * (DK ** -0.5)
    k = _short_conv_silu(k_ref[0], cwk_ref[...])
    k = k * lax.rsqrt(_head_sums(k * k) + EPS)
    v = _short_conv_silu(v_ref[0], cwv_ref[...])
    qs_ref[...] = q
    ks_ref[...] = k
    vs_ref[...] = v
    o_ref[...] = jnp.zeros_like(o_ref)

    g_col = -jnp.exp(alog_ref[...]) * jax.nn.softplus(a_ref[0] + dtb_ref[...])
    b_col = jax.nn.sigmoid(bb_ref[0])
    sel_row = lax.broadcasted_iota(jnp.int32, (2 * HA, 2 * LANES), 0)
    sel_lane = lax.broadcasted_iota(jnp.int32, (2 * HA, 2 * LANES), 1)
    sel = (sel_row == (sel_lane // LANES) * HA + 2 * hp + (sel_lane // HEAD_DIM) % 2).astype(BF16)
    g_b = _select_dot(g_col, sel)
    gc_ref[:, 0:LANES] = _chunk_cumsum(g_b[:, 0:LANES], reverse=False)
    gc_ref[:, LANES:2 * LANES] = _chunk_cumsum(g_b[:, LANES:2 * LANES], reverse=True)
    beta_ref[...] = _select_dot(b_col, sel)

    ri = lax.broadcasted_iota(jnp.int32, (c64, c64), 0)
    ci = lax.broadcasted_iota(jnp.int32, (c64, c64), 1)

    def intra_chunk(step, carry):
        chunk_ids = [step * DN_UNROLL + t for t in range(DN_UNROLL)]
        rows_c = [pl.ds(pl.multiple_of(c * c64, c64), c64) for c in chunk_ids]
        loads = [(qs_ref[r, :], ks_ref[r, :], vs_ref[r, :], gc_ref[r, :], beta_ref[r, :]) for r in rows_c]
        heads = [(t, j, ld[0][:, head_lanes[j]], ld[1][:, head_lanes[j]], ld[2][:, head_lanes[j]])
                 for t, ld in enumerate(loads) for j in range(2)]
        gate_lanes = [slice((c % 2) * LANES + (c // 2) * HEAD_DIM, (c % 2) * LANES + (c // 2 + 1) * HEAD_DIM)
                      for c in range(4)]
        gates = {(t, c): (ld[3][:, gate_lanes[c]], ld[4][:, gate_lanes[c]])
                 for t, ld in enumerate(loads) for c in range(4)}
        kbs = [h[3].astype(BF16) for h in heads]
        kks = [_nt_dot(kb, kb) for kb in kbs]
        qks = [_nt_dot(h[2].astype(BF16), kb) for h, kb in zip(heads, kbs)]
        chains, ms, rhss = [], [], []
        for (t, j, qj, kj, vj), kk, qk in zip(heads, kks, qks):
            for d in range(2):
                gcb, bet = gates[(t, 2 * j + d)]
                diff = gcb - gcb.T
                earlier = (ri >= ci) if d == 0 else (ri <= ci)
                strict = (ri > ci) if d == 0 else (ri < ci)
                dec = jnp.exp(jnp.where(earlier, diff, NEG_BIG))
                eg = jnp.exp(gcb)
                gtot = gcb[c64 - 1:c64, :] if d == 0 else gcb[0:1, :]
                ms.append(jnp.where(strict, kk * bet * dec, 0.0))
                rhss.append(jnp.concatenate([vj * bet, kj * bet * eg], axis=-1))
                chains.append((t, 2 * j + d, qk * dec, qj * eg, (kj * jnp.exp(gtot - gcb)).T,
                               jnp.broadcast_to(jnp.exp(gtot), (SUBLANES, HEAD_DIM))))
        uws = _unit_tri_solve(ms, rhss, ri, ci)
        for (t, c, attn, qp, kpt, etot), uw in zip(chains, uws):
            u_ref[c, rows_c[t], :] = uw[:, 0:HEAD_DIM]
            w_ref[c, rows_c[t], :] = uw[:, HEAD_DIM:2 * HEAD_DIM].astype(BF16)
            attn_ref[c, rows_c[t], :] = attn.astype(BF16)
            qp_ref[c, rows_c[t], :] = qp.astype(BF16)
            kpt_ref[c, rows_c[t], :] = kpt.astype(BF16)
            etot_ref[c, pl.ds(pl.multiple_of(chunk_ids[t] * SUBLANES, SUBLANES), SUBLANES), :] = etot
        return carry

    lax.fori_loop(0, nchunk // DN_UNROLL, intra_chunk, 0)

    def inter_chunk(i, states):
        rows_d = [pl.ds(pl.multiple_of(cc * c64, c64), c64) for cc in (i, nchunk - 1 - i)]
        erow_d = [pl.ds(pl.multiple_of(cc * SUBLANES, SUBLANES), SUBLANES) for cc in (i, nchunk - 1 - i)]
        loaded = [(w_ref[c, rows_d[c % 2], :], qp_ref[c, rows_d[c % 2], :], u_ref[c, rows_d[c % 2], :],
                   attn_ref[c, rows_d[c % 2], :], kpt_ref[c, rows_d[c % 2], :], etot_ref[c, erow_d[c % 2], :],
                   o_ref[c // 2, rows_d[c % 2], :]) for c in range(4)]
        ws_qs = [jnp.dot(jnp.concatenate([ld[0], ld[1]], axis=0), s.astype(BF16), preferred_element_type=F32)
                 for ld, s in zip(loaded, states)]
        v_new = [(ld[2] - r[0:c64]).astype(BF16) for ld, r in zip(loaded, ws_qs)]
        av = [jnp.dot(ld[3], vn, preferred_element_type=F32) for ld, vn in zip(loaded, v_new)]
        kv = [jnp.dot(ld[4], vn, preferred_element_type=F32) for ld, vn in zip(loaded, v_new)]
        for c in range(4):
            o_ref[c // 2, rows_d[c % 2], :] = loaded[c][6] + ws_qs[c][c64:2 * c64] + av[c]
        return tuple(s * ld[5][0:1, :] + x for s, ld, x in zip(states, loaded, kv))

    init = tuple(s0_ref[0, c % 2, c // 2] for c in range(4))
    fin = lax.fori_loop(0, nchunk, inter_chunk, init)
    for c in range(4):
        sfin_ref[0, c % 2, c // 2] = fin[c]

    o = jnp.concatenate([o_ref[0], o_ref[1]], axis=-1)
    o = o * lax.rsqrt(_head_sums(o * o) * (1.0 / DK) + EPS) * nw_ref[...]
    y_ref[0] = o * jax.nn.silu(z_ref[0])


def deltanet_pallas(qkv, z, a, bb, ep, s0):
    b, l, _ = qkv.shape
    npair = DN_PAIRS

    def col_spec(off):
        return pl.BlockSpec((1, l, LANES), lambda i, hp: (i, 0, off + hp))

    def cw_spec(off):
        return pl.BlockSpec((CONV_W, LANES), lambda i, hp: (0, off + hp))

    def full(shape):
        return pl.BlockSpec(shape, lambda i, hp: (0,) * len(shape))

    state_spec = pl.BlockSpec((1, 2, 2, DK, DK), lambda i, hp: (i, 0, hp, 0, 0))
    row_f32 = pltpu.VMEM((l, LANES), F32)
    gate_f32 = pltpu.VMEM((l, 2 * LANES), F32)
    chain_f32 = pltpu.VMEM((4, l, HEAD_DIM), F32)
    chain_bf16 = pltpu.VMEM((4, l, HEAD_DIM), BF16)
    y, s_fin = pl.pallas_call(
        functools.partial(_deltanet_kernel, seq=l),
        grid=(b, npair),
        in_specs=[col_spec(0), col_spec(npair), col_spec(2 * npair), col_spec(0),
                  pl.BlockSpec((1, l, 2 * HA), lambda i, hp: (i, 0, 0)),
                  pl.BlockSpec((1, l, 2 * HA), lambda i, hp: (i, 0, 0)),
                  cw_spec(0), cw_spec(npair), cw_spec(2 * npair),
                  full((1, 2 * HA)), full((1, 2 * HA)), full((1, LANES)), state_spec],
        out_specs=[col_spec(0), state_spec],
        out_shape=[jax.ShapeDtypeStruct((b, l, HA * DK), F32), jax.ShapeDtypeStruct((b, 2, HA, DK, DK), F32)],
        scratch_shapes=[row_f32, row_f32, row_f32, pltpu.VMEM((2, l, HEAD_DIM), F32), gate_f32, gate_f32,
                        chain_f32, chain_bf16, chain_bf16, chain_bf16, chain_bf16,
                        pltpu.VMEM((4, l // DN_CHUNK * SUBLANES, HEAD_DIM), F32)],
        compiler_params=pltpu.CompilerParams(dimension_semantics=("arbitrary", "arbitrary"),
                                             vmem_limit_bytes=VMEM_LIMIT),
        name="deltanet",
    )(qkv, qkv, qkv, z, a, bb, ep['conv_w'], ep['conv_w'], ep['conv_w'],
      ep['a_log'].astype(F32).reshape(1, 2 * HA), ep['dt_bias'].astype(F32).reshape(1, 2 * HA),
      jnp.tile(ep['norm_w'].astype(F32), 2).reshape(1, LANES), s0)
    return y, s_fin


TM = 512
ATT_TQ = BLK
ROT = HEAD_DIM // 4


def rope_tables():
    cos, sin = axial_rope_tables(DEC_SEQ // GRID_W)
    cos2 = jnp.tile(cos, (1, LANES // HEAD_DIM))
    sin2 = jnp.tile(sin, (1, LANES // HEAD_DIM))
    first = (jnp.arange(LANES) % (2 * ROT)) < ROT
    return cos2, jnp.where(first, -sin2, 0.0), jnp.where(first, 0.0, sin2)


def _prep_kernel(q_ref, k_ref, v_ref, *refs, norm, rope, k_f32):
    refs = list(refs)
    qw_ref, kw_ref = (refs.pop(0), refs.pop(0)) if norm else (None, None)
    cos_ref, sup_ref, sdn_ref = (refs.pop(0), refs.pop(0), refs.pop(0)) if rope else (None, None, None)
    qo_ref, ko_ref, vo_ref = refs[0:3]

    def process(x_ref, w_ref, scale):
        outs, normed = [], []
        for c in range(x_ref.shape[1] // LANES):
            xc = x_ref[:, c * LANES:(c + 1) * LANES]
            if norm:
                xc = xc * lax.rsqrt(_head_sums(xc * xc) * (1.0 / HEAD_DIM) + EPS) * w_ref[...]
                normed.append(xc)
            if rope:
                xc = (xc * cos_ref[...] + pltpu.roll(xc, LANES - ROT, 1) * sup_ref[...]
                      + pltpu.roll(xc, ROT, 1) * sdn_ref[...])
            outs.append((xc * scale).astype(BF16))
        cat = lambda parts: jnp.concatenate(parts, axis=-1) if len(parts) > 1 else parts[0]
        return cat(outs), (cat(normed) if norm else None)

    qo_ref[...] = process(q_ref, qw_ref, HEAD_DIM ** -0.5)[0]
    ko, kn = process(k_ref, kw_ref, 1.0)
    ko_ref[...] = ko
    if k_f32:
        refs[3][...] = kn
    vo_ref[...] = v_ref[...].astype(BF16)


def prep_pallas(q, k, v, norm_ws, rope_tabs, k_f32):
    t = q.shape[0]
    row = lambda i: (i, 0)
    ins = [q, k, v]
    in_specs = [pl.BlockSpec((TM, a.shape[1]), row) for a in ins]
    if norm_ws is not None:
        ins += [jnp.tile(w.astype(F32), LANES // HEAD_DIM).reshape(1, LANES) for w in norm_ws]
        in_specs += [pl.BlockSpec((1, LANES), lambda i: (0, 0))] * 2
    if rope_tabs is not None:
        tiles = DEC_SEQ // TM
        ins += list(rope_tabs)
        in_specs += [pl.BlockSpec((TM, LANES), lambda i: (i % tiles, 0))] * 3
    outs = [(a.shape, BF16) for a in (q, k, v)] + ([(k.shape, F32)] if k_f32 else [])
    return pl.pallas_call(
        functools.partial(_prep_kernel, norm=norm_ws is not None, rope=rope_tabs is not None, k_f32=k_f32),
        grid=(t // TM,),
        in_specs=in_specs,
        out_specs=[pl.BlockSpec((TM, s[1]), row) for s, _ in outs],
        out_shape=[jax.ShapeDtypeStruct(s, dt) for s, dt in outs],
        name="qkv_prep",
    )(*ins)


def _attn_kernel(q_ref, k_ref, v_ref, *refs, grp, hk, windowed, has_ctx, has_sink):
    refs = list(refs)
    kc_ref, vc_ref = (refs.pop(0), refs.pop(0)) if has_ctx else (None, None)
    sink_ref = refs.pop(0) if has_sink else None
    o_ref = refs[0]
    tq = q_ref.shape[1]
    seq_k = k_ref.shape[1]
    if windowed:
        qi = pl.program_id(1)
        start = jnp.clip((qi - 1) * tq, 0, seq_k - 3 * tq)
        krows = pl.ds(pl.multiple_of(start, tq), 3 * tq)
        qpos = qi * tq + lax.broadcasted_iota(jnp.int32, (grp * tq, 3 * tq), 0) % tq
        kpos = start + lax.broadcasted_iota(jnp.int32, (grp * tq, 3 * tq), 1)
        keep = jnp.abs(qpos - kpos) <= WINDOW
    else:
        krows = pl.ds(0, seq_k)
    q = q_ref[0]
    groups = range(hk)
    lanes = [slice(g * HEAD_DIM, (g + 1) * HEAD_DIM) for g in groups]
    qgs = [jnp.concatenate([q[:, (g * grp + j) * HEAD_DIM:(g * grp + j + 1) * HEAD_DIM] for j in range(grp)], axis=0)
           for g in groups]
    ss = [_nt_dot(qg, k_ref[0, krows, gl]) for qg, gl in zip(qgs, lanes)]
    if windowed:
        ss = [jnp.where(keep, s, NEG_BIG) for s in ss]
    ms = [jnp.max(s, axis=-1, keepdims=True) for s in ss]
    if has_ctx:
        kc = kc_ref[0].astype(BF16)
        scs = [_nt_dot(qg, kc[:, gl]) for qg, gl in zip(qgs, lanes)]
        ms = [jnp.maximum(m, jnp.max(sc, axis=-1, keepdims=True)) for m, sc in zip(ms, scs)]
    if has_sink:
        sks = [jnp.concatenate([jnp.broadcast_to(sink_ref[0:1, g * grp + j:g * grp + j + 1], (tq, 1))
                                for j in range(grp)], axis=0) for g in groups]
        ms = [jnp.maximum(m, sk) for m, sk in zip(ms, sks)]
    ps = [jnp.exp(s - m) for s, m in zip(ss, ms)]
    dens = [jnp.sum(p, axis=-1, keepdims=True) for p in ps]
    os_ = [jnp.dot(p.astype(BF16), v_ref[0, krows, gl], preferred_element_type=F32) for p, gl in zip(ps, lanes)]
    if has_ctx:
        vc = vc_ref[0].astype(BF16)
        pcs = [jnp.exp(sc - m) for sc, m in zip(scs, ms)]
        dens = [den + jnp.sum(pc, axis=-1, keepdims=True) for den, pc in zip(dens, pcs)]
        os_ = [o + jnp.dot(pc.astype(BF16), vc[:, gl], preferred_element_type=F32)
               for o, pc, gl in zip(os_, pcs, lanes)]
    if has_sink:
        dens = [den + jnp.exp(sk - m) for den, sk, m in zip(dens, sks, ms)]
    outs = [None] * (grp * hk)
    for g in groups:
        o = os_[g] / dens[g]
        for j in range(grp):
            outs[g * grp + j] = o[j * tq:(j + 1) * tq]
    o_ref[0] = jnp.concatenate(outs, axis=-1)


def attention_pallas(q, k, v, ctx=None, sink=None, windowed=False):
    b, lq, hd = q.shape
    lk, kd = k.shape[1], k.shape[2]
    hk = kd // HEAD_DIM
    grp = hd // kd
    tq = ATT_TQ
    ins = [q, k, v]
    in_specs = [pl.BlockSpec((1, tq, hd), lambda i, t: (i, t, 0)),
                pl.BlockSpec((1, lk, kd), lambda i, t: (i, 0, 0)),
                pl.BlockSpec((1, lk, kd), lambda i, t: (i, 0, 0))]
    if ctx is not None:
        ins += list(ctx)
        in_specs += [pl.BlockSpec((1, ctx[0].shape[1], kd), lambda i, t: (i, 0, 0))] * 2
    if sink is not None:
        ins.append(sink.astype(F32).reshape(1, -1))
        in_specs.append(pl.BlockSpec((1, sink.shape[0]), lambda i, t: (0, 0)))
    return pl.pallas_call(
        functools.partial(_attn_kernel, grp=grp, hk=hk, windowed=windowed, has_ctx=ctx is not None,
                          has_sink=sink is not None),
        grid=(b, lq // tq),
        in_specs=in_specs,
        out_specs=pl.BlockSpec((1, tq, hd), lambda i, t: (i, t, 0)),
        out_shape=jax.ShapeDtypeStruct((b, lq, hd), F32),
        compiler_params=pltpu.CompilerParams(dimension_semantics=("arbitrary", "arbitrary"),
                                             vmem_limit_bytes=VMEM_LIMIT),
        name="attention",
    )(*ins)


def even_mixer_context(proj, ep):
    qkv, z, ab, bq, bk, bv = proj
    qkv, z, ab = [t.reshape(BATCH, SEQ, -1) for t in (qkv, z, ab)]
    a, bb = ab[..., 0:2 * HA], ab[..., 2 * HA:4 * HA]
    y_a, dn_fin = deltanet_pallas(qkv, z, a, bb, ep, jnp.zeros((BATCH, 2, HA, DK, DK), F32))
    qh, kh, vh = [t.reshape(BATCH, SEQ, -1) for t in prep_pallas(bq, bk, bv, None, None, False)]
    y_b = attention_pallas(qh, kh, vh, sink=ep['sink'])
    return (y_a.reshape(-1, HA * DK), y_b.reshape(-1, HB * HEAD_DIM), dn_fin,
            bk.reshape(BATCH, SEQ, HKV_B, HEAD_DIM), bv.reshape(BATCH, SEQ, HKV_B, HEAD_DIM))


def even_mixer_latent(proj, ep, dn_state, ctx_k, ctx_v, rope_tabs):
    qkv, z, ab, bq, bk, bv = proj
    qkv, z, ab = [t.reshape(DEC_BATCH, DEC_SEQ, -1) for t in (qkv, z, ab)]
    a, bb = ab[..., 0:2 * HA], ab[..., 2 * HA:4 * HA]
    y_a, _ = deltanet_pallas(qkv, z, a, bb, ep, dn_state.astype(F32))
    qh, kh, vh = [t.reshape(DEC_BATCH, DEC_SEQ, -1) for t in prep_pallas(bq, bk, bv, None, rope_tabs, False)]
    ctx = (ctx_k.reshape(DEC_BATCH, PAST_LEN, -1), ctx_v.reshape(DEC_BATCH, PAST_LEN, -1))
    y_b = attention_pallas(qh, kh, vh, ctx=ctx, sink=ep['sink'], windowed=True)
    return y_a.reshape(-1, HA * DK), y_b.reshape(-1, HB * HEAD_DIM)


def odd_mixer_context(proj, op):
    cq, ck, cv, u = proj
    qh, kh, vh, k_normed = prep_pallas(cq, ck, cv, (op['q_norm'], op['k_norm']), None, True)
    y_c = attention_pallas(*[t.reshape(BATCH, SEQ, -1) for t in (qh, kh, vh)])
    y_d, s_fin = s5_pallas(u.reshape(BATCH, SEQ, -1), op, s5_operators(op), jnp.zeros((2, BATCH, 2 * S5_STATE), F32))
    s_fin = jnp.transpose(s_fin.reshape(2, BATCH, 2, G_D, S5_P), (1, 0, 3, 4, 2))
    return (y_c.reshape(-1, HC * HEAD_DIM), y_d.reshape(-1, D_S5), k_normed.reshape(BATCH, SEQ, HKV_C, HEAD_DIM),
            cv.reshape(BATCH, SEQ, HKV_C, HEAD_DIM), s_fin)


def odd_mixer_latent(proj, op, ctx_k, ctx_v, s5_state, rope_tabs):
    cq, ck, cv, u = proj
    qh, kh, vh = prep_pallas(cq, ck, cv, (op['q_norm'], op['k_norm']), rope_tabs, False)
    ctx = (ctx_k.reshape(DEC_BATCH, PAST_LEN, -1), ctx_v.reshape(DEC_BATCH, PAST_LEN, -1))
    y_c = attention_pallas(*[t.reshape(DEC_BATCH, DEC_SEQ, -1) for t in (qh, kh, vh)], ctx=ctx)
    s0 = jnp.transpose(s5_state.astype(F32), (1, 0, 4, 2, 3)).reshape(2, DEC_BATCH, 2 * S5_STATE)
    y_d, _ = s5_pallas(u.reshape(DEC_BATCH, DEC_SEQ, -1), op, s5_operators(op), s0)
    return y_c.reshape(-1, HC * HEAD_DIM), y_d.reshape(-1, D_S5)


ADA_ROWS = 16
ADA_TN = 1024
MOD_ROWS = 8
ROUTER_LANES = LANES
MOE_FCHUNK = EXP_PER_GROUP * EXPERT_FF


def _ada_kernel(c_ref, w_ref, b_ref, o_ref):
    o_ref[0] = _bf16_dot(jax.nn.silu(c_ref[...]), w_ref[0]) + b_ref[0]


def ada_pallas(cvec, ada_w, ada_b):
    n = 6 * D_MODEL
    return pl.pallas_call(
        _ada_kernel,
        grid=(DEPTH, n // ADA_TN),
        in_specs=[pl.BlockSpec((ADA_ROWS, D_MODEL), lambda l, j: (0, 0)),
                  pl.BlockSpec((1, D_MODEL, ADA_TN), lambda l, j: (l, 0, j)),
                  pl.BlockSpec((1, 1, ADA_TN), lambda l, j: (l, 0, j))],
        out_specs=pl.BlockSpec((1, ADA_ROWS, ADA_TN), lambda l, j: (l, 0, j)),
        out_shape=jax.ShapeDtypeStruct((DEPTH, ADA_ROWS, n), F32),
        name="ada_modulation",
    )(cvec, ada_w, ada_b.reshape(DEPTH, 1, n))


def _modulated(x, nw, shift, scale):
    return x * lax.rsqrt(jnp.mean(x * x, axis=-1, keepdims=True) + EPS) * nw * (1 + scale) + shift


def _inproj_kernel(x_ref, mod_ref, nw_ref, *refs):
    nout = len(refs) // 2
    m = mod_ref[0]
    h = _modulated(x_ref[...], nw_ref[...], m[0:1], m[1:2]).astype(BF16)
    for w_ref, o_ref in zip(refs[:nout], refs[nout:]):
        o_ref[...] = jnp.dot(h, w_ref[...], preferred_element_type=F32)


def inproj_pallas(x, mod, nw, weights, rows_per_mod):
    t = x.shape[0]
    tiles_per_mod = rows_per_mod // TM
    return pl.pallas_call(
        _inproj_kernel,
        grid=(t // TM,),
        in_specs=[pl.BlockSpec((TM, D_MODEL), lambda i: (i, 0)),
                  pl.BlockSpec((1, MOD_ROWS, D_MODEL), lambda i: (i // tiles_per_mod, 0, 0)),
                  pl.BlockSpec((1, D_MODEL), lambda i: (0, 0))]
        + [pl.BlockSpec(w.shape, lambda i: (0, 0)) for w in weights],
        out_specs=[pl.BlockSpec((TM, w.shape[1]), lambda i: (i, 0)) for w in weights],
        out_shape=[jax.ShapeDtypeStruct((t, w.shape[1]), F32) for w in weights],
        compiler_params=pltpu.CompilerParams(dimension_semantics=("arbitrary",), vmem_limit_bytes=VMEM_LIMIT),
        name="modulated_in_proj",
    )(x, mod, nw.reshape(1, D_MODEL), *weights)


def _moe_kernel(x_ref, ya_ref, yb_ref, mod_ref, nw_ref, woa_ref, wob_ref, wr_ref, br_ref, wg_ref, wu_ref, wd_ref,
                fw_ref, o_ref, x1_ref, h_ref, comb_ref, acc_ref, *, final):
    f = pl.program_id(1)
    m = mod_ref[0]

    @pl.when(f == 0)
    def _():
        y = _bf16_dot(ya_ref[...], woa_ref[...]) + _bf16_dot(yb_ref[...], wob_ref[...])
        x1 = x_ref[...] + m[2:3] * y
        x1_ref[...] = x1
        h = _modulated(x1, nw_ref[...], m[3:4], m[4:5])
        h_hi = h.astype(BF16)
        h_ref[...] = h_hi
        h_lo = (h - h_hi.astype(F32)).astype(BF16)
        logits = (jnp.dot(jnp.concatenate([h_hi, h_lo], axis=-1), wr_ref[0:2 * D_MODEL, :], preferred_element_type=F32)
                  + jnp.dot(h_hi, wr_ref[2 * D_MODEL:3 * D_MODEL, :], preferred_element_type=F32) + br_ref[...])
        lane = lax.broadcasted_iota(jnp.int32, logits.shape, 1)
        lane_f = lane.astype(F32)
        is_group = lane < N_GROUPS
        lg = jnp.where(is_group, logits, -jnp.inf)
        g_max = jnp.max(lg, axis=-1, keepdims=True)
        g_idx = jnp.min(jnp.where(lg == g_max, lane_f, float(ROUTER_LANES)), axis=-1, keepdims=True)
        gate_g = 1.0 / jnp.sum(jnp.where(is_group, jnp.exp(logits - g_max), 0.0), axis=-1, keepdims=True)
        lane_group = ((lane + (EXP_PER_GROUP - N_GROUPS)) // EXP_PER_GROUP - 1).astype(F32)
        in_group = (lane >= N_GROUPS) & (lane < N_GROUPS + N_EXP) & (lane_group == g_idx)
        le = jnp.where(in_group, logits, -jnp.inf)
        v1 = jnp.max(le, axis=-1, keepdims=True)
        i1 = jnp.min(jnp.where(le == v1, lane_f, float(ROUTER_LANES)), axis=-1, keepdims=True)
        le2 = jnp.where(lane_f == i1, -jnp.inf, le)
        v2 = jnp.max(le2, axis=-1, keepdims=True)
        i2 = jnp.min(jnp.where(le2 == v2, lane_f, float(ROUTER_LANES)), axis=-1, keepdims=True)
        e2 = jnp.exp(v2 - v1)
        p1 = gate_g / (1.0 + e2)
        comb = jnp.where(lane_f == i1, p1, 0.0) + jnp.where(lane_f == i2, p1 * e2, 0.0)
        for grp in range(N_GROUPS):
            comb_ref[grp] = pltpu.roll(comb, ROUTER_LANES - (N_GROUPS + grp * EXP_PER_GROUP), 1)
        acc_ref[...] = jnp.zeros_like(acc_ref)

    h = h_ref[...]
    hid = jax.nn.silu(jnp.dot(h, wg_ref[...], preferred_element_type=F32)) * jnp.dot(
        h, wu_ref[...], preferred_element_type=F32)
    comb = comb_ref[f]
    hid = jnp.concatenate([hid[:, k * EXPERT_FF:(k + 1) * EXPERT_FF] * comb[:, k:k + 1]
                           for k in range(EXP_PER_GROUP)], axis=-1)
    acc_ref[...] += jnp.dot(hid.astype(BF16), wd_ref[...], preferred_element_type=F32)

    @pl.when(f == N_GROUPS - 1)
    def _():
        out = x1_ref[...] + m[5:6] * acc_ref[...]
        if final:
            out = out * lax.rsqrt(jnp.mean(out * out, axis=-1, keepdims=True) + EPS) * fw_ref[...]
        o_ref[...] = out


def moe_pallas(x, ya, yb, mod, nw, wo_a, wo_b, wr, br, wg, wu, wd, final_w, rows_per_mod, final):
    t = x.shape[0]
    tiles_per_mod = rows_per_mod // TM
    row = lambda i, f: (i, 0)
    const = lambda i, f: (0, 0)
    return pl.pallas_call(
        functools.partial(_moe_kernel, final=final),
        grid=(t // TM, N_GROUPS),
        in_specs=[pl.BlockSpec((TM, D_MODEL), row),
                  pl.BlockSpec((TM, ya.shape[1]), row),
                  pl.BlockSpec((TM, yb.shape[1]), row),
                  pl.BlockSpec((1, MOD_ROWS, D_MODEL), lambda i, f: (i // tiles_per_mod, 0, 0)),
                  pl.BlockSpec((1, D_MODEL), const),
                  pl.BlockSpec(wo_a.shape, const),
                  pl.BlockSpec(wo_b.shape, const),
                  pl.BlockSpec((3 * D_MODEL, ROUTER_LANES), const),
                  pl.BlockSpec((1, ROUTER_LANES), const),
                  pl.BlockSpec((D_MODEL, MOE_FCHUNK), lambda i, f: (0, f)),
                  pl.BlockSpec((D_MODEL, MOE_FCHUNK), lambda i, f: (0, f)),
                  pl.BlockSpec((MOE_FCHUNK, D_MODEL), lambda i, f: (f, 0)),
                  pl.BlockSpec((1, D_MODEL), const)],
        out_specs=pl.BlockSpec((TM, D_MODEL), row),
        out_shape=jax.ShapeDtypeStruct((t, D_MODEL), F32),
        scratch_shapes=[pltpu.VMEM((TM, D_MODEL), F32), pltpu.VMEM((TM, D_MODEL), BF16),
                        pltpu.VMEM((N_GROUPS, TM, ROUTER_LANES), F32), pltpu.VMEM((TM, D_MODEL), F32)],
        compiler_params=pltpu.CompilerParams(dimension_semantics=("arbitrary", "arbitrary"),
                                             vmem_limit_bytes=VMEM_LIMIT),
        name="out_proj_moe",
    )(x, ya, yb, mod, nw.reshape(1, D_MODEL), wo_a, wo_b, wr, br, wg, wu, wd, final_w.reshape(1, D_MODEL))


def _pad_cols(w, n):
    return jnp.pad(w, ((0, 0), (0, n - w.shape[1])))


def _router_weights(rg, re):
    w = _pad_cols(jnp.concatenate([rg, re], axis=1).astype(F32), ROUTER_LANES)
    hi = w.astype(BF16)
    lo = (w - hi.astype(F32)).astype(BF16)
    return jnp.concatenate([hi, hi, lo], axis=0)


def _mod_table(m):
    m = m.reshape(m.shape[0], 6, D_MODEL)
    return jnp.pad(m, ((0, 0), (0, MOD_ROWS - 6), (0, 0)))


def kernel(x_prompt, x_sample, c, c_ctx, state_dn, cache_b_k, cache_b_v, cache_c_k, cache_c_v, state_s5,
           ada_w, ada_b, norm_w, w_in_e, dn_conv_w, dn_a_log, dn_dt_bias, dn_norm_w, b_sink, w_out_e,
           w_in_o, c_q_norm, c_k_norm, s5_lam_re, s5_lam_im, s5_log_dt, s5_b_re, s5_b_im, s5_c_re, s5_c_im,
           s5_d, s5_glu_w, s5_glu_b, w_out_o, moe_rg, moe_rg_b, moe_re, moe_re_b, moe_wg, moe_wu, moe_wd,
           final_norm_w):
    rope_tabs = rope_tables()
    cvec = jnp.concatenate([c_ctx[None, :], c, jnp.zeros((ADA_ROWS - 1 - DEC_BATCH, D_MODEL), F32)], axis=0)
    mods = ada_pallas(cvec, ada_w, ada_b)
    xp = x_prompt.reshape(BATCH * SEQ, D_MODEL)
    xs = x_sample.reshape(DEC_BATCH * DEC_SEQ, D_MODEL)
    new_dn, new_bk, new_bv, new_ck, new_cv, new_s5 = [], [], [], [], [], []
    for layer in range(DEPTH):
        mod_p = _mod_table(mods[layer, 0:1])
        mod_s = _mod_table(mods[layer, 1:1 + DEC_BATCH])
        i = layer // 2
        if layer % 2 == 0:
            ep = {'conv_w': dn_conv_w[i], 'a_log': dn_a_log[i], 'dt_bias': dn_dt_bias[i],
                  'norm_w': dn_norm_w[i], 'sink': b_sink[i]}
            qkv_w, z_w, a_w, bb_w, bq_w, bk_w, bv_w = split_cols(w_in_e[i].astype(BF16), EVEN_SPLITS)
            w_in = [qkv_w, z_w, _pad_cols(jnp.concatenate([a_w, bb_w], axis=1), LANES), bq_w, bk_w, bv_w]
            w_out = w_out_e[i].astype(BF16)
            wo_a, wo_b = w_out[:HA * DK], w_out[HA * DK:]
            proj_p = inproj_pallas(xp, mod_p, norm_w[layer, 0], w_in, BATCH * SEQ)
            proj_s = inproj_pallas(xs, mod_s, norm_w[layer, 0], w_in, DEC_SEQ)
            ya_p, yb_p, dn, bk, bv = even_mixer_context(proj_p, ep)
            ya_s, yb_s = even_mixer_latent(proj_s, ep, state_dn[:, i], cache_b_k[:, i], cache_b_v[:, i], rope_tabs)
            new_dn.append(dn)
            new_bk.append(bk)
            new_bv.append(bv)
        else:
            op = {'q_norm': c_q_norm[i], 'k_norm': c_k_norm[i], 'lam_re': s5_lam_re[i],
                  'lam_im': s5_lam_im[i], 'log_dt': s5_log_dt[i], 'b_re': s5_b_re[i], 'b_im': s5_b_im[i],
                  'c_re': s5_c_re[i], 'c_im': s5_c_im[i], 'd': s5_d[i], 'glu_w': s5_glu_w[i],
                  'glu_b': s5_glu_b[i]}
            w_in = split_cols(w_in_o[i].astype(BF16), ODD_SPLITS)
            w_out = w_out_o[i].astype(BF16)
            wo_a, wo_b = w_out[:HC * HEAD_DIM], w_out[HC * HEAD_DIM:]
            proj_p = inproj_pallas(xp, mod_p, norm_w[layer, 0], w_in, BATCH * SEQ)
            proj_s = inproj_pallas(xs, mod_s, norm_w[layer, 0], w_in, DEC_SEQ)
            ya_p, yb_p, ck, cv, s5 = odd_mixer_context(proj_p, op)
            ya_s, yb_s = odd_mixer_latent(proj_s, op, cache_c_k[:, i], cache_c_v[:, i], state_s5[:, i], rope_tabs)
            new_ck.append(ck)
            new_cv.append(cv)
            new_s5.append(s5)
        wr = _router_weights(moe_rg[layer], moe_re[layer])
        br = _pad_cols(jnp.concatenate([moe_rg_b[layer], moe_re_b[layer]])[None, :], ROUTER_LANES)
        moe_w = (wo_a, wo_b, wr, br, moe_wg[layer].astype(BF16), moe_wu[layer].astype(BF16),
                 moe_wd[layer].astype(BF16))
        last = layer == DEPTH - 1
        xp = moe_pallas(xp, ya_p, yb_p, mod_p, norm_w[layer, 1], *moe_w, final_norm_w, BATCH * SEQ, last)
        xs = moe_pallas(xs, ya_s, yb_s, mod_s, norm_w[layer, 1], *moe_w, final_norm_w, DEC_SEQ, last)
    y_prompt = xp.reshape(x_prompt.shape)
    y_sample = xs.reshape(x_sample.shape)
    return (y_prompt, y_sample, jnp.stack(new_dn, axis=1), jnp.stack(new_bk, axis=1), jnp.stack(new_bv, axis=1),
            jnp.stack(new_ck, axis=1), jnp.stack(new_cv, axis=1), jnp.stack(new_s5, axis=1))
```

```python
import functools

import numpy as np
import jax
import jax.numpy as jnp
from jax import lax
from jax.experimental import pallas as pl
from jax.experimental.pallas import tpu as pltpu

D_MODEL = 1024
BATCH = 32
SEQ = 256
DEPTH = 2
DEC_BATCH = 8
DEC_SEQ = 2048
PAST_LEN = 256

GRID_W = 64
HEAD_DIM = 64
BLK = 128
ROPE_THETA = 10000.0
EPS = 1e-6
N_EVEN = (DEPTH + 1) // 2
N_ODD = DEPTH // 2
HA = D_MODEL // 128
DK = HEAD_DIM
CONV_W = 5
DN_CHUNK = 64
HB = D_MODEL // 128
HKV_B = HB // 4
WINDOW = 128
HC = (3 * D_MODEL // 4) // HEAD_DIM
HKV_C = HC // 3
D_S5 = D_MODEL // 4
S5_CH = 16
G_D = D_S5 // S5_CH
S5_P = 64
N_GROUPS = 4
EXP_PER_GROUP = 8
N_EXP = N_GROUPS * EXP_PER_GROUP
EXPERT_FF = D_MODEL // 8
TOP_K = 2

EVEN_SPLITS = (3 * HA * DK, HA * DK, 2 * HA, 2 * HA, HB * HEAD_DIM, HKV_B * HEAD_DIM, HKV_B * HEAD_DIM)
D_IN_EVEN = sum(EVEN_SPLITS)
D_MIX_EVEN = HA * DK + HB * HEAD_DIM
ODD_SPLITS = (HC * HEAD_DIM, HKV_C * HEAD_DIM, HKV_C * HEAD_DIM, D_S5)
D_IN_ODD = sum(ODD_SPLITS)
D_MIX_ODD = HC * HEAD_DIM + D_S5

F32 = jnp.float32
BF16 = jnp.bfloat16


def split_cols(x, sizes):
    idx = np.cumsum(sizes)[:-1].tolist()
    return jnp.split(x, idx, axis=-1)


def axial_rope_tables(rows):
    row = jnp.repeat(jnp.arange(rows), GRID_W).astype(F32)
    col = jnp.tile(jnp.arange(GRID_W), rows).astype(F32)
    quarter = HEAD_DIM // 4
    freqs = ROPE_THETA ** (-jnp.arange(quarter, dtype=F32) / quarter)
    ang_r = row[:, None] * freqs
    ang_c = col[:, None] * freqs
    ang = jnp.concatenate([ang_r, ang_r, ang_c, ang_c], axis=-1)
    return jnp.cos(ang), jnp.sin(ang)


SUBLANES = 8
VMEM_LIMIT = 56 * 1024 * 1024
S5_ROWS = 1024
S5_STATE = G_D * S5_P


def _s5_scan_kernel(u_ref, bmat_ref, cmat_ref, a_ref, s0_ref, y_ref, sfin_ref, xs_ref, st_ref, *, nblk, bsz):
    dr = pl.program_id(0)
    blk = pl.program_id(1)

    @pl.when(blk == 0)
    def _():
        st_ref[...] = s0_ref[0]

    xs_ref[...] = jnp.dot(u_ref[...].astype(BF16), bmat_ref[0], preferred_element_type=F32)
    a_re = jnp.broadcast_to(a_ref[0, 0:1, :], (SUBLANES, S5_STATE))
    a_im = jnp.broadcast_to(a_ref[0, 1:2, :], (SUBLANES, S5_STATE))
    steps = S5_ROWS // bsz
    for sg in range(bsz // SUBLANES):
        rows = pl.ds(sg * SUBLANES, SUBLANES)

        def body(i, carry, sg=sg):
            x_re, x_im = carry
            l = i + dr * (steps - 1 - 2 * i)
            r = pl.ds(pl.multiple_of(l * bsz + sg * SUBLANES, SUBLANES), SUBLANES)
            n_re = a_re * x_re - a_im * x_im + xs_ref[r, 0:S5_STATE]
            n_im = a_re * x_im + a_im * x_re + xs_ref[r, S5_STATE:2 * S5_STATE]
            xs_ref[r, 0:S5_STATE] = n_re
            xs_ref[r, S5_STATE:2 * S5_STATE] = n_im
            return n_re, n_im

        x_re, x_im = lax.fori_loop(0, steps, body, (st_ref[rows, 0:S5_STATE], st_ref[rows, S5_STATE:2 * S5_STATE]),
                                   unroll=4)
        st_ref[rows, 0:S5_STATE] = x_re
        st_ref[rows, S5_STATE:2 * S5_STATE] = x_im
    y_ref[0] = jnp.dot(xs_ref[...].astype(BF16), cmat_ref[0], preferred_element_type=F32)

    @pl.when(blk == nblk - 1)
    def _():
        sfin_ref[0] = st_ref[...]


def _s5_out_kernel(yf_ref, yb_ref, u_ref, d_ref, w_ref, b_ref, o_ref):
    y = yf_ref[0] + yb_ref[0] + u_ref[...] * d_ref[...]
    y = jax.nn.gelu(y)
    gate = jnp.dot(y.astype(BF16), w_ref[...], preferred_element_type=F32) + b_ref[...]
    o_ref[...] = y * jax.nn.sigmoid(gate)


def s5_operators(op):
    lam_re, lam_im = op['lam_re'].astype(F32), op['lam_im'].astype(F32)
    dt = jnp.exp(op['log_dt'].astype(F32))[..., None]
    mag = jnp.exp(lam_re * dt)
    a_re, a_im = mag * jnp.cos(lam_im * dt), mag * jnp.sin(lam_im * dt)
    den = lam_re * lam_re + lam_im * lam_im
    k_re = ((a_re - 1) * lam_re + a_im * lam_im) / den
    k_im = (a_im * lam_re - (a_re - 1) * lam_im) / den
    b_re, b_im = op['b_re'].astype(F32), op['b_im'].astype(F32)
    bb_re = k_re[..., None] * b_re - k_im[..., None] * b_im
    bb_im = k_re[..., None] * b_im + k_im[..., None] * b_re
    eye = jnp.eye(G_D, dtype=F32)

    def b_blocks(t):
        return jnp.einsum('dgpc,gh->dgchp', t, eye).reshape(2, D_S5, S5_STATE)

    def c_blocks(t):
        return jnp.einsum('dgcp,gh->dgphc', t, eye).reshape(2, S5_STATE, D_S5)

    b_blk = jnp.concatenate([b_blocks(bb_re), b_blocks(bb_im)], axis=-1).astype(BF16)
    c_blk = jnp.concatenate([c_blocks(op['c_re'].astype(F32)), -c_blocks(op['c_im'].astype(F32))],
                            axis=1).astype(BF16)
    a_vec = jnp.stack([a_re.reshape(2, S5_STATE), a_im.reshape(2, S5_STATE)], axis=1)
    return b_blk, c_blk, a_vec


def s5_pallas(u, op, ops, s0):
    b, l, _ = u.shape
    b_blk, c_blk, a_vec = ops
    n = b * l
    nblk = n // S5_ROWS
    u_t = jnp.swapaxes(u, 0, 1).reshape(n, D_S5)

    def rows_map(dr, blk):
        return (blk + dr * (nblk - 1 - 2 * blk), 0)

    y2, s_fin = pl.pallas_call(
        functools.partial(_s5_scan_kernel, nblk=nblk, bsz=b),
        grid=(2, nblk),
        in_specs=[
            pl.BlockSpec((S5_ROWS, D_S5), rows_map),
            pl.BlockSpec((1, D_S5, 2 * S5_STATE), lambda dr, blk: (dr, 0, 0)),
            pl.BlockSpec((1, 2 * S5_STATE, D_S5), lambda dr, blk: (dr, 0, 0)),
            pl.BlockSpec((1, 2, S5_STATE), lambda dr, blk: (dr, 0, 0)),
            pl.BlockSpec((1, b, 2 * S5_STATE), lambda dr, blk: (dr, 0, 0)),
        ],
        out_specs=[
            pl.BlockSpec((1, S5_ROWS, D_S5), lambda dr, blk: (dr,) + rows_map(dr, blk)),
            pl.BlockSpec((1, b, 2 * S5_STATE), lambda dr, blk: (dr, 0, 0)),
        ],
        out_shape=[jax.ShapeDtypeStruct((2, n, D_S5), F32), jax.ShapeDtypeStruct((2, b, 2 * S5_STATE), F32)],
        scratch_shapes=[pltpu.VMEM((S5_ROWS, 2 * S5_STATE), F32), pltpu.VMEM((b, 2 * S5_STATE), F32)],
        compiler_params=pltpu.CompilerParams(dimension_semantics=("arbitrary", "arbitrary"),
                                             vmem_limit_bytes=VMEM_LIMIT),
        name="s5_scan",
    )(u_t, b_blk, c_blk, a_vec, s0)
    tm = S5_ROWS
    y = pl.pallas_call(
        _s5_out_kernel,
        grid=(n // tm,),
        in_specs=[
            pl.BlockSpec((1, tm, D_S5), lambda i: (0, i, 0)),
            pl.BlockSpec((1, tm, D_S5), lambda i: (1, i, 0)),
            pl.BlockSpec((tm, D_S5), lambda i: (i, 0)),
            pl.BlockSpec((1, D_S5), lambda i: (0, 0)),
            pl.BlockSpec((D_S5, D_S5), lambda i: (0, 0)),
            pl.BlockSpec((1, D_S5), lambda i: (0, 0)),
        ],
        out_specs=pl.BlockSpec((tm, D_S5), lambda i: (i, 0)),
        out_shape=jax.ShapeDtypeStruct((n, D_S5), F32),
        name="s5_out",
    )(y2, y2, u_t, op['d'].astype(F32).reshape(1, D_S5), op['glu_w'].astype(BF16),
      op['glu_b'].astype(F32).reshape(1, D_S5))
    return jnp.swapaxes(y.reshape(l, b, D_S5), 0, 1), s_fin


LANES = 128
DN_PAIRS = HA * DK // LANES
NEG_BIG = -1e30
DN_UNROLL = 4


def _nt_dot(a, b):
    return lax.dot_general(a, b, (((1,), (1,)), ((), ())), preferred_element_type=F32)


def _bf16_dot(a, b):
    return jnp.dot(a.astype(BF16), b.astype(BF16), preferred_element_type=F32)


def _select_dot(x, sel):
    x1 = x.astype(BF16)
    r1 = x - x1.astype(F32)
    x2 = r1.astype(BF16)
    x3 = (r1 - x2.astype(F32)).astype(BF16)
    dot = functools.partial(jnp.dot, preferred_element_type=F32)
    return dot(x1, sel) + dot(x2, sel) + dot(x3, sel)


def _head_sums(x):
    lane = lax.broadcasted_iota(jnp.int32, x.shape, 1)
    lo = lane < HEAD_DIM
    s0 = jnp.sum(jnp.where(lo, x, 0.0), axis=-1, keepdims=True)
    s1 = jnp.sum(jnp.where(lo, 0.0, x), axis=-1, keepdims=True)
    return jnp.where(lo, s0, s1)


def _chunk_cumsum(x, reverse):
    n = x.shape[0]
    pos = lax.broadcasted_iota(jnp.int32, x.shape, 0) % DN_CHUNK
    s = 1
    while s < DN_CHUNK:
        if reverse:
            x = x + jnp.where(pos < DN_CHUNK - s, pltpu.roll(x, n - s, 0), 0.0)
        else:
            x = x + jnp.where(pos >= s, pltpu.roll(x, s, 0), 0.0)
        s *= 2
    return x


def _short_conv_silu(x, w):
    n = x.shape[0]
    row = lax.broadcasted_iota(jnp.int32, x.shape, 0)
    pad = CONV_W // 2
    y = x * w[pad:pad + 1, :]
    for t in range(CONV_W):
        s = t - pad
        if s == 0:
            continue
        shifted = pltpu.roll(x, (-s) % n, 0)
        valid = (row + s >= 0) & (row + s < n)
        y = y + jnp.where(valid, shifted, 0.0) * w[t:t + 1, :]
    return jax.nn.silu(y)


def _unit_tri_solve(ms, rhss, ri, ci):
    n = rhss[0].shape[-1]
    same16 = (ri // 16) == (ci // 16)
    dot = functools.partial(jnp.dot, preferred_element_type=F32)

    def apply(p, r):
        hi = r.astype(BF16)
        lo = (r - hi.astype(F32)).astype(BF16)
        x = dot(p, jnp.concatenate([hi, lo], axis=-1))
        return x[:, 0:n] + x[:, n:2 * n]

    diag = [jnp.where(same16, m, 0.0) for m in ms]
    rest = [jnp.where(same16, 0.0, m).astype(BF16) for m in ms]
    pows = [d.astype(BF16) for d in diag]
    tinv = [jnp.where(ri == ci, 1.0, 0.0) - d for d in diag]
    for _ in range(3):
        pows = [dot(p, p).astype(BF16) for p in pows]
        tinv = [t + dot(t.astype(BF16), p) for t, p in zip(tinv, pows)]
    tinv = [t.astype(BF16) for t in tinv]
    p1 = [dot(t, e).astype(BF16) for t, e in zip(tinv, rest)]
    sols = [apply(t, r) for t, r in zip(tinv, rhss)]
    p2 = [dot(p, p).astype(BF16) for p in p1]
    sols = [s - apply(p, s) for p, s in zip(p1, sols)]
    return [s + apply(p, s) for p, s in zip(p2, sols)]


def _deltanet_kernel(q_ref, k_ref, v_ref, z_ref, a_ref, bb_ref, cwq_ref, cwk_ref, cwv_ref, alog_ref, dtb_ref, nw_ref,
                     s0_ref, y_ref, sfin_ref,
                     qs_ref, ks_ref, vs_ref, o_ref, gc_ref, beta_ref, u_ref, w_ref, attn_ref, qp_ref, kpt_ref,
                     etot_ref, *, seq):
    hp = pl.program_id(1)
    nchunk = seq // DN_CHUNK
    c64 = DN_CHUNK
    head_lanes = [slice(j * HEAD_DIM, (j + 1) * HEAD_DIM) for j in range(2)]

    q = _short_conv_silu(q_ref[0], cwq_ref[...])
    q = q * lax.rsqrt(_head_sums(q * q) + EPS) * (DK ** -0.5)
    k = _short_conv_silu(k_ref[0], cwk_ref[...])
    k = k * lax.rsqrt(_head_sums(k * k) + EPS)
    v = _short_conv_silu(v_ref[0], cwv_ref[...])
    qs_ref[...] = q
    ks_ref[...] = k
    vs_ref[...] = v
    o_ref[...] = jnp.zeros_like(o_ref)

    g_col = -jnp.exp(alog_ref[...]) * jax.nn.softplus(a_ref[0] + dtb_ref[...])
    b_col = jax.nn.sigmoid(bb_ref[0])
    sel_row = lax.broadcasted_iota(jnp.int32, (2 * HA, 2 * LANES), 0)
    sel_lane = lax.broadcasted_iota(jnp.int32, (2 * HA, 2 * LANES), 1)
    sel = (sel_row == (sel_lane // LANES) * HA + 2 * hp + (sel_lane // HEAD_DIM) % 2).astype(BF16)
    g_b = _select_dot(g_col, sel)
    gc_ref[:, 0:LANES] = _chunk_cumsum(g_b[:, 0:LANES], reverse=False)
    gc_ref[:, LANES:2 * LANES] = _chunk_cumsum(g_b[:, LANES:2 * LANES], reverse=True)
    beta_ref[...] = _select_dot(b_col, sel)

    ri = lax.broadcasted_iota(jnp.int32, (c64, c64), 0)
    ci = lax.broadcasted_iota(jnp.int32, (c64, c64), 1)

    def intra_chunk(step, carry):
        chunk_ids = [step * DN_UNROLL + t for t in range(DN_UNROLL)]
        rows_c = [pl.ds(pl.multiple_of(c * c64, c64), c64) for c in chunk_ids]
        loads = [(qs_ref[r, :], ks_ref[r, :], vs_ref[r, :], gc_ref[r, :], beta_ref[r, :]) for r in rows_c]
        heads = [(t, j, ld[0][:, head_lanes[j]], ld[1][:, head_lanes[j]], ld[2][:, head_lanes[j]])
                 for t, ld in enumerate(loads) for j in range(2)]
        gate_lanes = [slice((c % 2) * LANES + (c // 2) * HEAD_DIM, (c % 2) * LANES + (c // 2 + 1) * HEAD_DIM)
                      for c in range(4)]
        gates = {(t, c): (ld[3][:, gate_lanes[c]], ld[4][:, gate_lanes[c]])
                 for t, ld in enumerate(loads) for c in range(4)}
        kbs = [h[3].astype(BF16) for h in heads]
        kks = [_nt_dot(kb, kb) for kb in kbs]
        qks = [_nt_dot(h[2].astype(BF16), kb) for h, kb in zip(heads, kbs)]
        chains, ms, rhss = [], [], []
        for (t, j, qj, kj, vj), kk, qk in zip(heads, kks, qks):
            for d in range(2):
                gcb, bet = gates[(t, 2 * j + d)]
                diff = gcb - gcb.T
                earlier = (ri >= ci) if d == 0 else (ri <= ci)
                strict = (ri > ci) if d == 0 else (ri < ci)
                dec = jnp.exp(jnp.where(earlier, diff, NEG_BIG))
                eg = jnp.exp(gcb)
                gtot = gcb[c64 - 1:c64, :] if d == 0 else gcb[0:1, :]
                ms.append(jnp.where(strict, kk * bet * dec, 0.0))
                rhss.append(jnp.concatenate([vj * bet, kj * bet * eg], axis=-1))
                chains.append((t, 2 * j + d, qk * dec, qj * eg, (kj * jnp.exp(gtot - gcb)).T,
                               jnp.broadcast_to(jnp.exp(gtot), (SUBLANES, HEAD_DIM))))
        uws = _unit_tri_solve(ms, rhss, ri, ci)
        for (t, c, attn, qp, kpt, etot), uw in zip(chains, uws):
            u_ref[c, rows_c[t], :] = uw[:, 0:HEAD_DIM]
            w_ref[c, rows_c[t], :] = uw[:, HEAD_DIM:2 * HEAD_DIM].astype(BF16)
            attn_ref[c, rows_c[t], :] = attn.astype(BF16)
            qp_ref[c, rows_c[t], :] = qp.astype(BF16)
            kpt_ref[c, rows_c[t], :] = kpt.astype(BF16)
            etot_ref[c, pl.ds(pl.multiple_of(chunk_ids[t] * SUBLANES, SUBLANES), SUBLANES), :] = etot
        return carry

    lax.fori_loop(0, nchunk // DN_UNROLL, intra_chunk, 0)

    def inter_chunk(i, states):
        rows_d = [pl.ds(pl.multiple_of(cc * c64, c64), c64) for cc in (i, nchunk - 1 - i)]
        erow_d = [pl.ds(pl.multiple_of(cc * SUBLANES, SUBLANES), SUBLANES) for cc in (i, nchunk - 1 - i)]
        loaded = [(w_ref[c, rows_d[c % 2], :], qp_ref[c, rows_d[c % 2], :], u_ref[c, rows_d[c % 2], :],
                   attn_ref[c, rows_d[c % 2], :], kpt_ref[c, rows_d[c % 2], :], etot_ref[c, erow_d[c % 2], :],
                   o_ref[c // 2, rows_d[c % 2], :]) for c in range(4)]
        ws_qs = [jnp.dot(jnp.concatenate([ld[0], ld[1]], axis=0), s.astype(BF16), preferred_element_type=F32)
                 for ld, s in zip(loaded, states)]
        v_new = [(ld[2] - r[0:c64]).astype(BF16) for ld, r in zip(loaded, ws_qs)]
        av = [jnp.dot(ld[3], vn, preferred_element_type=F32) for ld, vn in zip(loaded, v_new)]
        kv = [jnp.dot(ld[4], vn, preferred_element_type=F32) for ld, vn in zip(loaded, v_new)]
        for c in range(4):
            o_ref[c // 2, rows_d[c % 2], :] = loaded[c][6] + ws_qs[c][c64:2 * c64] + av[c]
        return tuple(s * ld[5][0:1, :] + x for s, ld, x in zip(states, loaded, kv))

    init = tuple(s0_ref[0, c % 2, c // 2] for c in range(4))
    fin = lax.fori_loop(0, nchunk, inter_chunk, init)
    for c in range(4):
        sfin_ref[0, c % 2, c // 2] = fin[c]

    o = jnp.concatenate([o_ref[0], o_ref[1]], axis=-1)
    o = o * lax.rsqrt(_head_sums(o * o) * (1.0 / DK) + EPS) * nw_ref[...]
    y_ref[0] = o * jax.nn.silu(z_ref[0])


def deltanet_pallas(qkv, z, a, bb, ep, s0):
    b, l, _ = qkv.shape
    npair = DN_PAIRS

    def col_spec(off):
        return pl.BlockSpec((1, l, LANES), lambda i, hp: (i, 0, off + hp))

    def cw_spec(off):
        return pl.BlockSpec((CONV_W, LANES), lambda i, hp: (0, off + hp))

    def full(shape):
        return pl.BlockSpec(shape, lambda i, hp: (0,) * len(shape))

    state_spec = pl.BlockSpec((1, 2, 2, DK, DK), lambda i, hp: (i, 0, hp, 0, 0))
    row_f32 = pltpu.VMEM((l, LANES), F32)
    gate_f32 = pltpu.VMEM((l, 2 * LANES), F32)
    chain_f32 = pltpu.VMEM((4, l, HEAD_DIM), F32)
    chain_bf16 = pltpu.VMEM((4, l, HEAD_DIM), BF16)
    y, s_fin = pl.pallas_call(
        functools.partial(_deltanet_kernel, seq=l),
        grid=(b, npair),
        in_specs=[col_spec(0), col_spec(npair), col_spec(2 * npair), col_spec(0),
                  pl.BlockSpec((1, l, 2 * HA), lambda i, hp: (i, 0, 0)),
                  pl.BlockSpec((1, l, 2 * HA), lambda i, hp: (i, 0, 0)),
                  cw_spec(0), cw_spec(npair), cw_spec(2 * npair),
                  full((1, 2 * HA)), full((1, 2 * HA)), full((1, LANES)), state_spec],
        out_specs=[col_spec(0), state_spec],
        out_shape=[jax.ShapeDtypeStruct((b, l, HA * DK), F32), jax.ShapeDtypeStruct((b, 2, HA, DK, DK), F32)],
        scratch_shapes=[row_f32, row_f32, row_f32, pltpu.VMEM((2, l, HEAD_DIM), F32), gate_f32, gate_f32,
                        chain_f32, chain_bf16, chain_bf16, chain_bf16, chain_bf16,
                        pltpu.VMEM((4, l // DN_CHUNK * SUBLANES, HEAD_DIM), F32)],
        compiler_params=pltpu.CompilerParams(dimension_semantics=("arbitrary", "arbitrary"),
                                             vmem_limit_bytes=VMEM_LIMIT),
        name="deltanet",
    )(qkv, qkv, qkv, z, a, bb, ep['conv_w'], ep['conv_w'], ep['conv_w'],
      ep['a_log'].astype(F32).reshape(1, 2 * HA), ep['dt_bias'].astype(F32).reshape(1, 2 * HA),
      jnp.tile(ep['norm_w'].astype(F32), 2).reshape(1, LANES), s0)
    return y, s_fin


TM = 512
ATT_TQ = BLK
ROT = HEAD_DIM // 4


def rope_tables():
    cos, sin = axial_rope_tables(DEC_SEQ // GRID_W)
    cos2 = jnp.tile(cos, (1, LANES // HEAD_DIM))
    sin2 = jnp.tile(sin, (1, LANES // HEAD_DIM))
    first = (jnp.arange(LANES) % (2 * ROT)) < ROT
    return cos2, jnp.where(first, -sin2, 0.0), jnp.where(first, 0.0, sin2)


def _head_prep(x, w_ref, tabs, scale):
    outs, normed = [], []
    for c in range(x.shape[1] // LANES):
        xc = x[:, c * LANES:(c + 1) * LANES]
        if w_ref is not None:
            xc = xc * lax.rsqrt(_head_sums(xc * xc) * (1.0 / HEAD_DIM) + EPS) * w_ref[...]
            normed.append(xc)
        if tabs is not None:
            cos_ref, sup_ref, sdn_ref = tabs
            xc = (xc * cos_ref[...] + pltpu.roll(xc, LANES - ROT, 1) * sup_ref[...]
                  + pltpu.roll(xc, ROT, 1) * sdn_ref[...])
        outs.append((xc * scale).astype(BF16))
    cat = lambda parts: jnp.concatenate(parts, axis=-1) if len(parts) > 1 else parts[0]
    return cat(outs), (cat(normed) if w_ref is not None else None)


def _attn_kernel(q_ref, k_ref, v_ref, *refs, grp, hk, windowed, has_ctx, has_sink):
    refs = list(refs)
    kc_ref, vc_ref = (refs.pop(0), refs.pop(0)) if has_ctx else (None, None)
    sink_ref = refs.pop(0) if has_sink else None
    o_ref = refs[0]
    tq = q_ref.shape[1]
    seq_k = k_ref.shape[1]
    if windowed:
        qi = pl.program_id(1)
        start = jnp.clip((qi - 1) * tq, 0, seq_k - 3 * tq)
        krows = pl.ds(pl.multiple_of(start, tq), 3 * tq)
        qpos = qi * tq + lax.broadcasted_iota(jnp.int32, (grp * tq, 3 * tq), 0) % tq
        kpos = start + lax.broadcasted_iota(jnp.int32, (grp * tq, 3 * tq), 1)
        keep = jnp.abs(qpos - kpos) <= WINDOW
    else:
        krows = pl.ds(0, seq_k)
    q = q_ref[0]
    groups = range(hk)
    lanes = [slice(g * HEAD_DIM, (g + 1) * HEAD_DIM) for g in groups]
    qgs = [jnp.concatenate([q[:, (g * grp + j) * HEAD_DIM:(g * grp + j + 1) * HEAD_DIM] for j in range(grp)], axis=0)
           for g in groups]
    ss = [_nt_dot(qg, k_ref[0, krows, gl]) for qg, gl in zip(qgs, lanes)]
    if windowed:
        ss = [jnp.where(keep, s, NEG_BIG) for s in ss]
    ms = [jnp.max(s, axis=-1, keepdims=True) for s in ss]
    if has_ctx:
        kc = kc_ref[0].astype(BF16)
        scs = [_nt_dot(qg, kc[:, gl]) for qg, gl in zip(qgs, lanes)]
        ms = [jnp.maximum(m, jnp.max(sc, axis=-1, keepdims=True)) for m, sc in zip(ms, scs)]
    if has_sink:
        head_rows = [slice(j * tq, (j + 1) * tq) for j in range(grp)]
        ms = [jnp.concatenate([jnp.maximum(m[head_rows[j]], sink_ref[g * grp + j]) for j in range(grp)], axis=0)
              for g, m in zip(groups, ms)]
    ps = [jnp.exp(s - m) for s, m in zip(ss, ms)]
    dens = [jnp.sum(p, axis=-1, keepdims=True) for p in ps]
    os_ = [jnp.dot(p.astype(BF16), v_ref[0, krows, gl], preferred_element_type=F32) for p, gl in zip(ps, lanes)]
    if has_ctx:
        vc = vc_ref[0].astype(BF16)
        pcs = [jnp.exp(sc - m) for sc, m in zip(scs, ms)]
        dens = [den + jnp.sum(pc, axis=-1, keepdims=True) for den, pc in zip(dens, pcs)]
        os_ = [o + jnp.dot(pc.astype(BF16), vc[:, gl], preferred_element_type=F32)
               for o, pc, gl in zip(os_, pcs, lanes)]
    if has_sink:
        dens = [den + jnp.concatenate([jnp.exp(sink_ref[g * grp + j] - m[head_rows[j]]) for j in range(grp)], axis=0)
                for g, den, m in zip(groups, dens, ms)]
    outs = [None] * (grp * hk)
    for g in groups:
        o = os_[g] / dens[g]
        for j in range(grp):
            outs[g * grp + j] = o[j * tq:(j + 1) * tq]
    o_ref[0] = jnp.concatenate(outs, axis=-1)


def attention_pallas(q, k, v, ctx=None, sink=None, windowed=False):
    b, lq, hd = q.shape
    lk, kd = k.shape[1], k.shape[2]
    hk = kd // HEAD_DIM
    grp = hd // kd
    tq = ATT_TQ
    ins = [q, k, v]
    in_specs = [pl.BlockSpec((1, tq, hd), lambda i, t: (i, t, 0)),
                pl.BlockSpec((1, lk, kd), lambda i, t: (i, 0, 0)),
                pl.BlockSpec((1, lk, kd), lambda i, t: (i, 0, 0))]
    if ctx is not None:
        ins += list(ctx)
        in_specs += [pl.BlockSpec((1, ctx[0].shape[1], kd), lambda i, t: (i, 0, 0))] * 2
    if sink is not None:
        ins.append(sink.astype(F32))
        in_specs.append(pl.BlockSpec(memory_space=pltpu.SMEM))
    return pl.pallas_call(
        functools.partial(_attn_kernel, grp=grp, hk=hk, windowed=windowed, has_ctx=ctx is not None,
                          has_sink=sink is not None),
        grid=(b, lq // tq),
        in_specs=in_specs,
        out_specs=pl.BlockSpec((1, tq, hd), lambda i, t: (i, t, 0)),
        out_shape=jax.ShapeDtypeStruct((b, lq, hd), F32),
        compiler_params=pltpu.CompilerParams(dimension_semantics=("arbitrary", "arbitrary"),
                                             vmem_limit_bytes=VMEM_LIMIT),
        name="attention",
    )(*ins)


def even_mixer_context(proj, ep):
    qkv, z, ab, qh, kh, vh, bk, bv = proj
    qkv, z, ab = [t.reshape(BATCH, SEQ, -1) for t in (qkv, z, ab)]
    a, bb = ab[..., 0:2 * HA], ab[..., 2 * HA:4 * HA]
    y_a, dn_fin = deltanet_pallas(qkv, z, a, bb, ep, jnp.zeros((BATCH, 2, HA, DK, DK), F32))
    y_b = attention_pallas(*[t.reshape(BATCH, SEQ, -1) for t in (qh, kh, vh)], sink=ep['sink'])
    return (y_a.reshape(-1, HA * DK), y_b.reshape(-1, HB * HEAD_DIM), dn_fin,
            bk.reshape(BATCH, SEQ, HKV_B, HEAD_DIM), bv.reshape(BATCH, SEQ, HKV_B, HEAD_DIM))


def even_mixer_latent(proj, ep, dn_state, ctx_k, ctx_v):
    qkv, z, ab, qh, kh, vh = proj
    qkv, z, ab = [t.reshape(DEC_BATCH, DEC_SEQ, -1) for t in (qkv, z, ab)]
    a, bb = ab[..., 0:2 * HA], ab[..., 2 * HA:4 * HA]
    y_a, _ = deltanet_pallas(qkv, z, a, bb, ep, dn_state.astype(F32))
    ctx = (ctx_k.reshape(DEC_BATCH, PAST_LEN, -1), ctx_v.reshape(DEC_BATCH, PAST_LEN, -1))
    y_b = attention_pallas(*[t.reshape(DEC_BATCH, DEC_SEQ, -1) for t in (qh, kh, vh)], ctx=ctx, sink=ep['sink'],
                           windowed=True)
    return y_a.reshape(-1, HA * DK), y_b.reshape(-1, HB * HEAD_DIM)


def odd_mixer_context(proj, op):
    u, qh, kh, vh, k_normed, cv = proj
    y_c = attention_pallas(*[t.reshape(BATCH, SEQ, -1) for t in (qh, kh, vh)])
    y_d, s_fin = s5_pallas(u.reshape(BATCH, SEQ, -1), op, s5_operators(op), jnp.zeros((2, BATCH, 2 * S5_STATE), F32))
    s_fin = jnp.transpose(s_fin.reshape(2, BATCH, 2, G_D, S5_P), (1, 0, 3, 4, 2))
    return (y_c.reshape(-1, HC * HEAD_DIM), y_d.reshape(-1, D_S5), k_normed.reshape(BATCH, SEQ, HKV_C, HEAD_DIM),
            cv.reshape(BATCH, SEQ, HKV_C, HEAD_DIM), s_fin)


def odd_mixer_latent(proj, op, ctx_k, ctx_v, s5_state):
    u, qh, kh, vh = proj
    ctx = (ctx_k.reshape(DEC_BATCH, PAST_LEN, -1), ctx_v.reshape(DEC_BATCH, PAST_LEN, -1))
    y_c = attention_pallas(*[t.reshape(DEC_BATCH, DEC_SEQ, -1) for t in (qh, kh, vh)], ctx=ctx)
    s0 = jnp.transpose(s5_state.astype(F32), (1, 0, 4, 2, 3)).reshape(2, DEC_BATCH, 2 * S5_STATE)
    y_d, _ = s5_pallas(u.reshape(DEC_BATCH, DEC_SEQ, -1), op, s5_operators(op), s0)
    return y_c.reshape(-1, HC * HEAD_DIM), y_d.reshape(-1, D_S5)


ADA_ROWS = 16
ADA_TN = 1024
MOD_ROWS = 8
ROUTER_LANES = LANES
MOE_FCHUNK = EXP_PER_GROUP * EXPERT_FF


def _ada_kernel(c_ref, w_ref, b_ref, o_ref):
    o_ref[0] = _bf16_dot(jax.nn.silu(c_ref[...]), w_ref[0]) + b_ref[0]


def ada_pallas(cvec, ada_w, ada_b):
    n = 6 * D_MODEL
    return pl.pallas_call(
        _ada_kernel,
        grid=(DEPTH, n // ADA_TN),
        in_specs=[pl.BlockSpec((ADA_ROWS, D_MODEL), lambda l, j: (0, 0)),
                  pl.BlockSpec((1, D_MODEL, ADA_TN), lambda l, j: (l, 0, j)),
                  pl.BlockSpec((1, 1, ADA_TN), lambda l, j: (l, 0, j))],
        out_specs=pl.BlockSpec((1, ADA_ROWS, ADA_TN), lambda l, j: (l, 0, j)),
        out_shape=jax.ShapeDtypeStruct((DEPTH, ADA_ROWS, n), F32),
        name="ada_modulation",
    )(cvec, ada_w, ada_b.reshape(DEPTH, 1, n))


def _modulated(x, nw, shift, scale):
    return x * lax.rsqrt(jnp.mean(x * x, axis=-1, keepdims=True) + EPS) * nw * (1 + scale) + shift


def _inproj_kernel(x_ref, mod_ref, nw_ref, *refs, n_plain, norm, rope, keep_k, keep_v):
    refs = list(refs)
    m = mod_ref[0]
    h = _modulated(x_ref[...], nw_ref[...], m[0:1], m[1:2]).astype(BF16)
    w_refs = [refs.pop(0) for _ in range(n_plain + 3)]
    qw_ref, kw_ref = (refs.pop(0), refs.pop(0)) if norm else (None, None)
    tabs = [refs.pop(0) for _ in range(3)] if rope else None
    plain_refs = [refs.pop(0) for _ in range(n_plain)]
    qo_ref, ko_ref, vo_ref = refs.pop(0), refs.pop(0), refs.pop(0)
    for w_ref, o_ref in zip(w_refs, plain_refs):
        o_ref[...] = jnp.dot(h, w_ref[...], preferred_element_type=F32)
    q, k, v = [jnp.dot(h, w_ref[...], preferred_element_type=F32) for w_ref in w_refs[n_plain:]]
    qo_ref[...] = _head_prep(q, qw_ref, tabs, HEAD_DIM ** -0.5)[0]
    ko, k_normed = _head_prep(k, kw_ref, tabs, 1.0)
    ko_ref[...] = ko
    vo_ref[...] = v.astype(BF16)
    if keep_k:
        refs.pop(0)[...] = k_normed if norm else k
    if keep_v:
        refs.pop(0)[...] = v


def inproj_pallas(x, mod, nw, plain_ws, qkv_ws, rows_per_mod, norm_ws=None, rope_tabs=None, keep_kv=False):
    t = x.shape[0]
    tiles_per_mod = rows_per_mod // TM
    row = lambda i: (i, 0)
    const = lambda i: (0, 0)
    weights = list(plain_ws) + list(qkv_ws)
    ins = [x, mod, nw.reshape(1, D_MODEL)] + weights
    in_specs = [pl.BlockSpec((TM, D_MODEL), row),
                pl.BlockSpec((1, MOD_ROWS, D_MODEL), lambda i: (i // tiles_per_mod, 0, 0)),
                pl.BlockSpec((1, D_MODEL), const)] + [pl.BlockSpec(w.shape, const) for w in weights]
    if norm_ws is not None:
        ins += [jnp.tile(w.astype(F32), LANES // HEAD_DIM).reshape(1, LANES) for w in norm_ws]
        in_specs += [pl.BlockSpec((1, LANES), const)] * 2
    if rope_tabs is not None:
        tiles = DEC_SEQ // TM
        ins += list(rope_tabs)
        in_specs += [pl.BlockSpec((TM, LANES), lambda i: (i % tiles, 0))] * 3
    outs = [(w.shape[1], F32) for w in plain_ws] + [(w.shape[1], BF16) for w in qkv_ws]
    if keep_kv:
        outs += [(qkv_ws[1].shape[1], F32), (qkv_ws[2].shape[1], F32)]
    return pl.pallas_call(
        functools.partial(_inproj_kernel, n_plain=len(plain_ws), norm=norm_ws is not None,
                          rope=rope_tabs is not None, keep_k=keep_kv, keep_v=keep_kv),
        grid=(t // TM,),
        in_specs=in_specs,
        out_specs=[pl.BlockSpec((TM, n), row) for n, _ in outs],
        out_shape=[jax.ShapeDtypeStruct((t, n), dt) for n, dt in outs],
        compiler_params=pltpu.CompilerParams(dimension_semantics=("arbitrary",), vmem_limit_bytes=VMEM_LIMIT),
        name="modulated_in_proj",
    )(*ins)


def _moe_kernel(x_ref, ya_ref, yb_ref, mod_ref, nw_ref, woa_ref, wob_ref, wr_ref, br_ref, wg_ref, wu_ref, wd_ref,
                fw_ref, o_ref, x1_ref, h_ref, comb_ref, acc_ref, *, final):
    f = pl.program_id(1)
    m = mod_ref[0]

    @pl.when(f == 0)
    def _():
        y = _bf16_dot(ya_ref[...], woa_ref[...]) + _bf16_dot(yb_ref[...], wob_ref[...])
        x1 = x_ref[...] + m[2:3] * y
        x1_ref[...] = x1
        h = _modulated(x1, nw_ref[...], m[3:4], m[4:5])
        h_hi = h.astype(BF16)
        h_ref[...] = h_hi
        h_lo = (h - h_hi.astype(F32)).astype(BF16)
        logits = (jnp.dot(jnp.concatenate([h_hi, h_lo], axis=-1), wr_ref[0:2 * D_MODEL, :], preferred_element_type=F32)
                  + jnp.dot(h_hi, wr_ref[2 * D_MODEL:3 * D_MODEL, :], preferred_element_type=F32) + br_ref[...])
        lane = lax.broadcasted_iota(jnp.int32, logits.shape, 1)
        lane_f = lane.astype(F32)
        is_group = lane < N_GROUPS
        lg = jnp.where(is_group, logits, -jnp.inf)
        g_max = jnp.max(lg, axis=-1, keepdims=True)
        g_idx = jnp.min(jnp.where(lg == g_max, lane_f, float(ROUTER_LANES)), axis=-1, keepdims=True)
        gate_g = 1.0 / jnp.sum(jnp.where(is_group, jnp.exp(logits - g_max), 0.0), axis=-1, keepdims=True)
        lane_group = ((lane + (EXP_PER_GROUP - N_GROUPS)) // EXP_PER_GROUP - 1).astype(F32)
        in_group = (lane >= N_GROUPS) & (lane < N_GROUPS + N_EXP) & (lane_group == g_idx)
        le = jnp.where(in_group, logits, -jnp.inf)
        v1 = jnp.max(le, axis=-1, keepdims=True)
        i1 = jnp.min(jnp.where(le == v1, lane_f, float(ROUTER_LANES)), axis=-1, keepdims=True)
        le2 = jnp.where(lane_f == i1, -jnp.inf, le)
        v2 = jnp.max(le2, axis=-1, keepdims=True)
        i2 = jnp.min(jnp.where(le2 == v2, lane_f, float(ROUTER_LANES)), axis=-1, keepdims=True)
        e2 = jnp.exp(v2 - v1)
        p1 = gate_g / (1.0 + e2)
        comb = jnp.where(lane_f == i1, p1, 0.0) + jnp.where(lane_f == i2, p1 * e2, 0.0)
        for grp in range(N_GROUPS):
            comb_ref[grp] = pltpu.roll(comb, ROUTER_LANES - (N_GROUPS + grp * EXP_PER_GROUP), 1)
        acc_ref[...] = jnp.zeros_like(acc_ref)

    h = h_ref[...]
    hid = jax.nn.silu(jnp.dot(h, wg_ref[...], preferred_element_type=F32)) * jnp.dot(
        h, wu_ref[...], preferred_element_type=F32)
    comb = comb_ref[f]
    hid = jnp.concatenate([hid[:, k * EXPERT_FF:(k + 1) * EXPERT_FF] * comb[:, k:k + 1]
                           for k in range(EXP_PER_GROUP)], axis=-1)
    acc_ref[...] += jnp.dot(hid.astype(BF16), wd_ref[...], preferred_element_type=F32)

    @pl.when(f == N_GROUPS - 1)
    def _():
        out = x1_ref[...] + m[5:6] * acc_ref[...]
        if final:
            out = out * lax.rsqrt(jnp.mean(out * out, axis=-1, keepdims=True) + EPS) * fw_ref[...]
        o_ref[...] = out


def moe_pallas(x, ya, yb, mod, nw, wo_a, wo_b, wr, br, wg, wu, wd, final_w, rows_per_mod, final):
    t = x.shape[0]
    tiles_per_mod = rows_per_mod // TM
    row = lambda i, f: (i, 0)
    const = lambda i, f: (0, 0)
    return pl.pallas_call(
        functools.partial(_moe_kernel, final=final),
        grid=(t // TM, N_GROUPS),
        in_specs=[pl.BlockSpec((TM, D_MODEL), row),
                  pl.BlockSpec((TM, ya.shape[1]), row),
                  pl.BlockSpec((TM, yb.shape[1]), row),
                  pl.BlockSpec((1, MOD_ROWS, D_MODEL), lambda i, f: (i // tiles_per_mod, 0, 0)),
                  pl.BlockSpec((1, D_MODEL), const),
                  pl.BlockSpec(wo_a.shape, const),
                  pl.BlockSpec(wo_b.shape, const),
                  pl.BlockSpec((3 * D_MODEL, ROUTER_LANES), const),
                  pl.BlockSpec((1, ROUTER_LANES), const),
                  pl.BlockSpec((D_MODEL, MOE_FCHUNK), lambda i, f: (0, f)),
                  pl.BlockSpec((D_MODEL, MOE_FCHUNK), lambda i, f: (0, f)),
                  pl.BlockSpec((MOE_FCHUNK, D_MODEL), lambda i, f: (f, 0)),
                  pl.BlockSpec((1, D_MODEL), const)],
        out_specs=pl.BlockSpec((TM, D_MODEL), row),
        out_shape=jax.ShapeDtypeStruct((t, D_MODEL), F32),
        scratch_shapes=[pltpu.VMEM((TM, D_MODEL), F32), pltpu.VMEM((TM, D_MODEL), BF16),
                        pltpu.VMEM((N_GROUPS, TM, ROUTER_LANES), F32), pltpu.VMEM((TM, D_MODEL), F32)],
        compiler_params=pltpu.CompilerParams(dimension_semantics=("arbitrary", "arbitrary"),
                                             vmem_limit_bytes=VMEM_LIMIT),
        name="out_proj_moe",
    )(x, ya, yb, mod, nw.reshape(1, D_MODEL), wo_a, wo_b, wr, br, wg, wu, wd, final_w.reshape(1, D_MODEL))


def _pad_cols(w, n):
    return jnp.pad(w, ((0, 0), (0, n - w.shape[1])))


def _router_weights(rg, re):
    w = _pad_cols(jnp.concatenate([rg, re], axis=1).astype(F32), ROUTER_LANES)
    hi = w.astype(BF16)
    lo = (w - hi.astype(F32)).astype(BF16)
    return jnp.concatenate([hi, hi, lo], axis=0)


def _mod_table(m):
    m = m.reshape(m.shape[0], 6, D_MODEL)
    return jnp.pad(m, ((0, 0), (0, MOD_ROWS - 6), (0, 0)))


def kernel(x_prompt, x_sample, c, c_ctx, state_dn, cache_b_k, cache_b_v, cache_c_k, cache_c_v, state_s5,
           ada_w, ada_b, norm_w, w_in_e, dn_conv_w, dn_a_log, dn_dt_bias, dn_norm_w, b_sink, w_out_e,
           w_in_o, c_q_norm, c_k_norm, s5_lam_re, s5_lam_im, s5_log_dt, s5_b_re, s5_b_im, s5_c_re, s5_c_im,
           s5_d, s5_glu_w, s5_glu_b, w_out_o, moe_rg, moe_rg_b, moe_re, moe_re_b, moe_wg, moe_wu, moe_wd,
           final_norm_w):
    rope_tabs = rope_tables()
    cvec = jnp.concatenate([c_ctx[None, :], c, jnp.zeros((ADA_ROWS - 1 - DEC_BATCH, D_MODEL), F32)], axis=0)
    mods = ada_pallas(cvec, ada_w, ada_b)
    xp = x_prompt.reshape(BATCH * SEQ, D_MODEL)
    xs = x_sample.reshape(DEC_BATCH * DEC_SEQ, D_MODEL)
    new_dn, new_bk, new_bv, new_ck, new_cv, new_s5 = [], [], [], [], [], []
    for layer in range(DEPTH):
        mod_p = _mod_table(mods[layer, 0:1])
        mod_s = _mod_table(mods[layer, 1:1 + DEC_BATCH])
        i = layer // 2
        if layer % 2 == 0:
            ep = {'conv_w': dn_conv_w[i], 'a_log': dn_a_log[i], 'dt_bias': dn_dt_bias[i],
                  'norm_w': dn_norm_w[i], 'sink': b_sink[i]}
            qkv_w, z_w, a_w, bb_w, bq_w, bk_w, bv_w = split_cols(w_in_e[i].astype(BF16), EVEN_SPLITS)
            plain_w = [qkv_w, z_w, _pad_cols(jnp.concatenate([a_w, bb_w], axis=1), LANES)]
            att_w = [bq_w, bk_w, bv_w]
            w_out = w_out_e[i].astype(BF16)
            wo_a, wo_b = w_out[:HA * DK], w_out[HA * DK:]
            proj_p = inproj_pallas(xp, mod_p, norm_w[layer, 0], plain_w, att_w, BATCH * SEQ, keep_kv=True)
            proj_s = inproj_pallas(xs, mod_s, norm_w[layer, 0], plain_w, att_w, DEC_SEQ, rope_tabs=rope_tabs)
            ya_p, yb_p, dn, bk, bv = even_mixer_context(proj_p, ep)
            ya_s, yb_s = even_mixer_latent(proj_s, ep, state_dn[:, i], cache_b_k[:, i], cache_b_v[:, i])
            new_dn.append(dn)
            new_bk.append(bk)
            new_bv.append(bv)
        else:
            op = {'q_norm': c_q_norm[i], 'k_norm': c_k_norm[i], 'lam_re': s5_lam_re[i],
                  'lam_im': s5_lam_im[i], 'log_dt': s5_log_dt[i], 'b_re': s5_b_re[i], 'b_im': s5_b_im[i],
                  'c_re': s5_c_re[i], 'c_im': s5_c_im[i], 'd': s5_d[i], 'glu_w': s5_glu_w[i],
                  'glu_b': s5_glu_b[i]}
            cq_w, ck_w, cv_w, u_w = split_cols(w_in_o[i].astype(BF16), ODD_SPLITS)
            norms = (op['q_norm'], op['k_norm'])
            w_out = w_out_o[i].astype(BF16)
            wo_a, wo_b = w_out[:HC * HEAD_DIM], w_out[HC * HEAD_DIM:]
            proj_p = inproj_pallas(xp, mod_p, norm_w[layer, 0], [u_w], [cq_w, ck_w, cv_w], BATCH * SEQ,
                                   norm_ws=norms, keep_kv=True)
            proj_s = inproj_pallas(xs, mod_s, norm_w[layer, 0], [u_w], [cq_w, ck_w, cv_w], DEC_SEQ,
                                   norm_ws=norms, rope_tabs=rope_tabs)
            ya_p, yb_p, ck, cv, s5 = odd_mixer_context(proj_p, op)
            ya_s, yb_s = odd_mixer_latent(proj_s, op, cache_c_k[:, i], cache_c_v[:, i], state_s5[:, i])
            new_ck.append(ck)
            new_cv.append(cv)
            new_s5.append(s5)
        wr = _router_weights(moe_rg[layer], moe_re[layer])
        br = _pad_cols(jnp.concatenate([moe_rg_b[layer], moe_re_b[layer]])[None, :], ROUTER_LANES)
        moe_w = (wo_a, wo_b, wr, br, moe_wg[layer].astype(BF16), moe_wu[layer].astype(BF16),
                 moe_wd[layer].astype(BF16))
        last = layer == DEPTH - 1
        xp = moe_pallas(xp, ya_p, yb_p, mod_p, norm_w[layer, 1], *moe_w, final_norm_w, BATCH * SEQ, last)
        xs = moe_pallas(xs, ya_s, yb_s, mod_s, norm_w[layer, 1], *moe_w, final_norm_w, DEC_SEQ, last)
    y_prompt = xp.reshape(x_prompt.shape)
    y_sample = xs.reshape(x_sample.shape)
    return (y_prompt, y_sample, jnp.stack(new_dn, axis=1), jnp.stack(new_bk, axis=1), jnp.stack(new_bv, axis=1),
            jnp.stack(new_ck, axis=1), jnp.stack(new_cv, axis=1), jnp.stack(new_s5, axis=1))
```

```python
import functools

import numpy as np
import jax
import jax.numpy as jnp
from jax import lax
from jax.experimental import pallas as pl
from jax.experimental.pallas import tpu as pltpu

D_MODEL = 1024
BATCH = 32
SEQ = 256
DEPTH = 2
DEC_BATCH = 8
DEC_SEQ = 2048
PAST_LEN = 256

GRID_W = 64
HEAD_DIM = 64
BLK = 128
ROPE_THETA = 10000.0
EPS = 1e-6
N_EVEN = (DEPTH + 1) // 2
N_ODD = DEPTH // 2
HA = D_MODEL // 128
DK = HEAD_DIM
CONV_W = 5
DN_CHUNK = 64
HB = D_MODEL // 128
HKV_B = HB // 4
WINDOW = 128
HC = (3 * D_MODEL // 4) // HEAD_DIM
HKV_C = HC // 3
D_S5 = D_MODEL // 4
S5_CH = 16
G_D = D_S5 // S5_CH
S5_P = 64
N_GROUPS = 4
EXP_PER_GROUP = 8
N_EXP = N_GROUPS * EXP_PER_GROUP
EXPERT_FF = D_MODEL // 8
TOP_K = 2

EVEN_SPLITS = (3 * HA * DK, HA * DK, 2 * HA, 2 * HA, HB * HEAD_DIM, HKV_B * HEAD_DIM, HKV_B * HEAD_DIM)
D_IN_EVEN = sum(EVEN_SPLITS)
D_MIX_EVEN = HA * DK + HB * HEAD_DIM
ODD_SPLITS = (HC * HEAD_DIM, HKV_C * HEAD_DIM, HKV_C * HEAD_DIM, D_S5)
D_IN_ODD = sum(ODD_SPLITS)
D_MIX_ODD = HC * HEAD_DIM + D_S5

F32 = jnp.float32
BF16 = jnp.bfloat16


def split_cols(x, sizes):
    idx = np.cumsum(sizes)[:-1].tolist()
    return jnp.split(x, idx, axis=-1)


def axial_rope_tables(rows):
    row = jnp.repeat(jnp.arange(rows), GRID_W).astype(F32)
    col = jnp.tile(jnp.arange(GRID_W), rows).astype(F32)
    quarter = HEAD_DIM // 4
    freqs = ROPE_THETA ** (-jnp.arange(quarter, dtype=F32) / quarter)
    ang_r = row[:, None] * freqs
    ang_c = col[:, None] * freqs
    ang = jnp.concatenate([ang_r, ang_r, ang_c, ang_c], axis=-1)
    return jnp.cos(ang), jnp.sin(ang)


SUBLANES = 8
VMEM_LIMIT = 56 * 1024 * 1024
S5_ROWS = 1024
S5_STATE = G_D * S5_P


def _s5_scan_kernel(u_ref, bmat_ref, cmat_ref, a_ref, s0_ref, y_ref, sfin_ref, xs_ref, st_ref, *, nblk, bsz):
    dr = pl.program_id(0)
    blk = pl.program_id(1)

    @pl.when(blk == 0)
    def _():
        st_ref[...] = s0_ref[0]

    xs_ref[...] = jnp.dot(u_ref[...].astype(BF16), bmat_ref[0], preferred_element_type=F32)
    a_re = jnp.broadcast_to(a_ref[0, 0:1, :], (SUBLANES, S5_STATE))
    a_im = jnp.broadcast_to(a_ref[0, 1:2, :], (SUBLANES, S5_STATE))
    steps = S5_ROWS // bsz
    for sg in range(bsz // SUBLANES):
        rows = pl.ds(sg * SUBLANES, SUBLANES)

        def body(i, carry, sg=sg):
            x_re, x_im = carry
            l = i + dr * (steps - 1 - 2 * i)
            r = pl.ds(pl.multiple_of(l * bsz + sg * SUBLANES, SUBLANES), SUBLANES)
            n_re = a_re * x_re - a_im * x_im + xs_ref[r, 0:S5_STATE]
            n_im = a_re * x_im + a_im * x_re + xs_ref[r, S5_STATE:2 * S5_STATE]
            xs_ref[r, 0:S5_STATE] = n_re
            xs_ref[r, S5_STATE:2 * S5_STATE] = n_im
            return n_re, n_im

        x_re, x_im = lax.fori_loop(0, steps, body, (st_ref[rows, 0:S5_STATE], st_ref[rows, S5_STATE:2 * S5_STATE]),
                                   unroll=4)
        st_ref[rows, 0:S5_STATE] = x_re
        st_ref[rows, S5_STATE:2 * S5_STATE] = x_im
    y_ref[0] = jnp.dot(xs_ref[...].astype(BF16), cmat_ref[0], preferred_element_type=F32)

    @pl.when(blk == nblk - 1)
    def _():
        sfin_ref[0] = st_ref[...]


def _s5_out_kernel(yf_ref, yb_ref, u_ref, d_ref, w_ref, b_ref, o_ref):
    y = yf_ref[0] + yb_ref[0] + u_ref[...] * d_ref[...]
    y = jax.nn.gelu(y)
    gate = jnp.dot(y.astype(BF16), w_ref[...], preferred_element_type=F32) + b_ref[...]
    o_ref[...] = y * jax.nn.sigmoid(gate)


def s5_operators(op):
    lam_re, lam_im = op['lam_re'].astype(F32), op['lam_im'].astype(F32)
    dt = jnp.exp(op['log_dt'].astype(F32))[..., None]
    mag = jnp.exp(lam_re * dt)
    a_re, a_im = mag * jnp.cos(lam_im * dt), mag * jnp.sin(lam_im * dt)
    den = lam_re * lam_re + lam_im * lam_im
    k_re = ((a_re - 1) * lam_re + a_im * lam_im) / den
    k_im = (a_im * lam_re - (a_re - 1) * lam_im) / den
    b_re, b_im = op['b_re'].astype(F32), op['b_im'].astype(F32)
    bb_re = k_re[..., None] * b_re - k_im[..., None] * b_im
    bb_im = k_re[..., None] * b_im + k_im[..., None] * b_re
    eye = jnp.eye(G_D, dtype=F32)

    def b_blocks(t):
        return jnp.einsum('dgpc,gh->dgchp', t, eye).reshape(2, D_S5, S5_STATE)

    def c_blocks(t):
        return jnp.einsum('dgcp,gh->dgphc', t, eye).reshape(2, S5_STATE, D_S5)

    b_blk = jnp.concatenate([b_blocks(bb_re), b_blocks(bb_im)], axis=-1).astype(BF16)
    c_blk = jnp.concatenate([c_blocks(op['c_re'].astype(F32)), -c_blocks(op['c_im'].astype(F32))],
                            axis=1).astype(BF16)
    a_vec = jnp.stack([a_re.reshape(2, S5_STATE), a_im.reshape(2, S5_STATE)], axis=1)
    return b_blk, c_blk, a_vec


def s5_pallas(u, op, ops, s0):
    b, l, _ = u.shape
    b_blk, c_blk, a_vec = ops
    n = b * l
    nblk = n // S5_ROWS
    u_t = jnp.swapaxes(u, 0, 1).reshape(n, D_S5)

    def rows_map(dr, blk):
        return (blk + dr * (nblk - 1 - 2 * blk), 0)

    y2, s_fin = pl.pallas_call(
        functools.partial(_s5_scan_kernel, nblk=nblk, bsz=b),
        grid=(2, nblk),
        in_specs=[
            pl.BlockSpec((S5_ROWS, D_S5), rows_map),
            pl.BlockSpec((1, D_S5, 2 * S5_STATE), lambda dr, blk: (dr, 0, 0)),
            pl.BlockSpec((1, 2 * S5_STATE, D_S5), lambda dr, blk: (dr, 0, 0)),
            pl.BlockSpec((1, 2, S5_STATE), lambda dr, blk: (dr, 0, 0)),
            pl.BlockSpec((1, b, 2 * S5_STATE), lambda dr, blk: (dr, 0, 0)),
        ],
        out_specs=[
            pl.BlockSpec((1, S5_ROWS, D_S5), lambda dr, blk: (dr,) + rows_map(dr, blk)),
            pl.BlockSpec((1, b, 2 * S5_STATE), lambda dr, blk: (dr, 0, 0)),
        ],
        out_shape=[jax.ShapeDtypeStruct((2, n, D_S5), F32), jax.ShapeDtypeStruct((2, b, 2 * S5_STATE), F32)],
        scratch_shapes=[pltpu.VMEM((S5_ROWS, 2 * S5_STATE), F32), pltpu.VMEM((b, 2 * S5_STATE), F32)],
        compiler_params=pltpu.CompilerParams(dimension_semantics=("arbitrary", "arbitrary"),
                                             vmem_limit_bytes=VMEM_LIMIT),
        name="s5_scan",
    )(u_t, b_blk, c_blk, a_vec, s0)
    tm = S5_ROWS
    y = pl.pallas_call(
        _s5_out_kernel,
        grid=(n // tm,),
        in_specs=[
            pl.BlockSpec((1, tm, D_S5), lambda i: (0, i, 0)),
            pl.BlockSpec((1, tm, D_S5), lambda i: (1, i, 0)),
            pl.BlockSpec((tm, D_S5), lambda i: (i, 0)),
            pl.BlockSpec((1, D_S5), lambda i: (0, 0)),
            pl.BlockSpec((D_S5, D_S5), lambda i: (0, 0)),
            pl.BlockSpec((1, D_S5), lambda i: (0, 0)),
        ],
        out_specs=pl.BlockSpec((tm, D_S5), lambda i: (i, 0)),
        out_shape=jax.ShapeDtypeStruct((n, D_S5), F32),
        name="s5_out",
    )(y2, y2, u_t, op['d'].astype(F32).reshape(1, D_S5), op['glu_w'].astype(BF16),
      op['glu_b'].astype(F32).reshape(1, D_S5))
    return jnp.swapaxes(y.reshape(l, b, D_S5), 0, 1), s_fin


LANES = 128
DN_PAIRS = HA * DK // LANES
NEG_BIG = -1e30
DN_UNROLL = 4


def _nt_dot(a, b):
    return lax.dot_general(a, b, (((1,), (1,)), ((), ())), preferred_element_type=F32)


def _bf16_dot(a, b):
    return jnp.dot(a.astype(BF16), b.astype(BF16), preferred_element_type=F32)


def _select_dot(x, sel):
    x1 = x.astype(BF16)
    r1 = x - x1.astype(F32)
    x2 = r1.astype(BF16)
    x3 = (r1 - x2.astype(F32)).astype(BF16)
    dot = functools.partial(jnp.dot, preferred_element_type=F32)
    return dot(x1, sel) + dot(x2, sel) + dot(x3, sel)


def _head_sums(x):
    lane = lax.broadcasted_iota(jnp.int32, x.shape, 1)
    lo = lane < HEAD_DIM
    s0 = jnp.sum(jnp.where(lo, x, 0.0), axis=-1, keepdims=True)
    s1 = jnp.sum(jnp.where(lo, 0.0, x), axis=-1, keepdims=True)
    return jnp.where(lo, s0, s1)


def _chunk_cumsum(x, reverse):
    n = x.shape[0]
    pos = lax.broadcasted_iota(jnp.int32, x.shape, 0) % DN_CHUNK
    s = 1
    while s < DN_CHUNK:
        if reverse:
            x = x + jnp.where(pos < DN_CHUNK - s, pltpu.roll(x, n - s, 0), 0.0)
        else:
            x = x + jnp.where(pos >= s, pltpu.roll(x, s, 0), 0.0)
        s *= 2
    return x


def _short_conv_silu(x, w):
    n = x.shape[0]
    row = lax.broadcasted_iota(jnp.int32, x.shape, 0)
    pad = CONV_W // 2
    y = x * w[pad:pad + 1, :]
    for t in range(CONV_W):
        s = t - pad
        if s == 0:
            continue
        shifted = pltpu.roll(x, (-s) % n, 0)
        valid = (row + s >= 0) & (row + s < n)
        y = y + jnp.where(valid, shifted, 0.0) * w[t:t + 1, :]
    return jax.nn.silu(y)


def _unit_tri_solve(ms, rhss, ri, ci):
    n = rhss[0].shape[-1]
    same16 = (ri // 16) == (ci // 16)
    dot = functools.partial(jnp.dot, preferred_element_type=F32)

    def apply(p, r):
        hi = r.astype(BF16)
        lo = (r - hi.astype(F32)).astype(BF16)
        x = dot(p, jnp.concatenate([hi, lo], axis=-1))
        return x[:, 0:n] + x[:, n:2 * n]

    diag = [jnp.where(same16, m, 0.0) for m in ms]
    rest = [jnp.where(same16, 0.0, m).astype(BF16) for m in ms]
    pows = [d.astype(BF16) for d in diag]
    tinv = [jnp.where(ri == ci, 1.0, 0.0) - d for d in diag]
    for _ in range(3):
        pows = [dot(p, p).astype(BF16) for p in pows]
        tinv = [t + dot(t.astype(BF16), p) for t, p in zip(tinv, pows)]
    tinv = [t.astype(BF16) for t in tinv]
    p1 = [dot(t, e).astype(BF16) for t, e in zip(tinv, rest)]
    sols = [apply(t, r) for t, r in zip(tinv, rhss)]
    p2 = [dot(p, p).astype(BF16) for p in p1]
    sols = [s - apply(p, s) for p, s in zip(p1, sols)]
    return [s + apply(p, s) for p, s in zip(p2, sols)]


def _deltanet_kernel(q_ref, k_ref, v_ref, z_ref, a_ref, bb_ref, cwq_ref, cwk_ref, cwv_ref, alog_ref, dtb_ref, nw_ref,
                     s0_ref, y_ref, sfin_ref,
                     qs_ref, ks_ref, vs_ref, o_ref, gc_ref, beta_ref, u_ref, w_ref, attn_ref, qp_ref, kpt_ref,
                     etot_ref, *, seq):
    hp = pl.program_id(1)
    nchunk = seq // DN_CHUNK
    c64 = DN_CHUNK
    head_lanes = [slice(j * HEAD_DIM, (j + 1) * HEAD_DIM) for j in range(2)]

    q = _short_conv_silu(q_ref[0], cwq_ref[...])
    q = q * lax.rsqrt(_head_sums(q * q) + EPS) * (DK ** -0.5)
    k = _short_conv_silu(k_ref[0], cwk_ref[...])
    k = k * lax.rsqrt(_head_sums(k * k) + EPS)
    v = _short_conv_silu(v_ref[0], cwv_ref[...])
    qs_ref[...] = q
    ks_ref[...] = k
    vs_ref[...] = v
    o_ref[...] = jnp.zeros_like(o_ref)

    g_col = -jnp.exp(alog_ref[...]) * jax.nn.softplus(a_ref[0] + dtb_ref[...])
    b_col = jax.nn.sigmoid(bb_ref[0])
    sel_row = lax.broadcasted_iota(jnp.int32, (2 * HA, 2 * LANES), 0)
    sel_lane = lax.broadcasted_iota(jnp.int32, (2 * HA, 2 * LANES), 1)
    sel = (sel_row == (sel_lane // LANES) * HA + 2 * hp + (sel_lane // HEAD_DIM) % 2).astype(BF16)
    g_b = _select_dot(g_col, sel)
    gc_ref[:, 0:LANES] = _chunk_cumsum(g_b[:, 0:LANES], reverse=False)
    gc_ref[:, LANES:2 * LANES] = _chunk_cumsum(g_b[:, LANES:2 * LANES], reverse=True)
    beta_ref[...] = _select_dot(b_col, sel)

    ri = lax.broadcasted_iota(jnp.int32, (c64, c64), 0)
    ci = lax.broadcasted_iota(jnp.int32, (c64, c64), 1)

    def intra_chunk(step, carry):
        chunk_ids = [step * DN_UNROLL + t for t in range(DN_UNROLL)]
        rows_c = [pl.ds(pl.multiple_of(c * c64, c64), c64) for c in chunk_ids]
        loads = [(qs_ref[r, :], ks_ref[r, :], vs_ref[r, :], gc_ref[r, :], beta_ref[r, :]) for r in rows_c]
        heads = [(t, j, ld[0][:, head_lanes[j]], ld[1][:, head_lanes[j]], ld[2][:, head_lanes[j]])
                 for t, ld in enumerate(loads) for j in range(2)]
        gate_lanes = [slice((c % 2) * LANES + (c // 2) * HEAD_DIM, (c % 2) * LANES + (c // 2 + 1) * HEAD_DIM)
                      for c in range(4)]
        gates = {(t, c): (ld[3][:, gate_lanes[c]], ld[4][:, gate_lanes[c]])
                 for t, ld in enumerate(loads) for c in range(4)}
        kbs = [h[3].astype(BF16) for h in heads]
        kks = [_nt_dot(kb, kb) for kb in kbs]
        qks = [_nt_dot(h[2].astype(BF16), kb) for h, kb in zip(heads, kbs)]
        chains, ms, rhss = [], [], []
        for (t, j, qj, kj, vj), kk, qk in zip(heads, kks, qks):
            for d in range(2):
                gcb, bet = gates[(t, 2 * j + d)]
                diff = gcb - gcb.T
                earlier = (ri >= ci) if d == 0 else (ri <= ci)
                strict = (ri > ci) if d == 0 else (ri < ci)
                dec = jnp.exp(jnp.where(earlier, diff, NEG_BIG))
                eg = jnp.exp(gcb)
                gtot = gcb[c64 - 1:c64, :] if d == 0 else gcb[0:1, :]
                ms.append(jnp.where(strict, kk * bet * dec, 0.0))
                rhss.append(jnp.concatenate([vj * bet, kj * bet * eg], axis=-1))
                chains.append((t, 2 * j + d, qk * dec, qj * eg, (kj * jnp.exp(gtot - gcb)).T,
                               jnp.broadcast_to(jnp.exp(gtot), (SUBLANES, HEAD_DIM))))
        uws = _unit_tri_solve(ms, rhss, ri, ci)
        for (t, c, attn, qp, kpt, etot), uw in zip(chains, uws):
            u_ref[c, rows_c[t], :] = uw[:, 0:HEAD_DIM]
            w_ref[c, rows_c[t], :] = uw[:, HEAD_DIM:2 * HEAD_DIM].astype(BF16)
            attn_ref[c, rows_c[t], :] = attn.astype(BF16)
            qp_ref[c, rows_c[t], :] = qp.astype(BF16)
            kpt_ref[c, rows_c[t], :] = kpt.astype(BF16)
            etot_ref[c, pl.ds(pl.multiple_of(chunk_ids[t] * SUBLANES, SUBLANES), SUBLANES), :] = etot
        return carry

    lax.fori_loop(0, nchunk // DN_UNROLL, intra_chunk, 0)

    def inter_chunk(i, states):
        rows_d = [pl.ds(pl.multiple_of(cc * c64, c64), c64) for cc in (i, nchunk - 1 - i)]
        erow_d = [pl.ds(pl.multiple_of(cc * SUBLANES, SUBLANES), SUBLANES) for cc in (i, nchunk - 1 - i)]
        loaded = [(w_ref[c, rows_d[c % 2], :], qp_ref[c, rows_d[c % 2], :], u_ref[c, rows_d[c % 2], :],
                   attn_ref[c, rows_d[c % 2], :], kpt_ref[c, rows_d[c % 2], :], etot_ref[c, erow_d[c % 2], :],
                   o_ref[c // 2, rows_d[c % 2], :]) for c in range(4)]
        ws_qs = [jnp.dot(jnp.concatenate([ld[0], ld[1]], axis=0), s.astype(BF16), preferred_element_type=F32)
                 for ld, s in zip(loaded, states)]
        v_new = [(ld[2] - r[0:c64]).astype(BF16) for ld, r in zip(loaded, ws_qs)]
        av = [jnp.dot(ld[3], vn, preferred_element_type=F32) for ld, vn in zip(loaded, v_new)]
        kv = [jnp.dot(ld[4], vn, preferred_element_type=F32) for ld, vn in zip(loaded, v_new)]
        for c in range(4):
            o_ref[c // 2, rows_d[c % 2], :] = loaded[c][6] + ws_qs[c][c64:2 * c64] + av[c]
        return tuple(s * ld[5][0:1, :] + x for s, ld, x in zip(states, loaded, kv))

    init = tuple(s0_ref[0, c % 2, c // 2] for c in range(4))
    fin = lax.fori_loop(0, nchunk, inter_chunk, init)
    for c in range(4):
        sfin_ref[0, c % 2, c // 2] = fin[c]

    o = jnp.concatenate([o_ref[0], o_ref[1]], axis=-1)
    o = o * lax.rsqrt(_head_sums(o * o) * (1.0 / DK) + EPS) * nw_ref[...]
    y_ref[0] = o * jax.nn.silu(z_ref[0])


def deltanet_pallas(qkv, z, a, bb, ep, s0):
    b, l, _ = qkv.shape
    npair = DN_PAIRS

    def col_spec(off):
        return pl.BlockSpec((1, l, LANES), lambda i, hp: (i, 0, off + hp))

    def cw_spec(off):
        return pl.BlockSpec((CONV_W, LANES), lambda i, hp: (0, off + hp))

    def full(shape):
        return pl.BlockSpec(shape, lambda i, hp: (0,) * len(shape))

    state_spec = pl.BlockSpec((1, 2, 2, DK, DK), lambda i, hp: (i, 0, hp, 0, 0))
    row_f32 = pltpu.VMEM((l, LANES), F32)
    gate_f32 = pltpu.VMEM((l, 2 * LANES), F32)
    chain_f32 = pltpu.VMEM((4, l, HEAD_DIM), F32)
    chain_bf16 = pltpu.VMEM((4, l, HEAD_DIM), BF16)
    y, s_fin = pl.pallas_call(
        functools.partial(_deltanet_kernel, seq=l),
        grid=(b, npair),
        in_specs=[col_spec(0), col_spec(npair), col_spec(2 * npair), col_spec(0),
                  pl.BlockSpec((1, l, 2 * HA), lambda i, hp: (i, 0, 0)),
                  pl.BlockSpec((1, l, 2 * HA), lambda i, hp: (i, 0, 0)),
                  cw_spec(0), cw_spec(npair), cw_spec(2 * npair),
                  full((1, 2 * HA)), full((1, 2 * HA)), full((1, LANES)), state_spec],
        out_specs=[col_spec(0), state_spec],
        out_shape=[jax.ShapeDtypeStruct((b, l, HA * DK), F32), jax.ShapeDtypeStruct((b, 2, HA, DK, DK), F32)],
        scratch_shapes=[row_f32, row_f32, row_f32, pltpu.VMEM((2, l, HEAD_DIM), F32), gate_f32, gate_f32,
                        chain_f32, chain_bf16, chain_bf16, chain_bf16, chain_bf16,
                        pltpu.VMEM((4, l // DN_CHUNK * SUBLANES, HEAD_DIM), F32)],
        compiler_params=pltpu.CompilerParams(dimension_semantics=("arbitrary", "arbitrary"),
                                             vmem_limit_bytes=VMEM_LIMIT),
        name="deltanet",
    )(qkv, qkv, qkv, z, a, bb, ep['conv_w'], ep['conv_w'], ep['conv_w'],
      ep['a_log'].astype(F32).reshape(1, 2 * HA), ep['dt_bias'].astype(F32).reshape(1, 2 * HA),
      jnp.tile(ep['norm_w'].astype(F32), 2).reshape(1, LANES), s0)
    return y, s_fin


TM = 512
ATT_TQ = BLK
ROT = HEAD_DIM // 4


def rope_tables():
    cos, sin = axial_rope_tables(DEC_SEQ // GRID_W)
    cos2 = jnp.tile(cos, (1, LANES // HEAD_DIM))
    sin2 = jnp.tile(sin, (1, LANES // HEAD_DIM))
    first = (jnp.arange(LANES) % (2 * ROT)) < ROT
    return cos2, jnp.where(first, -sin2, 0.0), jnp.where(first, 0.0, sin2)


def _head_prep(x, w_ref, tabs, scale):
    outs, normed = [], []
    for c in range(x.shape[1] // LANES):
        xc = x[:, c * LANES:(c + 1) * LANES]
        if w_ref is not None:
            xc = xc * lax.rsqrt(_head_sums(xc * xc) * (1.0 / HEAD_DIM) + EPS) * w_ref[...]
            normed.append(xc)
        if tabs is not None:
            cos_ref, sup_ref, sdn_ref = tabs
            xc = (xc * cos_ref[...] + pltpu.roll(xc, LANES - ROT, 1) * sup_ref[...]
                  + pltpu.roll(xc, ROT, 1) * sdn_ref[...])
        outs.append((xc * scale).astype(BF16))
    cat = lambda parts: jnp.concatenate(parts, axis=-1) if len(parts) > 1 else parts[0]
    return cat(outs), (cat(normed) if w_ref is not None else None)


def _attn_kernel(q_ref, k_ref, v_ref, *refs, grp, hk, windowed, has_ctx, has_sink):
    refs = list(refs)
    kc_ref, vc_ref = (refs.pop(0), refs.pop(0)) if has_ctx else (None, None)
    sink_ref = refs.pop(0) if has_sink else None
    o_ref = refs[0]
    tq = q_ref.shape[1]
    seq_k = k_ref.shape[1]
    if windowed:
        qi = pl.program_id(1)
        start = jnp.clip((qi - 1) * tq, 0, seq_k - 3 * tq)
        krows = pl.ds(pl.multiple_of(start, tq), 3 * tq)
        qpos = qi * tq + lax.broadcasted_iota(jnp.int32, (grp * tq, 3 * tq), 0) % tq
        kpos = start + lax.broadcasted_iota(jnp.int32, (grp * tq, 3 * tq), 1)
        keep = jnp.abs(qpos - kpos) <= WINDOW
    else:
        krows = pl.ds(0, seq_k)
    q = q_ref[0]
    groups = range(hk)
    lanes = [slice(g * HEAD_DIM, (g + 1) * HEAD_DIM) for g in groups]
    qgs = [jnp.concatenate([q[:, (g * grp + j) * HEAD_DIM:(g * grp + j + 1) * HEAD_DIM] for j in range(grp)], axis=0)
           for g in groups]
    ss = [_nt_dot(qg, k_ref[0, krows, gl]) for qg, gl in zip(qgs, lanes)]
    if windowed:
        ss = [jnp.where(keep, s, NEG_BIG) for s in ss]
    ms = [jnp.max(s, axis=-1, keepdims=True) for s in ss]
    if has_ctx:
        kc = kc_ref[0].astype(BF16)
        scs = [_nt_dot(qg, kc[:, gl]) for qg, gl in zip(qgs, lanes)]
        ms = [jnp.maximum(m, jnp.max(sc, axis=-1, keepdims=True)) for m, sc in zip(ms, scs)]
    if has_sink:
        head_rows = [slice(j * tq, (j + 1) * tq) for j in range(grp)]
        ms = [jnp.concatenate([jnp.maximum(m[head_rows[j]], sink_ref[g * grp + j]) for j in range(grp)], axis=0)
              for g, m in zip(groups, ms)]
    ps = [jnp.exp(s - m) for s, m in zip(ss, ms)]
    dens = [jnp.sum(p, axis=-1, keepdims=True) for p in ps]
    os_ = [jnp.dot(p.astype(BF16), v_ref[0, krows, gl], preferred_element_type=F32) for p, gl in zip(ps, lanes)]
    if has_ctx:
        vc = vc_ref[0].astype(BF16)
        pcs = [jnp.exp(sc - m) for sc, m in zip(scs, ms)]
        dens = [den + jnp.sum(pc, axis=-1, keepdims=True) for den, pc in zip(dens, pcs)]
        os_ = [o + jnp.dot(pc.astype(BF16), vc[:, gl], preferred_element_type=F32)
               for o, pc, gl in zip(os_, pcs, lanes)]
    if has_sink:
        dens = [den + jnp.concatenate([jnp.exp(sink_ref[g * grp + j] - m[head_rows[j]]) for j in range(grp)], axis=0)
                for g, den, m in zip(groups, dens, ms)]
    outs = [None] * (grp * hk)
    for g in groups:
        o = os_[g] / dens[g]
        for j in range(grp):
            outs[g * grp + j] = o[j * tq:(j + 1) * tq]
    o_ref[0] = jnp.concatenate(outs, axis=-1)


def attention_pallas(q, k, v, ctx=None, sink=None, windowed=False):
    b, lq, hd = q.shape
    lk, kd = k.shape[1], k.shape[2]
    hk = kd // HEAD_DIM
    grp = hd // kd
    tq = ATT_TQ if windowed or lq > 2 * ATT_TQ else lq
    ins = [q, k, v]
    in_specs = [pl.BlockSpec((1, tq, hd), lambda i, t: (i, t, 0)),
                pl.BlockSpec((1, lk, kd), lambda i, t: (i, 0, 0)),
                pl.BlockSpec((1, lk, kd), lambda i, t: (i, 0, 0))]
    if ctx is not None:
        ins += list(ctx)
        in_specs += [pl.BlockSpec((1, ctx[0].shape[1], kd), lambda i, t: (i, 0, 0))] * 2
    if sink is not None:
        ins.append(sink.astype(F32))
        in_specs.append(pl.BlockSpec(memory_space=pltpu.SMEM))
    return pl.pallas_call(
        functools.partial(_attn_kernel, grp=grp, hk=hk, windowed=windowed, has_ctx=ctx is not None,
                          has_sink=sink is not None),
        grid=(b, lq // tq),
        in_specs=in_specs,
        out_specs=pl.BlockSpec((1, tq, hd), lambda i, t: (i, t, 0)),
        out_shape=jax.ShapeDtypeStruct((b, lq, hd), F32),
        compiler_params=pltpu.CompilerParams(dimension_semantics=("arbitrary", "arbitrary"),
                                             vmem_limit_bytes=VMEM_LIMIT),
        name="attention",
    )(*ins)


def even_mixer_context(proj, ep):
    qkv, z, ab, qh, kh, vh, bk, bv = proj
    qkv, z, ab = [t.reshape(BATCH, SEQ, -1) for t in (qkv, z, ab)]
    a, bb = ab[..., 0:2 * HA], ab[..., 2 * HA:4 * HA]
    y_a, dn_fin = deltanet_pallas(qkv, z, a, bb, ep, jnp.zeros((BATCH, 2, HA, DK, DK), F32))
    y_b = attention_pallas(*[t.reshape(BATCH, SEQ, -1) for t in (qh, kh, vh)], sink=ep['sink'])
    return (y_a.reshape(-1, HA * DK), y_b.reshape(-1, HB * HEAD_DIM), dn_fin,
            bk.reshape(BATCH, SEQ, HKV_B, HEAD_DIM), bv.reshape(BATCH, SEQ, HKV_B, HEAD_DIM))


def even_mixer_latent(proj, ep, dn_state, ctx_k, ctx_v):
    qkv, z, ab, qh, kh, vh = proj
    qkv, z, ab = [t.reshape(DEC_BATCH, DEC_SEQ, -1) for t in (qkv, z, ab)]
    a, bb = ab[..., 0:2 * HA], ab[..., 2 * HA:4 * HA]
    y_a, _ = deltanet_pallas(qkv, z, a, bb, ep, dn_state.astype(F32))
    ctx = (ctx_k.reshape(DEC_BATCH, PAST_LEN, -1), ctx_v.reshape(DEC_BATCH, PAST_LEN, -1))
    y_b = attention_pallas(*[t.reshape(DEC_BATCH, DEC_SEQ, -1) for t in (qh, kh, vh)], ctx=ctx, sink=ep['sink'],
                           windowed=True)
    return y_a.reshape(-1, HA * DK), y_b.reshape(-1, HB * HEAD_DIM)


def odd_mixer_context(proj, op):
    u, qh, kh, vh, k_normed, cv = proj
    y_c = attention_pallas(*[t.reshape(BATCH, SEQ, -1) for t in (qh, kh, vh)])
    y_d, s_fin = s5_pallas(u.reshape(BATCH, SEQ, -1), op, s5_operators(op), jnp.zeros((2, BATCH, 2 * S5_STATE), F32))
    s_fin = jnp.transpose(s_fin.reshape(2, BATCH, 2, G_D, S5_P), (1, 0, 3, 4, 2))
    return (y_c.reshape(-1, HC * HEAD_DIM), y_d.reshape(-1, D_S5), k_normed.reshape(BATCH, SEQ, HKV_C, HEAD_DIM),
            cv.reshape(BATCH, SEQ, HKV_C, HEAD_DIM), s_fin)


def odd_mixer_latent(proj, op, ctx_k, ctx_v, s5_state):
    u, qh, kh, vh = proj
    ctx = (ctx_k.reshape(DEC_BATCH, PAST_LEN, -1), ctx_v.reshape(DEC_BATCH, PAST_LEN, -1))
    y_c = attention_pallas(*[t.reshape(DEC_BATCH, DEC_SEQ, -1) for t in (qh, kh, vh)], ctx=ctx)
    s0 = jnp.transpose(s5_state.astype(F32), (1, 0, 4, 2, 3)).reshape(2, DEC_BATCH, 2 * S5_STATE)
    y_d, _ = s5_pallas(u.reshape(DEC_BATCH, DEC_SEQ, -1), op, s5_operators(op), s0)
    return y_c.reshape(-1, HC * HEAD_DIM), y_d.reshape(-1, D_S5)


ADA_ROWS = 16
ADA_TN = 1024
MOD_ROWS = 8
ROUTER_LANES = LANES
MOE_FCHUNK = EXP_PER_GROUP * EXPERT_FF


def _ada_kernel(c_ref, w_ref, b_ref, o_ref):
    o_ref[0] = _bf16_dot(jax.nn.silu(c_ref[...]), w_ref[0]) + b_ref[0]


def ada_pallas(cvec, ada_w, ada_b):
    n = 6 * D_MODEL
    return pl.pallas_call(
        _ada_kernel,
        grid=(DEPTH, n // ADA_TN),
        in_specs=[pl.BlockSpec((ADA_ROWS, D_MODEL), lambda l, j: (0, 0)),
                  pl.BlockSpec((1, D_MODEL, ADA_TN), lambda l, j: (l, 0, j)),
                  pl.BlockSpec((1, 1, ADA_TN), lambda l, j: (l, 0, j))],
        out_specs=pl.BlockSpec((1, ADA_ROWS, ADA_TN), lambda l, j: (l, 0, j)),
        out_shape=jax.ShapeDtypeStruct((DEPTH, ADA_ROWS, n), F32),
        name="ada_modulation",
    )(cvec, ada_w, ada_b.reshape(DEPTH, 1, n))


def _modulated(x, nw, shift, scale):
    return x * lax.rsqrt(jnp.mean(x * x, axis=-1, keepdims=True) + EPS) * nw * (1 + scale) + shift


def _inproj_kernel(x_ref, mod_ref, nw_ref, *refs, n_plain, norm, rope, keep_k, keep_v):
    refs = list(refs)
    m = mod_ref[0]
    h = _modulated(x_ref[...], nw_ref[...], m[0:1], m[1:2]).astype(BF16)
    w_refs = [refs.pop(0) for _ in range(n_plain + 3)]
    qw_ref, kw_ref = (refs.pop(0), refs.pop(0)) if norm else (None, None)
    tabs = [refs.pop(0) for _ in range(3)] if rope else None
    plain_refs = [refs.pop(0) for _ in range(n_plain)]
    qo_ref, ko_ref, vo_ref = refs.pop(0), refs.pop(0), refs.pop(0)
    for w_ref, o_ref in zip(w_refs, plain_refs):
        o_ref[...] = jnp.dot(h, w_ref[...], preferred_element_type=F32)
    q, k, v = [jnp.dot(h, w_ref[...], preferred_element_type=F32) for w_ref in w_refs[n_plain:]]
    qo_ref[...] = _head_prep(q, qw_ref, tabs, HEAD_DIM ** -0.5)[0]
    ko, k_normed = _head_prep(k, kw_ref, tabs, 1.0)
    ko_ref[...] = ko
    vo_ref[...] = v.astype(BF16)
    if keep_k:
        refs.pop(0)[...] = k_normed if norm else k
    if keep_v:
        refs.pop(0)[...] = v


def inproj_pallas(x, mod, nw, plain_ws, qkv_ws, rows_per_mod, norm_ws=None, rope_tabs=None, keep_kv=False):
    t = x.shape[0]
    tiles_per_mod = rows_per_mod // TM
    row = lambda i: (i, 0)
    const = lambda i: (0, 0)
    weights = list(plain_ws) + list(qkv_ws)
    ins = [x, mod, nw.reshape(1, D_MODEL)] + weights
    in_specs = [pl.BlockSpec((TM, D_MODEL), row),
                pl.BlockSpec((1, MOD_ROWS, D_MODEL), lambda i: (i // tiles_per_mod, 0, 0)),
                pl.BlockSpec((1, D_MODEL), const)] + [pl.BlockSpec(w.shape, const) for w in weights]
    if norm_ws is not None:
        ins += [jnp.tile(w.astype(F32), LANES // HEAD_DIM).reshape(1, LANES) for w in norm_ws]
        in_specs += [pl.BlockSpec((1, LANES), const)] * 2
    if rope_tabs is not None:
        tiles = DEC_SEQ // TM
        ins += list(rope_tabs)
        in_specs += [pl.BlockSpec((TM, LANES), lambda i: (i % tiles, 0))] * 3
    outs = [(w.shape[1], F32) for w in plain_ws] + [(w.shape[1], BF16) for w in qkv_ws]
    if keep_kv:
        outs += [(qkv_ws[1].shape[1], F32), (qkv_ws[2].shape[1], F32)]
    return pl.pallas_call(
        functools.partial(_inproj_kernel, n_plain=len(plain_ws), norm=norm_ws is not None,
                          rope=rope_tabs is not None, keep_k=keep_kv, keep_v=keep_kv),
        grid=(t // TM,),
        in_specs=in_specs,
        out_specs=[pl.BlockSpec((TM, n), row) for n, _ in outs],
        out_shape=[jax.ShapeDtypeStruct((t, n), dt) for n, dt in outs],
        compiler_params=pltpu.CompilerParams(dimension_semantics=("arbitrary",), vmem_limit_bytes=VMEM_LIMIT),
        name="modulated_in_proj",
    )(*ins)


def _moe_kernel(x_ref, ya_ref, yb_ref, mod_ref, nw_ref, woa_ref, wob_ref, wr_ref, br_ref, wg_ref, wu_ref, wd_ref,
                fw_ref, o_ref, x1_ref, h_ref, comb_ref, acc_ref, *, final):
    f = pl.program_id(1)
    m = mod_ref[0]

    @pl.when(f == 0)
    def _():
        y = _bf16_dot(ya_ref[...], woa_ref[...]) + _bf16_dot(yb_ref[...], wob_ref[...])
        x1 = x_ref[...] + m[2:3] * y
        x1_ref[...] = x1
        h = _modulated(x1, nw_ref[...], m[3:4], m[4:5])
        h_hi = h.astype(BF16)
        h_ref[...] = h_hi
        h_lo = (h - h_hi.astype(F32)).astype(BF16)
        logits = (jnp.dot(jnp.concatenate([h_hi, h_lo], axis=-1), wr_ref[0:2 * D_MODEL, :], preferred_element_type=F32)
                  + jnp.dot(h_hi, wr_ref[2 * D_MODEL:3 * D_MODEL, :], preferred_element_type=F32) + br_ref[...])
        lane = lax.broadcasted_iota(jnp.int32, logits.shape, 1)
        lane_f = lane.astype(F32)
        is_group = lane < N_GROUPS
        lg = jnp.where(is_group, logits, -jnp.inf)
        g_max = jnp.max(lg, axis=-1, keepdims=True)
        g_idx = jnp.min(jnp.where(lg == g_max, lane_f, float(ROUTER_LANES)), axis=-1, keepdims=True)
        gate_g = 1.0 / jnp.sum(jnp.where(is_group, jnp.exp(logits - g_max), 0.0), axis=-1, keepdims=True)
        lane_group = ((lane + (EXP_PER_GROUP - N_GROUPS)) // EXP_PER_GROUP - 1).astype(F32)
        in_group = (lane >= N_GROUPS) & (lane < N_GROUPS + N_EXP) & (lane_group == g_idx)
        le = jnp.where(in_group, logits, -jnp.inf)
        v1 = jnp.max(le, axis=-1, keepdims=True)
        i1 = jnp.min(jnp.where(le == v1, lane_f, float(ROUTER_LANES)), axis=-1, keepdims=True)
        le2 = jnp.where(lane_f == i1, -jnp.inf, le)
        v2 = jnp.max(le2, axis=-1, keepdims=True)
        i2 = jnp.min(jnp.where(le2 == v2, lane_f, float(ROUTER_LANES)), axis=-1, keepdims=True)
        e2 = jnp.exp(v2 - v1)
        p1 = gate_g / (1.0 + e2)
        comb = jnp.where(lane_f == i1, p1, 0.0) + jnp.where(lane_f == i2, p1 * e2, 0.0)
        for grp in range(N_GROUPS):
            comb_ref[grp] = pltpu.roll(comb, ROUTER_LANES - (N_GROUPS + grp * EXP_PER_GROUP), 1)
        acc_ref[...] = jnp.zeros_like(acc_ref)

    h = h_ref[...]
    hid = jax.nn.silu(jnp.dot(h, wg_ref[...], preferred_element_type=F32)) * jnp.dot(
        h, wu_ref[...], preferred_element_type=F32)
    comb = comb_ref[f]
    hid = jnp.concatenate([hid[:, k * EXPERT_FF:(k + 1) * EXPERT_FF] * comb[:, k:k + 1]
                           for k in range(EXP_PER_GROUP)], axis=-1)
    acc_ref[...] += jnp.dot(hid.astype(BF16), wd_ref[...], preferred_element_type=F32)

    @pl.when(f == N_GROUPS - 1)
    def _():
        out = x1_ref[...] + m[5:6] * acc_ref[...]
        if final:
            out = out * lax.rsqrt(jnp.mean(out * out, axis=-1, keepdims=True) + EPS) * fw_ref[...]
        o_ref[...] = out


def moe_pallas(x, ya, yb, mod, nw, wo_a, wo_b, wr, br, wg, wu, wd, final_w, rows_per_mod, final):
    t = x.shape[0]
    tiles_per_mod = rows_per_mod // TM
    row = lambda i, f: (i, 0)
    const = lambda i, f: (0, 0)
    return pl.pallas_call(
        functools.partial(_moe_kernel, final=final),
        grid=(t // TM, N_GROUPS),
        in_specs=[pl.BlockSpec((TM, D_MODEL), row),
                  pl.BlockSpec((TM, ya.shape[1]), row),
                  pl.BlockSpec((TM, yb.shape[1]), row),
                  pl.BlockSpec((1, MOD_ROWS, D_MODEL), lambda i, f: (i // tiles_per_mod, 0, 0)),
                  pl.BlockSpec((1, D_MODEL), const),
                  pl.BlockSpec(wo_a.shape, const),
                  pl.BlockSpec(wo_b.shape, const),
                  pl.BlockSpec((3 * D_MODEL, ROUTER_LANES), const),
                  pl.BlockSpec((1, ROUTER_LANES), const),
                  pl.BlockSpec((D_MODEL, MOE_FCHUNK), lambda i, f: (0, f)),
                  pl.BlockSpec((D_MODEL, MOE_FCHUNK), lambda i, f: (0, f)),
                  pl.BlockSpec((MOE_FCHUNK, D_MODEL), lambda i, f: (f, 0)),
                  pl.BlockSpec((1, D_MODEL), const)],
        out_specs=pl.BlockSpec((TM, D_MODEL), row),
        out_shape=jax.ShapeDtypeStruct((t, D_MODEL), F32),
        scratch_shapes=[pltpu.VMEM((TM, D_MODEL), F32), pltpu.VMEM((TM, D_MODEL), BF16),
                        pltpu.VMEM((N_GROUPS, TM, ROUTER_LANES), F32), pltpu.VMEM((TM, D_MODEL), F32)],
        compiler_params=pltpu.CompilerParams(dimension_semantics=("arbitrary", "arbitrary"),
                                             vmem_limit_bytes=VMEM_LIMIT),
        name="out_proj_moe",
    )(x, ya, yb, mod, nw.reshape(1, D_MODEL), wo_a, wo_b, wr, br, wg, wu, wd, final_w.reshape(1, D_MODEL))


def _pad_cols(w, n):
    return jnp.pad(w, ((0, 0), (0, n - w.shape[1])))


def _router_weights(rg, re):
    w = _pad_cols(jnp.concatenate([rg, re], axis=1).astype(F32), ROUTER_LANES)
    hi = w.astype(BF16)
    lo = (w - hi.astype(F32)).astype(BF16)
    return jnp.concatenate([hi, hi, lo], axis=0)


def _mod_table(m):
    m = m.reshape(m.shape[0], 6, D_MODEL)
    return jnp.pad(m, ((0, 0), (0, MOD_ROWS - 6), (0, 0)))


def kernel(x_prompt, x_sample, c, c_ctx, state_dn, cache_b_k, cache_b_v, cache_c_k, cache_c_v, state_s5,
           ada_w, ada_b, norm_w, w_in_e, dn_conv_w, dn_a_log, dn_dt_bias, dn_norm_w, b_sink, w_out_e,
           w_in_o, c_q_norm, c_k_norm, s5_lam_re, s5_lam_im, s5_log_dt, s5_b_re, s5_b_im, s5_c_re, s5_c_im,
           s5_d, s5_glu_w, s5_glu_b, w_out_o, moe_rg, moe_rg_b, moe_re, moe_re_b, moe_wg, moe_wu, moe_wd,
           final_norm_w):
    rope_tabs = rope_tables()
    cvec = jnp.concatenate([c_ctx[None, :], c, jnp.zeros((ADA_ROWS - 1 - DEC_BATCH, D_MODEL), F32)], axis=0)
    mods = ada_pallas(cvec, ada_w, ada_b)
    xp = x_prompt.reshape(BATCH * SEQ, D_MODEL)
    xs = x_sample.reshape(DEC_BATCH * DEC_SEQ, D_MODEL)
    new_dn, new_bk, new_bv, new_ck, new_cv, new_s5 = [], [], [], [], [], []
    for layer in range(DEPTH):
        mod_p = _mod_table(mods[layer, 0:1])
        mod_s = _mod_table(mods[layer, 1:1 + DEC_BATCH])
        i = layer // 2
        if layer % 2 == 0:
            ep = {'conv_w': dn_conv_w[i], 'a_log': dn_a_log[i], 'dt_bias': dn_dt_bias[i],
                  'norm_w': dn_norm_w[i], 'sink': b_sink[i]}
            qkv_w, z_w, a_w, bb_w, bq_w, bk_w, bv_w = split_cols(w_in_e[i].astype(BF16), EVEN_SPLITS)
            plain_w = [qkv_w, z_w, _pad_cols(jnp.concatenate([a_w, bb_w], axis=1), LANES)]
            att_w = [bq_w, bk_w, bv_w]
            w_out = w_out_e[i].astype(BF16)
            wo_a, wo_b = w_out[:HA * DK], w_out[HA * DK:]
            proj_p = inproj_pallas(xp, mod_p, norm_w[layer, 0], plain_w, att_w, BATCH * SEQ, keep_kv=True)
            proj_s = inproj_pallas(xs, mod_s, norm_w[layer, 0], plain_w, att_w, DEC_SEQ, rope_tabs=rope_tabs)
            ya_p, yb_p, dn, bk, bv = even_mixer_context(proj_p, ep)
            ya_s, yb_s = even_mixer_latent(proj_s, ep, state_dn[:, i], cache_b_k[:, i], cache_b_v[:, i])
            new_dn.append(dn)
            new_bk.append(bk)
            new_bv.append(bv)
        else:
            op = {'q_norm': c_q_norm[i], 'k_norm': c_k_norm[i], 'lam_re': s5_lam_re[i],
                  'lam_im': s5_lam_im[i], 'log_dt': s5_log_dt[i], 'b_re': s5_b_re[i], 'b_im': s5_b_im[i],
                  'c_re': s5_c_re[i], 'c_im': s5_c_im[i], 'd': s5_d[i], 'glu_w': s5_glu_w[i],
                  'glu_b': s5_glu_b[i]}
            cq_w, ck_w, cv_w, u_w = split_cols(w_in_o[i].astype(BF16), ODD_SPLITS)
            norms = (op['q_norm'], op['k_norm'])
            w_out = w_out_o[i].astype(BF16)
            wo_a, wo_b = w_out[:HC * HEAD_DIM], w_out[HC * HEAD_DIM:]
            proj_p = inproj_pallas(xp, mod_p, norm_w[layer, 0], [u_w], [cq_w, ck_w, cv_w], BATCH * SEQ,
                                   norm_ws=norms, keep_kv=True)
            proj_s = inproj_pallas(xs, mod_s, norm_w[layer, 0], [u_w], [cq_w, ck_w, cv_w], DEC_SEQ,
                                   norm_ws=norms, rope_tabs=rope_tabs)
            ya_p, yb_p, ck, cv, s5 = odd_mixer_context(proj_p, op)
            ya_s, yb_s = odd_mixer_latent(proj_s, op, cache_c_k[:, i], cache_c_v[:, i], state_s5[:, i])
            new_ck.append(ck)
            new_cv.append(cv)
            new_s5.append(s5)
        wr = _router_weights(moe_rg[layer], moe_re[layer])
        br = _pad_cols(jnp.concatenate([moe_rg_b[layer], moe_re_b[layer]])[None, :], ROUTER_LANES)
        moe_w = (wo_a, wo_b, wr, br, moe_wg[layer].astype(BF16), moe_wu[layer].astype(BF16),
                 moe_wd[layer].astype(BF16))
        last = layer == DEPTH - 1
        xp = moe_pallas(xp, ya_p, yb_p, mod_p, norm_w[layer, 1], *moe_w, final_norm_w, BATCH * SEQ, last)
        xs = moe_pallas(xs, ya_s, yb_s, mod_s, norm_w[layer, 1], *moe_w, final_norm_w, DEC_SEQ, last)
    y_prompt = xp.reshape(x_prompt.shape)
    y_sample = xs.reshape(x_sample.shape)
    return (y_prompt, y_sample, jnp.stack(new_dn, axis=1), jnp.stack(new_bk, axis=1), jnp.stack(new_bv, axis=1),
            jnp.stack(new_ck, axis=1), jnp.stack(new_cv, axis=1), jnp.stack(new_s5, axis=1))
```

```python
import functools

import numpy as np
import jax
import jax.numpy as jnp
from jax import lax
from jax.experimental import pallas as pl
from jax.experimental.pallas import tpu as pltpu

D_MODEL = 1024
BATCH = 32
SEQ = 256
DEPTH = 2
DEC_BATCH = 8
DEC_SEQ = 2048
PAST_LEN = 256

GRID_W = 64
HEAD_DIM = 64
BLK = 128
ROPE_THETA = 10000.0
EPS = 1e-6
N_EVEN = (DEPTH + 1) // 2
N_ODD = DEPTH // 2
HA = D_MODEL // 128
DK = HEAD_DIM
CONV_W = 5
DN_CHUNK = 64
HB = D_MODEL // 128
HKV_B = HB // 4
WINDOW = 128
HC = (3 * D_MODEL // 4) // HEAD_DIM
HKV_C = HC // 3
D_S5 = D_MODEL // 4
S5_CH = 16
G_D = D_S5 // S5_CH
S5_P = 64
N_GROUPS = 4
EXP_PER_GROUP = 8
N_EXP = N_GROUPS * EXP_PER_GROUP
EXPERT_FF = D_MODEL // 8
TOP_K = 2

EVEN_SPLITS = (3 * HA * DK, HA * DK, 2 * HA, 2 * HA, HB * HEAD_DIM, HKV_B * HEAD_DIM, HKV_B * HEAD_DIM)
D_IN_EVEN = sum(EVEN_SPLITS)
D_MIX_EVEN = HA * DK + HB * HEAD_DIM
ODD_SPLITS = (HC * HEAD_DIM, HKV_C * HEAD_DIM, HKV_C * HEAD_DIM, D_S5)
D_IN_ODD = sum(ODD_SPLITS)
D_MIX_ODD = HC * HEAD_DIM + D_S5

F32 = jnp.float32
BF16 = jnp.bfloat16


def split_cols(x, sizes):
    idx = np.cumsum(sizes)[:-1].tolist()
    return jnp.split(x, idx, axis=-1)


def axial_rope_tables(rows):
    row = jnp.repeat(jnp.arange(rows), GRID_W).astype(F32)
    col = jnp.tile(jnp.arange(GRID_W), rows).astype(F32)
    quarter = HEAD_DIM // 4
    freqs = ROPE_THETA ** (-jnp.arange(quarter, dtype=F32) / quarter)
    ang_r = row[:, None] * freqs
    ang_c = col[:, None] * freqs
    ang = jnp.concatenate([ang_r, ang_r, ang_c, ang_c], axis=-1)
    return jnp.cos(ang), jnp.sin(ang)


SUBLANES = 8
VMEM_LIMIT = 56 * 1024 * 1024
S5_ROWS = 1024
S5_STATE = G_D * S5_P


def _s5_scan_kernel(u_ref, bmat_ref, cmat_ref, a_ref, s0_ref, y_ref, sfin_ref, xs_ref, st_ref, *, nblk, bsz):
    dr = pl.program_id(0)
    blk = pl.program_id(1)

    @pl.when(blk == 0)
    def _():
        st_ref[...] = s0_ref[0]

    xs_ref[...] = jnp.dot(u_ref[...].astype(BF16), bmat_ref[0], preferred_element_type=F32)
    a_re = jnp.broadcast_to(a_ref[0, 0:1, :], (SUBLANES, S5_STATE))
    a_im = jnp.broadcast_to(a_ref[0, 1:2, :], (SUBLANES, S5_STATE))
    steps = S5_ROWS // bsz
    for sg in range(bsz // SUBLANES):
        rows = pl.ds(sg * SUBLANES, SUBLANES)

        def body(i, carry, sg=sg):
            x_re, x_im = carry
            l = i + dr * (steps - 1 - 2 * i)
            r = pl.ds(pl.multiple_of(l * bsz + sg * SUBLANES, SUBLANES), SUBLANES)
            n_re = a_re * x_re - a_im * x_im + xs_ref[r, 0:S5_STATE]
            n_im = a_re * x_im + a_im * x_re + xs_ref[r, S5_STATE:2 * S5_STATE]
            xs_ref[r, 0:S5_STATE] = n_re
            xs_ref[r, S5_STATE:2 * S5_STATE] = n_im
            return n_re, n_im

        x_re, x_im = lax.fori_loop(0, steps, body, (st_ref[rows, 0:S5_STATE], st_ref[rows, S5_STATE:2 * S5_STATE]),
                                   unroll=4)
        st_ref[rows, 0:S5_STATE] = x_re
        st_ref[rows, S5_STATE:2 * S5_STATE] = x_im
    y_ref[0] = jnp.dot(xs_ref[...].astype(BF16), cmat_ref[0], preferred_element_type=F32)

    @pl.when(blk == nblk - 1)
    def _():
        sfin_ref[0] = st_ref[...]


def _s5_out_kernel(yf_ref, yb_ref, u_ref, d_ref, w_ref, b_ref, o_ref):
    y = yf_ref[0] + yb_ref[0] + u_ref[...] * d_ref[...]
    y = jax.nn.gelu(y)
    gate = jnp.dot(y.astype(BF16), w_ref[...], preferred_element_type=F32) + b_ref[...]
    o_ref[...] = y * jax.nn.sigmoid(gate)


def s5_operators(op):
    lam_re, lam_im = op['lam_re'].astype(F32), op['lam_im'].astype(F32)
    dt = jnp.exp(op['log_dt'].astype(F32))[..., None]
    mag = jnp.exp(lam_re * dt)
    a_re, a_im = mag * jnp.cos(lam_im * dt), mag * jnp.sin(lam_im * dt)
    den = lam_re * lam_re + lam_im * lam_im
    k_re = ((a_re - 1) * lam_re + a_im * lam_im) / den
    k_im = (a_im * lam_re - (a_re - 1) * lam_im) / den
    b_re, b_im = op['b_re'].astype(F32), op['b_im'].astype(F32)
    bb_re = k_re[..., None] * b_re - k_im[..., None] * b_im
    bb_im = k_re[..., None] * b_im + k_im[..., None] * b_re
    eye = jnp.eye(G_D, dtype=F32)

    def b_blocks(t):
        return jnp.einsum('dgpc,gh->dgchp', t, eye).reshape(2, D_S5, S5_STATE)

    def c_blocks(t):
        return jnp.einsum('dgcp,gh->dgphc', t, eye).reshape(2, S5_STATE, D_S5)

    b_blk = jnp.concatenate([b_blocks(bb_re), b_blocks(bb_im)], axis=-1).astype(BF16)
    c_blk = jnp.concatenate([c_blocks(op['c_re'].astype(F32)), -c_blocks(op['c_im'].astype(F32))],
                            axis=1).astype(BF16)
    a_vec = jnp.stack([a_re.reshape(2, S5_STATE), a_im.reshape(2, S5_STATE)], axis=1)
    return b_blk, c_blk, a_vec


def s5_pallas(u, op, ops, s0):
    b, l, _ = u.shape
    b_blk, c_blk, a_vec = ops
    n = b * l
    nblk = n // S5_ROWS
    u_t = jnp.swapaxes(u, 0, 1).reshape(n, D_S5)

    def rows_map(dr, blk):
        return (blk + dr * (nblk - 1 - 2 * blk), 0)

    y2, s_fin = pl.pallas_call(
        functools.partial(_s5_scan_kernel, nblk=nblk, bsz=b),
        grid=(2, nblk),
        in_specs=[
            pl.BlockSpec((S5_ROWS, D_S5), rows_map),
            pl.BlockSpec((1, D_S5, 2 * S5_STATE), lambda dr, blk: (dr, 0, 0)),
            pl.BlockSpec((1, 2 * S5_STATE, D_S5), lambda dr, blk: (dr, 0, 0)),
            pl.BlockSpec((1, 2, S5_STATE), lambda dr, blk: (dr, 0, 0)),
            pl.BlockSpec((1, b, 2 * S5_STATE), lambda dr, blk: (dr, 0, 0)),
        ],
        out_specs=[
            pl.BlockSpec((1, S5_ROWS, D_S5), lambda dr, blk: (dr,) + rows_map(dr, blk)),
            pl.BlockSpec((1, b, 2 * S5_STATE), lambda dr, blk: (dr, 0, 0)),
        ],
        out_shape=[jax.ShapeDtypeStruct((2, n, D_S5), F32), jax.ShapeDtypeStruct((2, b, 2 * S5_STATE), F32)],
        scratch_shapes=[pltpu.VMEM((S5_ROWS, 2 * S5_STATE), F32), pltpu.VMEM((b, 2 * S5_STATE), F32)],
        compiler_params=pltpu.CompilerParams(dimension_semantics=("arbitrary", "arbitrary"),
                                             vmem_limit_bytes=VMEM_LIMIT),
        name="s5_scan",
    )(u_t, b_blk, c_blk, a_vec, s0)
    tm = S5_ROWS
    y = pl.pallas_call(
        _s5_out_kernel,
        grid=(n // tm,),
        in_specs=[
            pl.BlockSpec((1, tm, D_S5), lambda i: (0, i, 0)),
            pl.BlockSpec((1, tm, D_S5), lambda i: (1, i, 0)),
            pl.BlockSpec((tm, D_S5), lambda i: (i, 0)),
            pl.BlockSpec((1, D_S5), lambda i: (0, 0)),
            pl.BlockSpec((D_S5, D_S5), lambda i: (0, 0)),
            pl.BlockSpec((1, D_S5), lambda i: (0, 0)),
        ],
        out_specs=pl.BlockSpec((tm, D_S5), lambda i: (i, 0)),
        out_shape=jax.ShapeDtypeStruct((n, D_S5), F32),
        name="s5_out",
    )(y2, y2, u_t, op['d'].astype(F32).reshape(1, D_S5), op['glu_w'].astype(BF16),
      op['glu_b'].astype(F32).reshape(1, D_S5))
    return jnp.swapaxes(y.reshape(l, b, D_S5), 0, 1), s_fin


LANES = 128
DN_PAIRS = HA * DK // LANES
NEG_BIG = -1e30
DN_UNROLL = 4
DN_BLOCK = DN_CHUNK // 4


def _nt_dot(a, b):
    return lax.dot_general(a, b, (((1,), (1,)), ((), ())), preferred_element_type=F32)


def _bf16_dot(a, b):
    return jnp.dot(a.astype(BF16), b.astype(BF16), preferred_element_type=F32)


def _select_dot(x, sel):
    x1 = x.astype(BF16)
    r1 = x - x1.astype(F32)
    x2 = r1.astype(BF16)
    x3 = (r1 - x2.astype(F32)).astype(BF16)
    dot = functools.partial(jnp.dot, preferred_element_type=F32)
    return dot(x1, sel) + dot(x2, sel) + dot(x3, sel)


def _head_sums(x):
    lane = lax.broadcasted_iota(jnp.int32, x.shape, 1)
    lo = lane < HEAD_DIM
    s0 = jnp.sum(jnp.where(lo, x, 0.0), axis=-1, keepdims=True)
    s1 = jnp.sum(jnp.where(lo, 0.0, x), axis=-1, keepdims=True)
    return jnp.where(lo, s0, s1)


def _chunk_cumsum(x, reverse):
    n = x.shape[0]
    pos = lax.broadcasted_iota(jnp.int32, x.shape, 0) % DN_CHUNK
    s = 1
    while s < DN_CHUNK:
        if reverse:
            x = x + jnp.where(pos < DN_CHUNK - s, pltpu.roll(x, n - s, 0), 0.0)
        else:
            x = x + jnp.where(pos >= s, pltpu.roll(x, s, 0), 0.0)
        s *= 2
    return x


def _short_conv_silu(x, w):
    n = x.shape[0]
    row = lax.broadcasted_iota(jnp.int32, x.shape, 0)
    pad = CONV_W // 2
    y = x * w[pad:pad + 1, :]
    for t in range(CONV_W):
        s = t - pad
        if s == 0:
            continue
        shifted = pltpu.roll(x, (-s) % n, 0)
        valid = (row + s >= 0) & (row + s < n)
        y = y + jnp.where(valid, shifted, 0.0) * w[t:t + 1, :]
    return jax.nn.silu(y)


def _unit_tri_solve(ms, rhss, ri, ci):
    assert DN_CHUNK == 4 * DN_BLOCK
    n = rhss[0].shape[-1]
    same_block = (ri // DN_BLOCK) == (ci // DN_BLOCK)
    dot = functools.partial(jnp.dot, preferred_element_type=F32)

    def apply(p, r):
        hi = r.astype(BF16)
        lo = (r - hi.astype(F32)).astype(BF16)
        x = dot(p, jnp.concatenate([hi, lo], axis=-1))
        return x[:, 0:n] + x[:, n:2 * n]

    diag = [jnp.where(same_block, m, 0.0) for m in ms]
    rest = [jnp.where(same_block, 0.0, m).astype(BF16) for m in ms]
    pows = [d.astype(BF16) for d in diag]
    tinv = [jnp.where(ri == ci, 1.0, 0.0) - d for d in diag]
    power = 2
    while power < DN_BLOCK:
        pows = [dot(p, p).astype(BF16) for p in pows]
        tinv = [t + dot(t.astype(BF16), p) for t, p in zip(tinv, pows)]
        power *= 2
    tinv = [t.astype(BF16) for t in tinv]
    p1 = [dot(t, e).astype(BF16) for t, e in zip(tinv, rest)]
    sols = [apply(t, r) for t, r in zip(tinv, rhss)]
    p2 = [dot(p, p).astype(BF16) for p in p1]
    sols = [s - apply(p, s) for p, s in zip(p1, sols)]
    return [s + apply(p, s) for p, s in zip(p2, sols)]


def _deltanet_kernel(q_ref, k_ref, v_ref, z_ref, a_ref, bb_ref, cwq_ref, cwk_ref, cwv_ref, alog_ref, dtb_ref, nw_ref,
                     s0_ref, y_ref, sfin_ref,
                     qs_ref, ks_ref, vs_ref, o_ref, gc_ref, beta_ref, u_ref, w_ref, attn_ref, qp_ref, kpt_ref,
                     etot_ref, *, seq):
    hp = pl.program_id(1)
    nchunk = seq // DN_CHUNK
    c64 = DN_CHUNK
    head_lanes = [slice(j * HEAD_DIM, (j + 1) * HEAD_DIM) for j in range(2)]

    q = _short_conv_silu(q_ref[0], cwq_ref[...])
    q = q * lax.rsqrt(_head_sums(q * q) + EPS) * (DK ** -0.5)
    k = _short_conv_silu(k_ref[0], cwk_ref[...])
    k = k * lax.rsqrt(_head_sums(k * k) + EPS)
    v = _short_conv_silu(v_ref[0], cwv_ref[...])
    qs_ref[...] = q
    ks_ref[...] = k
    vs_ref[...] = v
    o_ref[...] = jnp.zeros_like(o_ref)

    g_col = -jnp.exp(alog_ref[...]) * jax.nn.softplus(a_ref[0] + dtb_ref[...])
    b_col = jax.nn.sigmoid(bb_ref[0])
    sel_row = lax.broadcasted_iota(jnp.int32, (2 * HA, 2 * LANES), 0)
    sel_lane = lax.broadcasted_iota(jnp.int32, (2 * HA, 2 * LANES), 1)
    sel = (sel_row == (sel_lane // LANES) * HA + 2 * hp + (sel_lane // HEAD_DIM) % 2).astype(BF16)
    g_b = _select_dot(g_col, sel)
    gc_ref[:, 0:LANES] = _chunk_cumsum(g_b[:, 0:LANES], reverse=False)
    gc_ref[:, LANES:2 * LANES] = _chunk_cumsum(g_b[:, LANES:2 * LANES], reverse=True)
    beta_ref[...] = _select_dot(b_col, sel)

    ri = lax.broadcasted_iota(jnp.int32, (c64, c64), 0)
    ci = lax.broadcasted_iota(jnp.int32, (c64, c64), 1)

    def intra_chunk(step, carry):
        chunk_ids = [step * DN_UNROLL + t for t in range(DN_UNROLL)]
        rows_c = [pl.ds(pl.multiple_of(c * c64, c64), c64) for c in chunk_ids]
        loads = [(qs_ref[r, :], ks_ref[r, :], vs_ref[r, :], gc_ref[r, :], beta_ref[r, :]) for r in rows_c]
        heads = [(t, j, ld[0][:, head_lanes[j]], ld[1][:, head_lanes[j]], ld[2][:, head_lanes[j]])
                 for t, ld in enumerate(loads) for j in range(2)]
        gate_lanes = [slice((c % 2) * LANES + (c // 2) * HEAD_DIM, (c % 2) * LANES + (c // 2 + 1) * HEAD_DIM)
                      for c in range(4)]
        gates = {(t, c): (ld[3][:, gate_lanes[c]], ld[4][:, gate_lanes[c]])
                 for t, ld in enumerate(loads) for c in range(4)}
        kbs = [h[3].astype(BF16) for h in heads]
        kks = [_nt_dot(kb, kb) for kb in kbs]
        qks = [_nt_dot(h[2].astype(BF16), kb) for h, kb in zip(heads, kbs)]
        chains, ms, rhss = [], [], []
        for (t, j, qj, kj, vj), kk, qk in zip(heads, kks, qks):
            for d in range(2):
                gcb, bet = gates[(t, 2 * j + d)]
                diff = gcb - gcb.T
                earlier = (ri >= ci) if d == 0 else (ri <= ci)
                strict = (ri > ci) if d == 0 else (ri < ci)
                dec = jnp.exp(jnp.where(earlier, diff, NEG_BIG))
                eg = jnp.exp(gcb)
                gtot = gcb[c64 - 1:c64, :] if d == 0 else gcb[0:1, :]
                ms.append(jnp.where(strict, kk * bet * dec, 0.0))
                rhss.append(jnp.concatenate([vj * bet, kj * bet * eg], axis=-1))
                chains.append((t, 2 * j + d, qk * dec, qj * eg, (kj * jnp.exp(gtot - gcb)).T,
                               jnp.broadcast_to(jnp.exp(gtot), (SUBLANES, HEAD_DIM))))
        uws = _unit_tri_solve(ms, rhss, ri, ci)
        for (t, c, attn, qp, kpt, etot), uw in zip(chains, uws):
            u_ref[c, rows_c[t], :] = uw[:, 0:HEAD_DIM]
            w_ref[c, rows_c[t], :] = uw[:, HEAD_DIM:2 * HEAD_DIM].astype(BF16)
            attn_ref[c, rows_c[t], :] = attn.astype(BF16)
            qp_ref[c, rows_c[t], :] = qp.astype(BF16)
            kpt_ref[c, rows_c[t], :] = kpt.astype(BF16)
            etot_ref[c, pl.ds(pl.multiple_of(chunk_ids[t] * SUBLANES, SUBLANES), SUBLANES), :] = etot
        return carry

    lax.fori_loop(0, nchunk // DN_UNROLL, intra_chunk, 0)

    def inter_chunk(i, states):
        rows_d = [pl.ds(pl.multiple_of(cc * c64, c64), c64) for cc in (i, nchunk - 1 - i)]
        erow_d = [pl.ds(pl.multiple_of(cc * SUBLANES, SUBLANES), SUBLANES) for cc in (i, nchunk - 1 - i)]
        loaded = [(w_ref[c, rows_d[c % 2], :], qp_ref[c, rows_d[c % 2], :], u_ref[c, rows_d[c % 2], :],
                   attn_ref[c, rows_d[c % 2], :], kpt_ref[c, rows_d[c % 2], :], etot_ref[c, erow_d[c % 2], :],
                   o_ref[c // 2, rows_d[c % 2], :]) for c in range(4)]
        ws_qs = [jnp.dot(jnp.concatenate([ld[0], ld[1]], axis=0), s.astype(BF16), preferred_element_type=F32)
                 for ld, s in zip(loaded, states)]
        v_new = [(ld[2] - r[0:c64]).astype(BF16) for ld, r in zip(loaded, ws_qs)]
        av = [jnp.dot(ld[3], vn, preferred_element_type=F32) for ld, vn in zip(loaded, v_new)]
        kv = [jnp.dot(ld[4], vn, preferred_element_type=F32) for ld, vn in zip(loaded, v_new)]
        for c in range(4):
            o_ref[c // 2, rows_d[c % 2], :] = loaded[c][6] + ws_qs[c][c64:2 * c64] + av[c]
        return tuple(s * ld[5][0:1, :] + x for s, ld, x in zip(states, loaded, kv))

    init = tuple(s0_ref[0, c % 2, c // 2] for c in range(4))
    fin = lax.fori_loop(0, nchunk, inter_chunk, init)
    for c in range(4):
        sfin_ref[0, c % 2, c // 2] = fin[c]

    o = jnp.concatenate([o_ref[0], o_ref[1]], axis=-1)
    o = o * lax.rsqrt(_head_sums(o * o) * (1.0 / DK) + EPS) * nw_ref[...]
    y_ref[0] = o * jax.nn.silu(z_ref[0])


def deltanet_pallas(qkv, z, a, bb, ep, s0):
    b, l, _ = qkv.shape
    npair = DN_PAIRS

    def col_spec(off):
        return pl.BlockSpec((1, l, LANES), lambda i, hp: (i, 0, off + hp))

    def cw_spec(off):
        return pl.BlockSpec((CONV_W, LANES), lambda i, hp: (0, off + hp))

    def full(shape):
        return pl.BlockSpec(shape, lambda i, hp: (0,) * len(shape))

    state_spec = pl.BlockSpec((1, 2, 2, DK, DK), lambda i, hp: (i, 0, hp, 0, 0))
    row_f32 = pltpu.VMEM((l, LANES), F32)
    gate_f32 = pltpu.VMEM((l, 2 * LANES), F32)
    chain_f32 = pltpu.VMEM((4, l, HEAD_DIM), F32)
    chain_bf16 = pltpu.VMEM((4, l, HEAD_DIM), BF16)
    y, s_fin = pl.pallas_call(
        functools.partial(_deltanet_kernel, seq=l),
        grid=(b, npair),
        in_specs=[col_spec(0), col_spec(npair), col_spec(2 * npair), col_spec(0),
                  pl.BlockSpec((1, l, 2 * HA), lambda i, hp: (i, 0, 0)),
                  pl.BlockSpec((1, l, 2 * HA), lambda i, hp: (i, 0, 0)),
                  cw_spec(0), cw_spec(npair), cw_spec(2 * npair),
                  full((1, 2 * HA)), full((1, 2 * HA)), full((1, LANES)), state_spec],
        out_specs=[col_spec(0), state_spec],
        out_shape=[jax.ShapeDtypeStruct((b, l, HA * DK), F32), jax.ShapeDtypeStruct((b, 2, HA, DK, DK), F32)],
        scratch_shapes=[row_f32, row_f32, row_f32, pltpu.VMEM((2, l, HEAD_DIM), F32), gate_f32, gate_f32,
                        chain_f32, chain_bf16, chain_bf16, chain_bf16, chain_bf16,
                        pltpu.VMEM((4, l // DN_CHUNK * SUBLANES, HEAD_DIM), F32)],
        compiler_params=pltpu.CompilerParams(dimension_semantics=("arbitrary", "arbitrary"),
                                             vmem_limit_bytes=VMEM_LIMIT),
        name="deltanet",
    )(qkv, qkv, qkv, z, a, bb, ep['conv_w'], ep['conv_w'], ep['conv_w'],
      ep['a_log'].astype(F32).reshape(1, 2 * HA), ep['dt_bias'].astype(F32).reshape(1, 2 * HA),
      jnp.tile(ep['norm_w'].astype(F32), 2).reshape(1, LANES), s0)
    return y, s_fin


TM = 512
ATT_TQ = BLK
ROT = HEAD_DIM // 4


def rope_tables():
    cos, sin = axial_rope_tables(DEC_SEQ // GRID_W)
    cos2 = jnp.tile(cos, (1, LANES // HEAD_DIM))
    sin2 = jnp.tile(sin, (1, LANES // HEAD_DIM))
    first = (jnp.arange(LANES) % (2 * ROT)) < ROT
    return cos2, jnp.where(first, -sin2, 0.0), jnp.where(first, 0.0, sin2)


def _head_prep(x, w_ref, tabs, scale):
    outs, normed = [], []
    for c in range(x.shape[1] // LANES):
        xc = x[:, c * LANES:(c + 1) * LANES]
        if w_ref is not None:
            xc = xc * lax.rsqrt(_head_sums(xc * xc) * (1.0 / HEAD_DIM) + EPS) * w_ref[...]
            normed.append(xc)
        if tabs is not None:
            cos_ref, sup_ref, sdn_ref = tabs
            xc = (xc * cos_ref[...] + pltpu.roll(xc, LANES - ROT, 1) * sup_ref[...]
                  + pltpu.roll(xc, ROT, 1) * sdn_ref[...])
        outs.append((xc * scale).astype(BF16))
    cat = lambda parts: jnp.concatenate(parts, axis=-1) if len(parts) > 1 else parts[0]
    return cat(outs), (cat(normed) if w_ref is not None else None)


def _attn_kernel(q_ref, k_ref, v_ref, *refs, grp, hk, windowed, has_ctx, has_sink):
    refs = list(refs)
    kc_ref, vc_ref = (refs.pop(0), refs.pop(0)) if has_ctx else (None, None)
    sink_ref = refs.pop(0) if has_sink else None
    o_ref = refs[0]
    tq = q_ref.shape[1]
    seq_k = k_ref.shape[1]
    if windowed:
        qi = pl.program_id(1)
        start = jnp.clip((qi - 1) * tq, 0, seq_k - 3 * tq)
        krows = pl.ds(pl.multiple_of(start, tq), 3 * tq)
        qpos = qi * tq + lax.broadcasted_iota(jnp.int32, (grp * tq, 3 * tq), 0) % tq
        kpos = start + lax.broadcasted_iota(jnp.int32, (grp * tq, 3 * tq), 1)
        keep = jnp.abs(qpos - kpos) <= WINDOW
    else:
        krows = pl.ds(0, seq_k)
    q = q_ref[0]
    groups = range(hk)
    lanes = [slice(g * HEAD_DIM, (g + 1) * HEAD_DIM) for g in groups]
    qgs = [jnp.concatenate([q[:, (g * grp + j) * HEAD_DIM:(g * grp + j + 1) * HEAD_DIM] for j in range(grp)], axis=0)
           for g in groups]
    ss = [_nt_dot(qg, k_ref[0, krows, gl]) for qg, gl in zip(qgs, lanes)]
    if windowed:
        ss = [jnp.where(keep, s, NEG_BIG) for s in ss]
    ms = [jnp.max(s, axis=-1, keepdims=True) for s in ss]
    if has_ctx:
        kc = kc_ref[0].astype(BF16)
        scs = [_nt_dot(qg, kc[:, gl]) for qg, gl in zip(qgs, lanes)]
        ms = [jnp.maximum(m, jnp.max(sc, axis=-1, keepdims=True)) for m, sc in zip(ms, scs)]
    if has_sink:
        head_rows = [slice(j * tq, (j + 1) * tq) for j in range(grp)]
        ms = [jnp.concatenate([jnp.maximum(m[head_rows[j]], sink_ref[g * grp + j]) for j in range(grp)], axis=0)
              for g, m in zip(groups, ms)]
    ps = [jnp.exp(s - m) for s, m in zip(ss, ms)]
    dens = [jnp.sum(p, axis=-1, keepdims=True) for p in ps]
    os_ = [jnp.dot(p.astype(BF16), v_ref[0, krows, gl], preferred_element_type=F32) for p, gl in zip(ps, lanes)]
    if has_ctx:
        vc = vc_ref[0].astype(BF16)
        pcs = [jnp.exp(sc - m) for sc, m in zip(scs, ms)]
        dens = [den + jnp.sum(pc, axis=-1, keepdims=True) for den, pc in zip(dens, pcs)]
        os_ = [o + jnp.dot(pc.astype(BF16), vc[:, gl], preferred_element_type=F32)
               for o, pc, gl in zip(os_, pcs, lanes)]
    if has_sink:
        dens = [den + jnp.concatenate([jnp.exp(sink_ref[g * grp + j] - m[head_rows[j]]) for j in range(grp)], axis=0)
                for g, den, m in zip(groups, dens, ms)]
    outs = [None] * (grp * hk)
    for g in groups:
        o = os_[g] / dens[g]
        for j in range(grp):
            outs[g * grp + j] = o[j * tq:(j + 1) * tq]
    o_ref[0] = jnp.concatenate(outs, axis=-1)


def attention_pallas(q, k, v, ctx=None, sink=None, windowed=False):
    b, lq, hd = q.shape
    lk, kd = k.shape[1], k.shape[2]
    hk = kd // HEAD_DIM
    grp = hd // kd
    tq = ATT_TQ if windowed or lq > 2 * ATT_TQ else lq
    ins = [q, k, v]
    in_specs = [pl.BlockSpec((1, tq, hd), lambda i, t: (i, t, 0)),
                pl.BlockSpec((1, lk, kd), lambda i, t: (i, 0, 0)),
                pl.BlockSpec((1, lk, kd), lambda i, t: (i, 0, 0))]
    if ctx is not None:
        ins += list(ctx)
        in_specs += [pl.BlockSpec((1, ctx[0].shape[1], kd), lambda i, t: (i, 0, 0))] * 2
    if sink is not None:
        ins.append(sink.astype(F32))
        in_specs.append(pl.BlockSpec(memory_space=pltpu.SMEM))
    return pl.pallas_call(
        functools.partial(_attn_kernel, grp=grp, hk=hk, windowed=windowed, has_ctx=ctx is not None,
                          has_sink=sink is not None),
        grid=(b, lq // tq),
        in_specs=in_specs,
        out_specs=pl.BlockSpec((1, tq, hd), lambda i, t: (i, t, 0)),
        out_shape=jax.ShapeDtypeStruct((b, lq, hd), F32),
        compiler_params=pltpu.CompilerParams(dimension_semantics=("arbitrary", "arbitrary"),
                                             vmem_limit_bytes=VMEM_LIMIT),
        name="attention",
    )(*ins)


def even_mixer_context(proj, ep):
    qkv, z, ab, qh, kh, vh, bk, bv = proj
    qkv, z, ab = [t.reshape(BATCH, SEQ, -1) for t in (qkv, z, ab)]
    a, bb = ab[..., 0:2 * HA], ab[..., 2 * HA:4 * HA]
    y_a, dn_fin = deltanet_pallas(qkv, z, a, bb, ep, jnp.zeros((BATCH, 2, HA, DK, DK), F32))
    y_b = attention_pallas(*[t.reshape(BATCH, SEQ, -1) for t in (qh, kh, vh)], sink=ep['sink'])
    return (y_a.reshape(-1, HA * DK), y_b.reshape(-1, HB * HEAD_DIM), dn_fin,
            bk.reshape(BATCH, SEQ, HKV_B, HEAD_DIM), bv.reshape(BATCH, SEQ, HKV_B, HEAD_DIM))


def even_mixer_latent(proj, ep, dn_state, ctx_k, ctx_v):
    qkv, z, ab, qh, kh, vh = proj
    qkv, z, ab = [t.reshape(DEC_BATCH, DEC_SEQ, -1) for t in (qkv, z, ab)]
    a, bb = ab[..., 0:2 * HA], ab[..., 2 * HA:4 * HA]
    y_a, _ = deltanet_pallas(qkv, z, a, bb, ep, dn_state.astype(F32))
    ctx = (ctx_k.reshape(DEC_BATCH, PAST_LEN, -1), ctx_v.reshape(DEC_BATCH, PAST_LEN, -1))
    y_b = attention_pallas(*[t.reshape(DEC_BATCH, DEC_SEQ, -1) for t in (qh, kh, vh)], ctx=ctx, sink=ep['sink'],
                           windowed=True)
    return y_a.reshape(-1, HA * DK), y_b.reshape(-1, HB * HEAD_DIM)


def odd_mixer_context(proj, op):
    u, qh, kh, vh, k_normed, cv = proj
    y_c = attention_pallas(*[t.reshape(BATCH, SEQ, -1) for t in (qh, kh, vh)])
    y_d, s_fin = s5_pallas(u.reshape(BATCH, SEQ, -1), op, s5_operators(op), jnp.zeros((2, BATCH, 2 * S5_STATE), F32))
    s_fin = jnp.transpose(s_fin.reshape(2, BATCH, 2, G_D, S5_P), (1, 0, 3, 4, 2))
    return (y_c.reshape(-1, HC * HEAD_DIM), y_d.reshape(-1, D_S5), k_normed.reshape(BATCH, SEQ, HKV_C, HEAD_DIM),
            cv.reshape(BATCH, SEQ, HKV_C, HEAD_DIM), s_fin)


def odd_mixer_latent(proj, op, ctx_k, ctx_v, s5_state):
    u, qh, kh, vh = proj
    ctx = (ctx_k.reshape(DEC_BATCH, PAST_LEN, -1), ctx_v.reshape(DEC_BATCH, PAST_LEN, -1))
    y_c = attention_pallas(*[t.reshape(DEC_BATCH, DEC_SEQ, -1) for t in (qh, kh, vh)], ctx=ctx)
    s0 = jnp.transpose(s5_state.astype(F32), (1, 0, 4, 2, 3)).reshape(2, DEC_BATCH, 2 * S5_STATE)
    y_d, _ = s5_pallas(u.reshape(DEC_BATCH, DEC_SEQ, -1), op, s5_operators(op), s0)
    return y_c.reshape(-1, HC * HEAD_DIM), y_d.reshape(-1, D_S5)


ADA_ROWS = 16
ADA_TN = 1024
MOD_ROWS = 8
ROUTER_LANES = LANES
MOE_FCHUNK = EXP_PER_GROUP * EXPERT_FF


def _ada_kernel(c_ref, w_ref, b_ref, o_ref):
    o_ref[0] = _bf16_dot(jax.nn.silu(c_ref[...]), w_ref[0]) + b_ref[0]


def ada_pallas(cvec, ada_w, ada_b):
    n = 6 * D_MODEL
    return pl.pallas_call(
        _ada_kernel,
        grid=(DEPTH, n // ADA_TN),
        in_specs=[pl.BlockSpec((ADA_ROWS, D_MODEL), lambda l, j: (0, 0)),
                  pl.BlockSpec((1, D_MODEL, ADA_TN), lambda l, j: (l, 0, j)),
                  pl.BlockSpec((1, 1, ADA_TN), lambda l, j: (l, 0, j))],
        out_specs=pl.BlockSpec((1, ADA_ROWS, ADA_TN), lambda l, j: (l, 0, j)),
        out_shape=jax.ShapeDtypeStruct((DEPTH, ADA_ROWS, n), F32),
        name="ada_modulation",
    )(cvec, ada_w, ada_b.reshape(DEPTH, 1, n))


def _modulated(x, nw, shift, scale):
    return x * lax.rsqrt(jnp.mean(x * x, axis=-1, keepdims=True) + EPS) * nw * (1 + scale) + shift


def _inproj_kernel(x_ref, mod_ref, nw_ref, *refs, n_plain, norm, rope, keep_k, keep_v):
    refs = list(refs)
    m = mod_ref[0]
    h = _modulated(x_ref[...], nw_ref[...], m[0:1], m[1:2]).astype(BF16)
    w_refs = [refs.pop(0) for _ in range(n_plain + 3)]
    qw_ref, kw_ref = (refs.pop(0), refs.pop(0)) if norm else (None, None)
    tabs = [refs.pop(0) for _ in range(3)] if rope else None
    plain_refs = [refs.pop(0) for _ in range(n_plain)]
    qo_ref, ko_ref, vo_ref = refs.pop(0), refs.pop(0), refs.pop(0)
    for w_ref, o_ref in zip(w_refs, plain_refs):
        o_ref[...] = jnp.dot(h, w_ref[...], preferred_element_type=F32)
    q, k, v = [jnp.dot(h, w_ref[...], preferred_element_type=F32) for w_ref in w_refs[n_plain:]]
    qo_ref[...] = _head_prep(q, qw_ref, tabs, HEAD_DIM ** -0.5)[0]
    ko, k_normed = _head_prep(k, kw_ref, tabs, 1.0)
    ko_ref[...] = ko
    vo_ref[...] = v.astype(BF16)
    if keep_k:
        refs.pop(0)[...] = k_normed if norm else k
    if keep_v:
        refs.pop(0)[...] = v


def inproj_pallas(x, mod, nw, plain_ws, qkv_ws, rows_per_mod, norm_ws=None, rope_tabs=None, keep_kv=False):
    t = x.shape[0]
    tiles_per_mod = rows_per_mod // TM
    row = lambda i: (i, 0)
    const = lambda i: (0, 0)
    weights = list(plain_ws) + list(qkv_ws)
    ins = [x, mod, nw.reshape(1, D_MODEL)] + weights
    in_specs = [pl.BlockSpec((TM, D_MODEL), row),
                pl.BlockSpec((1, MOD_ROWS, D_MODEL), lambda i: (i // tiles_per_mod, 0, 0)),
                pl.BlockSpec((1, D_MODEL), const)] + [pl.BlockSpec(w.shape, const) for w in weights]
    if norm_ws is not None:
        ins += [jnp.tile(w.astype(F32), LANES // HEAD_DIM).reshape(1, LANES) for w in norm_ws]
        in_specs += [pl.BlockSpec((1, LANES), const)] * 2
    if rope_tabs is not None:
        tiles = DEC_SEQ // TM
        ins += list(rope_tabs)
        in_specs += [pl.BlockSpec((TM, LANES), lambda i: (i % tiles, 0))] * 3
    outs = [(w.shape[1], F32) for w in plain_ws] + [(w.shape[1], BF16) for w in qkv_ws]
    if keep_kv:
        outs += [(qkv_ws[1].shape[1], F32), (qkv_ws[2].shape[1], F32)]
    return pl.pallas_call(
        functools.partial(_inproj_kernel, n_plain=len(plain_ws), norm=norm_ws is not None,
                          rope=rope_tabs is not None, keep_k=keep_kv, keep_v=keep_kv),
        grid=(t // TM,),
        in_specs=in_specs,
        out_specs=[pl.BlockSpec((TM, n), row) for n, _ in outs],
        out_shape=[jax.ShapeDtypeStruct((t, n), dt) for n, dt in outs],
        compiler_params=pltpu.CompilerParams(dimension_semantics=("arbitrary",), vmem_limit_bytes=VMEM_LIMIT),
        name="modulated_in_proj",
    )(*ins)


def _moe_kernel(x_ref, ya_ref, yb_ref, mod_ref, nw_ref, woa_ref, wob_ref, wr_ref, br_ref, wg_ref, wu_ref, wd_ref,
                fw_ref, o_ref, x1_ref, h_ref, comb_ref, acc_ref, *, final):
    f = pl.program_id(1)
    m = mod_ref[0]

    @pl.when(f == 0)
    def _():
        y = _bf16_dot(ya_ref[...], woa_ref[...]) + _bf16_dot(yb_ref[...], wob_ref[...])
        x1 = x_ref[...] + m[2:3] * y
        x1_ref[...] = x1
        h = _modulated(x1, nw_ref[...], m[3:4], m[4:5])
        h_hi = h.astype(BF16)
        h_ref[...] = h_hi
        h_lo = (h - h_hi.astype(F32)).astype(BF16)
        logits = (jnp.dot(jnp.concatenate([h_hi, h_lo], axis=-1), wr_ref[0:2 * D_MODEL, :], preferred_element_type=F32)
                  + jnp.dot(h_hi, wr_ref[2 * D_MODEL:3 * D_MODEL, :], preferred_element_type=F32) + br_ref[...])
        lane = lax.broadcasted_iota(jnp.int32, logits.shape, 1)
        lane_f = lane.astype(F32)
        is_group = lane < N_GROUPS
        lg = jnp.where(is_group, logits, -jnp.inf)
        g_max = jnp.max(lg, axis=-1, keepdims=True)
        g_idx = jnp.min(jnp.where(lg == g_max, lane_f, float(ROUTER_LANES)), axis=-1, keepdims=True)
        gate_g = 1.0 / jnp.sum(jnp.where(is_group, jnp.exp(logits - g_max), 0.0), axis=-1, keepdims=True)
        lane_group = ((lane + (EXP_PER_GROUP - N_GROUPS)) // EXP_PER_GROUP - 1).astype(F32)
        in_group = (lane >= N_GROUPS) & (lane < N_GROUPS + N_EXP) & (lane_group == g_idx)
        le = jnp.where(in_group, logits, -jnp.inf)
        v1 = jnp.max(le, axis=-1, keepdims=True)
        i1 = jnp.min(jnp.where(le == v1, lane_f, float(ROUTER_LANES)), axis=-1, keepdims=True)
        le2 = jnp.where(lane_f == i1, -jnp.inf, le)
        v2 = jnp.max(le2, axis=-1, keepdims=True)
        i2 = jnp.min(jnp.where(le2 == v2, lane_f, float(ROUTER_LANES)), axis=-1, keepdims=True)
        e2 = jnp.exp(v2 - v1)
        p1 = gate_g / (1.0 + e2)
        comb = jnp.where(lane_f == i1, p1, 0.0) + jnp.where(lane_f == i2, p1 * e2, 0.0)
        for grp in range(N_GROUPS):
            comb_ref[grp] = pltpu.roll(comb, ROUTER_LANES - (N_GROUPS + grp * EXP_PER_GROUP), 1)
        acc_ref[...] = jnp.zeros_like(acc_ref)

    h = h_ref[...]
    hid = jax.nn.silu(jnp.dot(h, wg_ref[...], preferred_element_type=F32)) * jnp.dot(
        h, wu_ref[...], preferred_element_type=F32)
    comb = comb_ref[f]
    hid = jnp.concatenate([hid[:, k * EXPERT_FF:(k + 1) * EXPERT_FF] * comb[:, k:k + 1]
                           for k in range(EXP_PER_GROUP)], axis=-1)
    acc_ref[...] += jnp.dot(hid.astype(BF16), wd_ref[...], preferred_element_type=F32)

    @pl.when(f == N_GROUPS - 1)
    def _():
        out = x1_ref[...] + m[5:6] * acc_ref[...]
        if final:
            out = out * lax.rsqrt(jnp.mean(out * out, axis=-1, keepdims=True) + EPS) * fw_ref[...]
        o_ref[...] = out


def moe_pallas(x, ya, yb, mod, nw, wo_a, wo_b, wr, br, wg, wu, wd, final_w, rows_per_mod, final):
    t = x.shape[0]
    tiles_per_mod = rows_per_mod // TM
    row = lambda i, f: (i, 0)
    const = lambda i, f: (0, 0)
    return pl.pallas_call(
        functools.partial(_moe_kernel, final=final),
        grid=(t // TM, N_GROUPS),
        in_specs=[pl.BlockSpec((TM, D_MODEL), row),
                  pl.BlockSpec((TM, ya.shape[1]), row),
                  pl.BlockSpec((TM, yb.shape[1]), row),
                  pl.BlockSpec((1, MOD_ROWS, D_MODEL), lambda i, f: (i // tiles_per_mod, 0, 0)),
                  pl.BlockSpec((1, D_MODEL), const),
                  pl.BlockSpec(wo_a.shape, const),
                  pl.BlockSpec(wo_b.shape, const),
                  pl.BlockSpec((3 * D_MODEL, ROUTER_LANES), const),
                  pl.BlockSpec((1, ROUTER_LANES), const),
                  pl.BlockSpec((D_MODEL, MOE_FCHUNK), lambda i, f: (0, f)),
                  pl.BlockSpec((D_MODEL, MOE_FCHUNK), lambda i, f: (0, f)),
                  pl.BlockSpec((MOE_FCHUNK, D_MODEL), lambda i, f: (f, 0)),
                  pl.BlockSpec((1, D_MODEL), const)],
        out_specs=pl.BlockSpec((TM, D_MODEL), row),
        out_shape=jax.ShapeDtypeStruct((t, D_MODEL), F32),
        scratch_shapes=[pltpu.VMEM((TM, D_MODEL), F32), pltpu.VMEM((TM, D_MODEL), BF16),
                        pltpu.VMEM((N_GROUPS, TM, ROUTER_LANES), F32), pltpu.VMEM((TM, D_MODEL), F32)],
        compiler_params=pltpu.CompilerParams(dimension_semantics=("arbitrary", "arbitrary"),
                                             vmem_limit_bytes=VMEM_LIMIT),
        name="out_proj_moe",
    )(x, ya, yb, mod, nw.reshape(1, D_MODEL), wo_a, wo_b, wr, br, wg, wu, wd, final_w.reshape(1, D_MODEL))


def _pad_cols(w, n):
    return jnp.pad(w, ((0, 0), (0, n - w.shape[1])))


def _router_weights(rg, re):
    w = _pad_cols(jnp.concatenate([rg, re], axis=1).astype(F32), ROUTER_LANES)
    hi = w.astype(BF16)
    lo = (w - hi.astype(F32)).astype(BF16)
    return jnp.concatenate([hi, hi, lo], axis=0)


def _mod_table(m):
    m = m.reshape(m.shape[0], 6, D_MODEL)
    return jnp.pad(m, ((0, 0), (0, MOD_ROWS - 6), (0, 0)))


def kernel(x_prompt, x_sample, c, c_ctx, state_dn, cache_b_k, cache_b_v, cache_c_k, cache_c_v, state_s5,
           ada_w, ada_b, norm_w, w_in_e, dn_conv_w, dn_a_log, dn_dt_bias, dn_norm_w, b_sink, w_out_e,
           w_in_o, c_q_norm, c_k_norm, s5_lam_re, s5_lam_im, s5_log_dt, s5_b_re, s5_b_im, s5_c_re, s5_c_im,
           s5_d, s5_glu_w, s5_glu_b, w_out_o, moe_rg, moe_rg_b, moe_re, moe_re_b, moe_wg, moe_wu, moe_wd,
           final_norm_w):
    rope_tabs = rope_tables()
    cvec = jnp.concatenate([c_ctx[None, :], c, jnp.zeros((ADA_ROWS - 1 - DEC_BATCH, D_MODEL), F32)], axis=0)
    mods = ada_pallas(cvec, ada_w, ada_b)
    xp = x_prompt.reshape(BATCH * SEQ, D_MODEL)
    xs = x_sample.reshape(DEC_BATCH * DEC_SEQ, D_MODEL)
    new_dn, new_bk, new_bv, new_ck, new_cv, new_s5 = [], [], [], [], [], []
    for layer in range(DEPTH):
        mod_p = _mod_table(mods[layer, 0:1])
        mod_s = _mod_table(mods[layer, 1:1 + DEC_BATCH])
        i = layer // 2
        if layer % 2 == 0:
            ep = {'conv_w': dn_conv_w[i], 'a_log': dn_a_log[i], 'dt_bias': dn_dt_bias[i],
                  'norm_w': dn_norm_w[i], 'sink': b_sink[i]}
            qkv_w, z_w, a_w, bb_w, bq_w, bk_w, bv_w = split_cols(w_in_e[i].astype(BF16), EVEN_SPLITS)
            plain_w = [qkv_w, z_w, _pad_cols(jnp.concatenate([a_w, bb_w], axis=1), LANES)]
            att_w = [bq_w, bk_w, bv_w]
            w_out = w_out_e[i].astype(BF16)
            wo_a, wo_b = w_out[:HA * DK], w_out[HA * DK:]
            proj_p = inproj_pallas(xp, mod_p, norm_w[layer, 0], plain_w, att_w, BATCH * SEQ, keep_kv=True)
            proj_s = inproj_pallas(xs, mod_s, norm_w[layer, 0], plain_w, att_w, DEC_SEQ, rope_tabs=rope_tabs)
            ya_p, yb_p, dn, bk, bv = even_mixer_context(proj_p, ep)
            ya_s, yb_s = even_mixer_latent(proj_s, ep, state_dn[:, i], cache_b_k[:, i], cache_b_v[:, i])
            new_dn.append(dn)
            new_bk.append(bk)
            new_bv.append(bv)
        else:
            op = {'q_norm': c_q_norm[i], 'k_norm': c_k_norm[i], 'lam_re': s5_lam_re[i],
                  'lam_im': s5_lam_im[i], 'log_dt': s5_log_dt[i], 'b_re': s5_b_re[i], 'b_im': s5_b_im[i],
                  'c_re': s5_c_re[i], 'c_im': s5_c_im[i], 'd': s5_d[i], 'glu_w': s5_glu_w[i],
                  'glu_b': s5_glu_b[i]}
            cq_w, ck_w, cv_w, u_w = split_cols(w_in_o[i].astype(BF16), ODD_SPLITS)
            norms = (op['q_norm'], op['k_norm'])
            w_out = w_out_o[i].astype(BF16)
            wo_a, wo_b = w_out[:HC * HEAD_DIM], w_out[HC * HEAD_DIM:]
            proj_p = inproj_pallas(xp, mod_p, norm_w[layer, 0], [u_w], [cq_w, ck_w, cv_w], BATCH * SEQ,
                                   norm_ws=norms, keep_kv=True)
            proj_s = inproj_pallas(xs, mod_s, norm_w[layer, 0], [u_w], [cq_w, ck_w, cv_w], DEC_SEQ,
                                   norm_ws=norms, rope_tabs=rope_tabs)
            ya_p, yb_p, ck, cv, s5 = odd_mixer_context(proj_p, op)
            ya_s, yb_s = odd_mixer_latent(proj_s, op, cache_c_k[:, i], cache_c_v[:, i], state_s5[:, i])
            new_ck.append(ck)
            new_cv.append(cv)
            new_s5.append(s5)
        wr = _router_weights(moe_rg[layer], moe_re[layer])
        br = _pad_cols(jnp.concatenate([moe_rg_b[layer], moe_re_b[layer]])[None, :], ROUTER_LANES)
        moe_w = (wo_a, wo_b, wr, br, moe_wg[layer].astype(BF16), moe_wu[layer].astype(BF16),
                 moe_wd[layer].astype(BF16))
        last = layer == DEPTH - 1
        xp = moe_pallas(xp, ya_p, yb_p, mod_p, norm_w[layer, 1], *moe_w, final_norm_w, BATCH * SEQ, last)
        xs = moe_pallas(xs, ya_s, yb_s, mod_s, norm_w[layer, 1], *moe_w, final_norm_w, DEC_SEQ, last)
    y_prompt = xp.reshape(x_prompt.shape)
    y_sample = xs.reshape(x_sample.shape)
    return (y_prompt, y_sample, jnp.stack(new_dn, axis=1), jnp.stack(new_bk, axis=1), jnp.stack(new_bv, axis=1),
            jnp.stack(new_ck, axis=1), jnp.stack(new_cv, axis=1), jnp.stack(new_s5, axis=1))
```

```python
import functools

import numpy as np
import jax
import jax.numpy as jnp
from jax import lax
from jax.experimental import pallas as pl
from jax.experimental.pallas import tpu as pltpu

D_MODEL = 1024
BATCH = 32
SEQ = 256
DEPTH = 2
DEC_BATCH = 8
DEC_SEQ = 2048
PAST_LEN = 256

GRID_W = 64
HEAD_DIM = 64
BLK = 128
ROPE_THETA = 10000.0
EPS = 1e-6
N_EVEN = (DEPTH + 1) // 2
N_ODD = DEPTH // 2
HA = D_MODEL // 128
DK = HEAD_DIM
CONV_W = 5
DN_CHUNK = 64
HB = D_MODEL // 128
HKV_B = HB // 4
WINDOW = 128
HC = (3 * D_MODEL // 4) // HEAD_DIM
HKV_C = HC // 3
D_S5 = D_MODEL // 4
S5_CH = 16
G_D = D_S5 // S5_CH
S5_P = 64
N_GROUPS = 4
EXP_PER_GROUP = 8
N_EXP = N_GROUPS * EXP_PER_GROUP
EXPERT_FF = D_MODEL // 8
TOP_K = 2

EVEN_SPLITS = (3 * HA * DK, HA * DK, 2 * HA, 2 * HA, HB * HEAD_DIM, HKV_B * HEAD_DIM, HKV_B * HEAD_DIM)
D_IN_EVEN = sum(EVEN_SPLITS)
D_MIX_EVEN = HA * DK + HB * HEAD_DIM
ODD_SPLITS = (HC * HEAD_DIM, HKV_C * HEAD_DIM, HKV_C * HEAD_DIM, D_S5)
D_IN_ODD = sum(ODD_SPLITS)
D_MIX_ODD = HC * HEAD_DIM + D_S5

F32 = jnp.float32
BF16 = jnp.bfloat16


def split_cols(x, sizes):
    idx = np.cumsum(sizes)[:-1].tolist()
    return jnp.split(x, idx, axis=-1)


def axial_rope_tables(rows):
    row = jnp.repeat(jnp.arange(rows), GRID_W).astype(F32)
    col = jnp.tile(jnp.arange(GRID_W), rows).astype(F32)
    quarter = HEAD_DIM // 4
    freqs = ROPE_THETA ** (-jnp.arange(quarter, dtype=F32) / quarter)
    ang_r = row[:, None] * freqs
    ang_c = col[:, None] * freqs
    ang = jnp.concatenate([ang_r, ang_r, ang_c, ang_c], axis=-1)
    return jnp.cos(ang), jnp.sin(ang)


SUBLANES = 8
VMEM_LIMIT = 56 * 1024 * 1024
S5_ROWS = 1024
S5_STATE = G_D * S5_P


def _s5_scan_kernel(u_ref, bmat_ref, cmat_ref, a_ref, s0_ref, y_ref, sfin_ref, xs_ref, st_ref, *, nblk, bsz):
    dr = pl.program_id(0)
    blk = pl.program_id(1)

    @pl.when(blk == 0)
    def _():
        st_ref[...] = s0_ref[0]

    xs_ref[...] = jnp.dot(u_ref[...].astype(BF16), bmat_ref[0], preferred_element_type=F32)
    a_re = jnp.broadcast_to(a_ref[0, 0:1, :], (SUBLANES, S5_STATE))
    a_im = jnp.broadcast_to(a_ref[0, 1:2, :], (SUBLANES, S5_STATE))
    steps = S5_ROWS // bsz
    for sg in range(bsz // SUBLANES):
        rows = pl.ds(sg * SUBLANES, SUBLANES)

        def body(i, carry, sg=sg):
            x_re, x_im = carry
            l = i + dr * (steps - 1 - 2 * i)
            r = pl.ds(pl.multiple_of(l * bsz + sg * SUBLANES, SUBLANES), SUBLANES)
            n_re = a_re * x_re - a_im * x_im + xs_ref[r, 0:S5_STATE]
            n_im = a_re * x_im + a_im * x_re + xs_ref[r, S5_STATE:2 * S5_STATE]
            xs_ref[r, 0:S5_STATE] = n_re
            xs_ref[r, S5_STATE:2 * S5_STATE] = n_im
            return n_re, n_im

        x_re, x_im = lax.fori_loop(0, steps, body, (st_ref[rows, 0:S5_STATE], st_ref[rows, S5_STATE:2 * S5_STATE]),
                                   unroll=4)
        st_ref[rows, 0:S5_STATE] = x_re
        st_ref[rows, S5_STATE:2 * S5_STATE] = x_im
    y_ref[0] = jnp.dot(xs_ref[...].astype(BF16), cmat_ref[0], preferred_element_type=F32)

    @pl.when(blk == nblk - 1)
    def _():
        sfin_ref[0] = st_ref[...]


def _s5_out_kernel(yf_ref, yb_ref, u_ref, d_ref, w_ref, b_ref, o_ref):
    y = yf_ref[0] + yb_ref[0] + u_ref[...] * d_ref[...]
    y = jax.nn.gelu(y)
    gate = jnp.dot(y.astype(BF16), w_ref[...], preferred_element_type=F32) + b_ref[...]
    o_ref[...] = y * jax.nn.sigmoid(gate)


def s5_operators(op):
    lam_re, lam_im = op['lam_re'].astype(F32), op['lam_im'].astype(F32)
    dt = jnp.exp(op['log_dt'].astype(F32))[..., None]
    mag = jnp.exp(lam_re * dt)
    a_re, a_im = mag * jnp.cos(lam_im * dt), mag * jnp.sin(lam_im * dt)
    den = lam_re * lam_re + lam_im * lam_im
    k_re = ((a_re - 1) * lam_re + a_im * lam_im) / den
    k_im = (a_im * lam_re - (a_re - 1) * lam_im) / den
    b_re, b_im = op['b_re'].astype(F32), op['b_im'].astype(F32)
    bb_re = k_re[..., None] * b_re - k_im[..., None] * b_im
    bb_im = k_re[..., None] * b_im + k_im[..., None] * b_re
    eye = jnp.eye(G_D, dtype=F32)

    def b_blocks(t):
        return jnp.einsum('dgpc,gh->dgchp', t, eye).reshape(2, D_S5, S5_STATE)

    def c_blocks(t):
        return jnp.einsum('dgcp,gh->dgphc', t, eye).reshape(2, S5_STATE, D_S5)

    b_blk = jnp.concatenate([b_blocks(bb_re), b_blocks(bb_im)], axis=-1).astype(BF16)
    c_blk = jnp.concatenate([c_blocks(op['c_re'].astype(F32)), -c_blocks(op['c_im'].astype(F32))],
                            axis=1).astype(BF16)
    a_vec = jnp.stack([a_re.reshape(2, S5_STATE), a_im.reshape(2, S5_STATE)], axis=1)
    return b_blk, c_blk, a_vec


def s5_pallas(u, op, ops, s0):
    b, l, _ = u.shape
    b_blk, c_blk, a_vec = ops
    n = b * l
    nblk = n // S5_ROWS
    u_t = jnp.swapaxes(u, 0, 1).reshape(n, D_S5)

    def rows_map(dr, blk):
        return (blk + dr * (nblk - 1 - 2 * blk), 0)

    y2, s_fin = pl.pallas_call(
        functools.partial(_s5_scan_kernel, nblk=nblk, bsz=b),
        grid=(2, nblk),
        in_specs=[
            pl.BlockSpec((S5_ROWS, D_S5), rows_map),
            pl.BlockSpec((1, D_S5, 2 * S5_STATE), lambda dr, blk: (dr, 0, 0)),
            pl.BlockSpec((1, 2 * S5_STATE, D_S5), lambda dr, blk: (dr, 0, 0)),
            pl.BlockSpec((1, 2, S5_STATE), lambda dr, blk: (dr, 0, 0)),
            pl.BlockSpec((1, b, 2 * S5_STATE), lambda dr, blk: (dr, 0, 0)),
        ],
        out_specs=[
            pl.BlockSpec((1, S5_ROWS, D_S5), lambda dr, blk: (dr,) + rows_map(dr, blk)),
            pl.BlockSpec((1, b, 2 * S5_STATE), lambda dr, blk: (dr, 0, 0)),
        ],
        out_shape=[jax.ShapeDtypeStruct((2, n, D_S5), F32), jax.ShapeDtypeStruct((2, b, 2 * S5_STATE), F32)],
        scratch_shapes=[pltpu.VMEM((S5_ROWS, 2 * S5_STATE), F32), pltpu.VMEM((b, 2 * S5_STATE), F32)],
        compiler_params=pltpu.CompilerParams(dimension_semantics=("arbitrary", "arbitrary"),
                                             vmem_limit_bytes=VMEM_LIMIT),
        name="s5_scan",
    )(u_t, b_blk, c_blk, a_vec, s0)
    tm = S5_ROWS
    y = pl.pallas_call(
        _s5_out_kernel,
        grid=(n // tm,),
        in_specs=[
            pl.BlockSpec((1, tm, D_S5), lambda i: (0, i, 0)),
            pl.BlockSpec((1, tm, D_S5), lambda i: (1, i, 0)),
            pl.BlockSpec((tm, D_S5), lambda i: (i, 0)),
            pl.BlockSpec((1, D_S5), lambda i: (0, 0)),
            pl.BlockSpec((D_S5, D_S5), lambda i: (0, 0)),
            pl.BlockSpec((1, D_S5), lambda i: (0, 0)),
        ],
        out_specs=pl.BlockSpec((tm, D_S5), lambda i: (i, 0)),
        out_shape=jax.ShapeDtypeStruct((n, D_S5), F32),
        name="s5_out",
    )(y2, y2, u_t, op['d'].astype(F32).reshape(1, D_S5), op['glu_w'].astype(BF16),
      op['glu_b'].astype(F32).reshape(1, D_S5))
    return jnp.swapaxes(y.reshape(l, b, D_S5), 0, 1), s_fin


LANES = 128
DN_PAIRS = HA * DK // LANES
NEG_BIG = -1e30
DN_UNROLL = 8
DN_BLOCK = DN_CHUNK // 4


def _nt_dot(a, b):
    return lax.dot_general(a, b, (((1,), (1,)), ((), ())), preferred_element_type=F32)


def _bf16_dot(a, b):
    return jnp.dot(a.astype(BF16), b.astype(BF16), preferred_element_type=F32)


def _select_dot(x, sel):
    x1 = x.astype(BF16)
    r1 = x - x1.astype(F32)
    x2 = r1.astype(BF16)
    x3 = (r1 - x2.astype(F32)).astype(BF16)
    dot = functools.partial(jnp.dot, preferred_element_type=F32)
    return dot(x1, sel) + dot(x2, sel) + dot(x3, sel)


def _head_sums(x):
    lane = lax.broadcasted_iota(jnp.int32, x.shape, 1)
    lo = lane < HEAD_DIM
    s0 = jnp.sum(jnp.where(lo, x, 0.0), axis=-1, keepdims=True)
    s1 = jnp.sum(jnp.where(lo, 0.0, x), axis=-1, keepdims=True)
    return jnp.where(lo, s0, s1)


def _chunk_cumsum(x, reverse):
    n = x.shape[0]
    pos = lax.broadcasted_iota(jnp.int32, x.shape, 0) % DN_CHUNK
    s = 1
    while s < DN_CHUNK:
        if reverse:
            x = x + jnp.where(pos < DN_CHUNK - s, pltpu.roll(x, n - s, 0), 0.0)
        else:
            x = x + jnp.where(pos >= s, pltpu.roll(x, s, 0), 0.0)
        s *= 2
    return x


def _short_conv_silu(x, w):
    n = x.shape[0]
    row = lax.broadcasted_iota(jnp.int32, x.shape, 0)
    pad = CONV_W // 2
    y = x * w[pad:pad + 1, :]
    for t in range(CONV_W):
        s = t - pad
        if s == 0:
            continue
        shifted = pltpu.roll(x, (-s) % n, 0)
        valid = (row + s >= 0) & (row + s < n)
        y = y + jnp.where(valid, shifted, 0.0) * w[t:t + 1, :]
    return jax.nn.silu(y)


def _unit_tri_solve(ms, rhss, ri, ci):
    assert DN_CHUNK == 4 * DN_BLOCK
    n = rhss[0].shape[-1]
    same_block = (ri // DN_BLOCK) == (ci // DN_BLOCK)
    dot = functools.partial(jnp.dot, preferred_element_type=F32)

    def apply(p, r):
        hi = r.astype(BF16)
        lo = (r - hi.astype(F32)).astype(BF16)
        x = dot(p, jnp.concatenate([hi, lo], axis=-1))
        return x[:, 0:n] + x[:, n:2 * n]

    diag = [jnp.where(same_block, m, 0.0) for m in ms]
    rest = [jnp.where(same_block, 0.0, m).astype(BF16) for m in ms]
    pows = [d.astype(BF16) for d in diag]
    tinv = [jnp.where(ri == ci, 1.0, 0.0) - d for d in diag]
    power = 2
    while power < DN_BLOCK:
        pows = [dot(p, p).astype(BF16) for p in pows]
        tinv = [t + dot(t.astype(BF16), p) for t, p in zip(tinv, pows)]
        power *= 2
    tinv = [t.astype(BF16) for t in tinv]
    p1 = [dot(t, e).astype(BF16) for t, e in zip(tinv, rest)]
    sols = [apply(t, r) for t, r in zip(tinv, rhss)]
    p2 = [dot(p, p).astype(BF16) for p in p1]
    sols = [s - apply(p, s) for p, s in zip(p1, sols)]
    return [s + apply(p, s) for p, s in zip(p2, sols)]


def _deltanet_kernel(q_ref, k_ref, v_ref, z_ref, a_ref, bb_ref, cwq_ref, cwk_ref, cwv_ref, alog_ref, dtb_ref, nw_ref,
                     s0_ref, y_ref, sfin_ref,
                     qs_ref, ks_ref, vs_ref, o_ref, gc_ref, beta_ref, u_ref, w_ref, attn_ref, qp_ref, kpt_ref,
                     etot_ref, *, seq):
    hp = pl.program_id(1)
    nchunk = seq // DN_CHUNK
    unroll = min(DN_UNROLL, nchunk)
    c64 = DN_CHUNK
    head_lanes = [slice(j * HEAD_DIM, (j + 1) * HEAD_DIM) for j in range(2)]

    q = _short_conv_silu(q_ref[0], cwq_ref[...])
    q = q * lax.rsqrt(_head_sums(q * q) + EPS) * (DK ** -0.5)
    k = _short_conv_silu(k_ref[0], cwk_ref[...])
    k = k * lax.rsqrt(_head_sums(k * k) + EPS)
    v = _short_conv_silu(v_ref[0], cwv_ref[...])
    qs_ref[...] = q
    ks_ref[...] = k
    vs_ref[...] = v
    o_ref[...] = jnp.zeros_like(o_ref)

    g_col = -jnp.exp(alog_ref[...]) * jax.nn.softplus(a_ref[0] + dtb_ref[...])
    b_col = jax.nn.sigmoid(bb_ref[0])
    sel_row = lax.broadcasted_iota(jnp.int32, (2 * HA, 2 * LANES), 0)
    sel_lane = lax.broadcasted_iota(jnp.int32, (2 * HA, 2 * LANES), 1)
    sel = (sel_row == (sel_lane // LANES) * HA + 2 * hp + (sel_lane // HEAD_DIM) % 2).astype(BF16)
    g_b = _select_dot(g_col, sel)
    gc_ref[:, 0:LANES] = _chunk_cumsum(g_b[:, 0:LANES], reverse=False)
    gc_ref[:, LANES:2 * LANES] = _chunk_cumsum(g_b[:, LANES:2 * LANES], reverse=True)
    beta_ref[...] = _select_dot(b_col, sel)

    ri = lax.broadcasted_iota(jnp.int32, (c64, c64), 0)
    ci = lax.broadcasted_iota(jnp.int32, (c64, c64), 1)

    def intra_chunk(step, carry):
        chunk_ids = [step * unroll + t for t in range(unroll)]
        rows_c = [pl.ds(pl.multiple_of(c * c64, c64), c64) for c in chunk_ids]
        loads = [(qs_ref[r, :], ks_ref[r, :], vs_ref[r, :], gc_ref[r, :], beta_ref[r, :]) for r in rows_c]
        heads = [(t, j, ld[0][:, head_lanes[j]], ld[1][:, head_lanes[j]], ld[2][:, head_lanes[j]])
                 for t, ld in enumerate(loads) for j in range(2)]
        gate_lanes = [slice((c % 2) * LANES + (c // 2) * HEAD_DIM, (c % 2) * LANES + (c // 2 + 1) * HEAD_DIM)
                      for c in range(4)]
        gates = {(t, c): (ld[3][:, gate_lanes[c]], ld[4][:, gate_lanes[c]])
                 for t, ld in enumerate(loads) for c in range(4)}
        kbs = [h[3].astype(BF16) for h in heads]
        kks = [_nt_dot(kb, kb) for kb in kbs]
        qks = [_nt_dot(h[2].astype(BF16), kb) for h, kb in zip(heads, kbs)]
        chains, ms, rhss = [], [], []
        for (t, j, qj, kj, vj), kk, qk in zip(heads, kks, qks):
            for d in range(2):
                gcb, bet = gates[(t, 2 * j + d)]
                diff = gcb - gcb.T
                earlier = (ri >= ci) if d == 0 else (ri <= ci)
                strict = (ri > ci) if d == 0 else (ri < ci)
                dec = jnp.exp(jnp.where(earlier, diff, NEG_BIG))
                eg = jnp.exp(gcb)
                gtot = gcb[c64 - 1:c64, :] if d == 0 else gcb[0:1, :]
                ms.append(jnp.where(strict, kk * bet * dec, 0.0))
                rhss.append(jnp.concatenate([vj * bet, kj * bet * eg], axis=-1))
                chains.append((t, 2 * j + d, qk * dec, qj * eg, (kj * jnp.exp(gtot - gcb)).T,
                               jnp.broadcast_to(jnp.exp(gtot), (SUBLANES, HEAD_DIM))))
        uws = _unit_tri_solve(ms, rhss, ri, ci)
        for (t, c, attn, qp, kpt, etot), uw in zip(chains, uws):
            u_ref[c, rows_c[t], :] = uw[:, 0:HEAD_DIM]
            w_ref[c, rows_c[t], :] = uw[:, HEAD_DIM:2 * HEAD_DIM].astype(BF16)
            attn_ref[c, rows_c[t], :] = attn.astype(BF16)
            qp_ref[c, rows_c[t], :] = qp.astype(BF16)
            kpt_ref[c, rows_c[t], :] = kpt.astype(BF16)
            etot_ref[c, pl.ds(pl.multiple_of(chunk_ids[t] * SUBLANES, SUBLANES), SUBLANES), :] = etot
        return carry

    lax.fori_loop(0, nchunk // unroll, intra_chunk, 0)

    def inter_chunk(i, states):
        rows_d = [pl.ds(pl.multiple_of(cc * c64, c64), c64) for cc in (i, nchunk - 1 - i)]
        erow_d = [pl.ds(pl.multiple_of(cc * SUBLANES, SUBLANES), SUBLANES) for cc in (i, nchunk - 1 - i)]
        loaded = [(w_ref[c, rows_d[c % 2], :], qp_ref[c, rows_d[c % 2], :], u_ref[c, rows_d[c % 2], :],
                   attn_ref[c, rows_d[c % 2], :], kpt_ref[c, rows_d[c % 2], :], etot_ref[c, erow_d[c % 2], :],
                   o_ref[c // 2, rows_d[c % 2], :]) for c in range(4)]
        ws_qs = [jnp.dot(jnp.concatenate([ld[0], ld[1]], axis=0), s.astype(BF16), preferred_element_type=F32)
                 for ld, s in zip(loaded, states)]
        v_new = [(ld[2] - r[0:c64]).astype(BF16) for ld, r in zip(loaded, ws_qs)]
        av = [jnp.dot(ld[3], vn, preferred_element_type=F32) for ld, vn in zip(loaded, v_new)]
        kv = [jnp.dot(ld[4], vn, preferred_element_type=F32) for ld, vn in zip(loaded, v_new)]
        for c in range(4):
            o_ref[c // 2, rows_d[c % 2], :] = loaded[c][6] + ws_qs[c][c64:2 * c64] + av[c]
        return tuple(s * ld[5][0:1, :] + x for s, ld, x in zip(states, loaded, kv))

    init = tuple(s0_ref[0, c % 2, c // 2] for c in range(4))
    fin = lax.fori_loop(0, nchunk, inter_chunk, init)
    for c in range(4):
        sfin_ref[0, c % 2, c // 2] = fin[c]

    o = jnp.concatenate([o_ref[0], o_ref[1]], axis=-1)
    o = o * lax.rsqrt(_head_sums(o * o) * (1.0 / DK) + EPS) * nw_ref[...]
    y_ref[0] = o * jax.nn.silu(z_ref[0])


def deltanet_pallas(qkv, z, a, bb, ep, s0):
    b, l, _ = qkv.shape
    npair = DN_PAIRS

    def col_spec(off):
        return pl.BlockSpec((1, l, LANES), lambda i, hp: (i, 0, off + hp))

    def cw_spec(off):
        return pl.BlockSpec((CONV_W, LANES), lambda i, hp: (0, off + hp))

    def full(shape):
        return pl.BlockSpec(shape, lambda i, hp: (0,) * len(shape))

    state_spec = pl.BlockSpec((1, 2, 2, DK, DK), lambda i, hp: (i, 0, hp, 0, 0))
    row_f32 = pltpu.VMEM((l, LANES), F32)
    gate_f32 = pltpu.VMEM((l, 2 * LANES), F32)
    chain_f32 = pltpu.VMEM((4, l, HEAD_DIM), F32)
    chain_bf16 = pltpu.VMEM((4, l, HEAD_DIM), BF16)
    y, s_fin = pl.pallas_call(
        functools.partial(_deltanet_kernel, seq=l),
        grid=(b, npair),
        in_specs=[col_spec(0), col_spec(npair), col_spec(2 * npair), col_spec(0),
                  pl.BlockSpec((1, l, 2 * HA), lambda i, hp: (i, 0, 0)),
                  pl.BlockSpec((1, l, 2 * HA), lambda i, hp: (i, 0, 0)),
                  cw_spec(0), cw_spec(npair), cw_spec(2 * npair),
                  full((1, 2 * HA)), full((1, 2 * HA)), full((1, LANES)), state_spec],
        out_specs=[col_spec(0), state_spec],
        out_shape=[jax.ShapeDtypeStruct((b, l, HA * DK), F32), jax.ShapeDtypeStruct((b, 2, HA, DK, DK), F32)],
        scratch_shapes=[row_f32, row_f32, row_f32, pltpu.VMEM((2, l, HEAD_DIM), F32), gate_f32, gate_f32,
                        chain_f32, chain_bf16, chain_bf16, chain_bf16, chain_bf16,
                        pltpu.VMEM((4, l // DN_CHUNK * SUBLANES, HEAD_DIM), F32)],
        compiler_params=pltpu.CompilerParams(dimension_semantics=("arbitrary", "arbitrary"),
                                             vmem_limit_bytes=VMEM_LIMIT),
        name="deltanet",
    )(qkv, qkv, qkv, z, a, bb, ep['conv_w'], ep['conv_w'], ep['conv_w'],
      ep['a_log'].astype(F32).reshape(1, 2 * HA), ep['dt_bias'].astype(F32).reshape(1, 2 * HA),
      jnp.tile(ep['norm_w'].astype(F32), 2).reshape(1, LANES), s0)
    return y, s_fin


TM = 512
ATT_TQ = 128
ROT = HEAD_DIM // 4


def rope_tables():
    cos, sin = axial_rope_tables(DEC_SEQ // GRID_W)
    cos2 = jnp.tile(cos, (1, LANES // HEAD_DIM))
    sin2 = jnp.tile(sin, (1, LANES // HEAD_DIM))
    first = (jnp.arange(LANES) % (2 * ROT)) < ROT
    return cos2, jnp.where(first, -sin2, 0.0), jnp.where(first, 0.0, sin2)


def _head_prep(x, w_ref, tabs, scale):
    outs, normed = [], []
    for c in range(x.shape[1] // LANES):
        xc = x[:, c * LANES:(c + 1) * LANES]
        if w_ref is not None:
            xc = xc * lax.rsqrt(_head_sums(xc * xc) * (1.0 / HEAD_DIM) + EPS) * w_ref[...]
            normed.append(xc)
        if tabs is not None:
            cos_ref, sup_ref, sdn_ref = tabs
            xc = (xc * cos_ref[...] + pltpu.roll(xc, LANES - ROT, 1) * sup_ref[...]
                  + pltpu.roll(xc, ROT, 1) * sdn_ref[...])
        outs.append((xc * scale).astype(BF16))
    cat = lambda parts: jnp.concatenate(parts, axis=-1) if len(parts) > 1 else parts[0]
    return cat(outs), (cat(normed) if w_ref is not None else None)


def _attn_kernel(q_ref, k_ref, v_ref, *refs, grp, hk, windowed, has_ctx, has_sink):
    refs = list(refs)
    kc_ref, vc_ref = (refs.pop(0), refs.pop(0)) if has_ctx else (None, None)
    sink_ref = refs.pop(0) if has_sink else None
    o_ref = refs[0]
    tq = q_ref.shape[1]
    seq_k = k_ref.shape[1]
    if windowed:
        qi = pl.program_id(1)
        nkeys = tq + 2 * WINDOW
        start = jnp.clip(qi * tq - WINDOW, 0, seq_k - nkeys)
        krows = pl.ds(pl.multiple_of(start, WINDOW), nkeys)
        qpos = qi * tq + lax.broadcasted_iota(jnp.int32, (grp * tq, nkeys), 0) % tq
        kpos = start + lax.broadcasted_iota(jnp.int32, (grp * tq, nkeys), 1)
        keep = jnp.abs(qpos - kpos) <= WINDOW
    else:
        krows = pl.ds(0, seq_k)
    q = q_ref[0]
    groups = range(hk)
    lanes = [slice(g * HEAD_DIM, (g + 1) * HEAD_DIM) for g in groups]
    qgs = [jnp.concatenate([q[:, (g * grp + j) * HEAD_DIM:(g * grp + j + 1) * HEAD_DIM] for j in range(grp)], axis=0)
           for g in groups]
    ss = [_nt_dot(qg, k_ref[0, krows, gl]) for qg, gl in zip(qgs, lanes)]
    if windowed:
        ss = [jnp.where(keep, s, NEG_BIG) for s in ss]
    ms = [jnp.max(s, axis=-1, keepdims=True) for s in ss]
    if has_ctx:
        kc = kc_ref[0].astype(BF16)
        scs = [_nt_dot(qg, kc[:, gl]) for qg, gl in zip(qgs, lanes)]
        ms = [jnp.maximum(m, jnp.max(sc, axis=-1, keepdims=True)) for m, sc in zip(ms, scs)]
    if has_sink:
        head_rows = [slice(j * tq, (j + 1) * tq) for j in range(grp)]
        ms = [jnp.concatenate([jnp.maximum(m[head_rows[j]], sink_ref[g * grp + j]) for j in range(grp)], axis=0)
              for g, m in zip(groups, ms)]
    ps = [jnp.exp(s - m) for s, m in zip(ss, ms)]
    dens = [jnp.sum(p, axis=-1, keepdims=True) for p in ps]
    os_ = [jnp.dot(p.astype(BF16), v_ref[0, krows, gl], preferred_element_type=F32) for p, gl in zip(ps, lanes)]
    if has_ctx:
        vc = vc_ref[0].astype(BF16)
        pcs = [jnp.exp(sc - m) for sc, m in zip(scs, ms)]
        dens = [den + jnp.sum(pc, axis=-1, keepdims=True) for den, pc in zip(dens, pcs)]
        os_ = [o + jnp.dot(pc.astype(BF16), vc[:, gl], preferred_element_type=F32)
               for o, pc, gl in zip(os_, pcs, lanes)]
    if has_sink:
        dens = [den + jnp.concatenate([jnp.exp(sink_ref[g * grp + j] - m[head_rows[j]]) for j in range(grp)], axis=0)
                for g, den, m in zip(groups, dens, ms)]
    outs = [None] * (grp * hk)
    for g in groups:
        o = os_[g] / dens[g]
        for j in range(grp):
            outs[g * grp + j] = o[j * tq:(j + 1) * tq]
    o_ref[0] = jnp.concatenate(outs, axis=-1)


def attention_pallas(q, k, v, ctx=None, sink=None, windowed=False):
    b, lq, hd = q.shape
    lk, kd = k.shape[1], k.shape[2]
    hk = kd // HEAD_DIM
    grp = hd // kd
    tq = ATT_TQ if lq > 2 * ATT_TQ and not windowed else 2 * ATT_TQ
    ins = [q, k, v]
    in_specs = [pl.BlockSpec((1, tq, hd), lambda i, t: (i, t, 0)),
                pl.BlockSpec((1, lk, kd), lambda i, t: (i, 0, 0)),
                pl.BlockSpec((1, lk, kd), lambda i, t: (i, 0, 0))]
    if ctx is not None:
        ins += list(ctx)
        in_specs += [pl.BlockSpec((1, ctx[0].shape[1], kd), lambda i, t: (i, 0, 0))] * 2
    if sink is not None:
        ins.append(sink.astype(F32))
        in_specs.append(pl.BlockSpec(memory_space=pltpu.SMEM))
    return pl.pallas_call(
        functools.partial(_attn_kernel, grp=grp, hk=hk, windowed=windowed, has_ctx=ctx is not None,
                          has_sink=sink is not None),
        grid=(b, lq // tq),
        in_specs=in_specs,
        out_specs=pl.BlockSpec((1, tq, hd), lambda i, t: (i, t, 0)),
        out_shape=jax.ShapeDtypeStruct((b, lq, hd), F32),
        compiler_params=pltpu.CompilerParams(dimension_semantics=("arbitrary", "arbitrary"),
                                             vmem_limit_bytes=VMEM_LIMIT),
        name="attention",
    )(*ins)


def even_mixer_context(proj, ep):
    qkv, z, ab, qh, kh, vh, bk, bv = proj
    qkv, z, ab = [t.reshape(BATCH, SEQ, -1) for t in (qkv, z, ab)]
    a, bb = ab[..., 0:2 * HA], ab[..., 2 * HA:4 * HA]
    y_a, dn_fin = deltanet_pallas(qkv, z, a, bb, ep, jnp.zeros((BATCH, 2, HA, DK, DK), F32))
    y_b = attention_pallas(*[t.reshape(BATCH, SEQ, -1) for t in (qh, kh, vh)], sink=ep['sink'])
    return (y_a.reshape(-1, HA * DK), y_b.reshape(-1, HB * HEAD_DIM), dn_fin,
            bk.reshape(BATCH, SEQ, HKV_B, HEAD_DIM), bv.reshape(BATCH, SEQ, HKV_B, HEAD_DIM))


def even_mixer_latent(proj, ep, dn_state, ctx_k, ctx_v):
    qkv, z, ab, qh, kh, vh = proj
    qkv, z, ab = [t.reshape(DEC_BATCH, DEC_SEQ, -1) for t in (qkv, z, ab)]
    a, bb = ab[..., 0:2 * HA], ab[..., 2 * HA:4 * HA]
    y_a, _ = deltanet_pallas(qkv, z, a, bb, ep, dn_state.astype(F32))
    ctx = (ctx_k.reshape(DEC_BATCH, PAST_LEN, -1), ctx_v.reshape(DEC_BATCH, PAST_LEN, -1))
    y_b = attention_pallas(*[t.reshape(DEC_BATCH, DEC_SEQ, -1) for t in (qh, kh, vh)], ctx=ctx, sink=ep['sink'],
                           windowed=True)
    return y_a.reshape(-1, HA * DK), y_b.reshape(-1, HB * HEAD_DIM)


def odd_mixer_context(proj, op):
    u, qh, kh, vh, k_normed, cv = proj
    y_c = attention_pallas(*[t.reshape(BATCH, SEQ, -1) for t in (qh, kh, vh)])
    y_d, s_fin = s5_pallas(u.reshape(BATCH, SEQ, -1), op, s5_operators(op), jnp.zeros((2, BATCH, 2 * S5_STATE), F32))
    s_fin = jnp.transpose(s_fin.reshape(2, BATCH, 2, G_D, S5_P), (1, 0, 3, 4, 2))
    return (y_c.reshape(-1, HC * HEAD_DIM), y_d.reshape(-1, D_S5), k_normed.reshape(BATCH, SEQ, HKV_C, HEAD_DIM),
            cv.reshape(BATCH, SEQ, HKV_C, HEAD_DIM), s_fin)


def odd_mixer_latent(proj, op, ctx_k, ctx_v, s5_state):
    u, qh, kh, vh = proj
    ctx = (ctx_k.reshape(DEC_BATCH, PAST_LEN, -1), ctx_v.reshape(DEC_BATCH, PAST_LEN, -1))
    y_c = attention_pallas(*[t.reshape(DEC_BATCH, DEC_SEQ, -1) for t in (qh, kh, vh)], ctx=ctx)
    s0 = jnp.transpose(s5_state.astype(F32), (1, 0, 4, 2, 3)).reshape(2, DEC_BATCH, 2 * S5_STATE)
    y_d, _ = s5_pallas(u.reshape(DEC_BATCH, DEC_SEQ, -1), op, s5_operators(op), s0)
    return y_c.reshape(-1, HC * HEAD_DIM), y_d.reshape(-1, D_S5)


ADA_ROWS = 16
ADA_TN = 1024
MOD_ROWS = 8
ROUTER_LANES = LANES
MOE_FCHUNK = EXP_PER_GROUP * EXPERT_FF


def _ada_kernel(c_ref, w_ref, b_ref, o_ref):
    o_ref[0] = _bf16_dot(jax.nn.silu(c_ref[...]), w_ref[0]) + b_ref[0]


def ada_pallas(cvec, ada_w, ada_b):
    n = 6 * D_MODEL
    return pl.pallas_call(
        _ada_kernel,
        grid=(DEPTH, n // ADA_TN),
        in_specs=[pl.BlockSpec((ADA_ROWS, D_MODEL), lambda l, j: (0, 0)),
                  pl.BlockSpec((1, D_MODEL, ADA_TN), lambda l, j: (l, 0, j)),
                  pl.BlockSpec((1, 1, ADA_TN), lambda l, j: (l, 0, j))],
        out_specs=pl.BlockSpec((1, ADA_ROWS, ADA_TN), lambda l, j: (l, 0, j)),
        out_shape=jax.ShapeDtypeStruct((DEPTH, ADA_ROWS, n), F32),
        name="ada_modulation",
    )(cvec, ada_w, ada_b.reshape(DEPTH, 1, n))


def _modulated(x, nw, shift, scale):
    return x * lax.rsqrt(jnp.mean(x * x, axis=-1, keepdims=True) + EPS) * nw * (1 + scale) + shift


def _inproj_kernel(x_ref, mod_ref, nw_ref, *refs, n_plain, norm, rope, keep_k, keep_v):
    refs = list(refs)
    m = mod_ref[0]
    h = _modulated(x_ref[...], nw_ref[...], m[0:1], m[1:2]).astype(BF16)
    w_refs = [refs.pop(0) for _ in range(n_plain + 3)]
    qw_ref, kw_ref = (refs.pop(0), refs.pop(0)) if norm else (None, None)
    tabs = [refs.pop(0) for _ in range(3)] if rope else None
    plain_refs = [refs.pop(0) for _ in range(n_plain)]
    qo_ref, ko_ref, vo_ref = refs.pop(0), refs.pop(0), refs.pop(0)
    for w_ref, o_ref in zip(w_refs, plain_refs):
        o_ref[...] = jnp.dot(h, w_ref[...], preferred_element_type=F32)
    q, k, v = [jnp.dot(h, w_ref[...], preferred_element_type=F32) for w_ref in w_refs[n_plain:]]
    qo_ref[...] = _head_prep(q, qw_ref, tabs, HEAD_DIM ** -0.5)[0]
    ko, k_normed = _head_prep(k, kw_ref, tabs, 1.0)
    ko_ref[...] = ko
    vo_ref[...] = v.astype(BF16)
    if keep_k:
        refs.pop(0)[...] = k_normed if norm else k
    if keep_v:
        refs.pop(0)[...] = v


def inproj_pallas(x, mod, nw, plain_ws, qkv_ws, rows_per_mod, norm_ws=None, rope_tabs=None, keep_kv=False):
    t = x.shape[0]
    tiles_per_mod = rows_per_mod // TM
    row = lambda i: (i, 0)
    const = lambda i: (0, 0)
    weights = list(plain_ws) + list(qkv_ws)
    ins = [x, mod, nw.reshape(1, D_MODEL)] + weights
    in_specs = [pl.BlockSpec((TM, D_MODEL), row),
                pl.BlockSpec((1, MOD_ROWS, D_MODEL), lambda i: (i // tiles_per_mod, 0, 0)),
                pl.BlockSpec((1, D_MODEL), const)] + [pl.BlockSpec(w.shape, const) for w in weights]
    if norm_ws is not None:
        ins += [jnp.tile(w.astype(F32), LANES // HEAD_DIM).reshape(1, LANES) for w in norm_ws]
        in_specs += [pl.BlockSpec((1, LANES), const)] * 2
    if rope_tabs is not None:
        tiles = DEC_SEQ // TM
        ins += list(rope_tabs)
        in_specs += [pl.BlockSpec((TM, LANES), lambda i: (i % tiles, 0))] * 3
    outs = [(w.shape[1], F32) for w in plain_ws] + [(w.shape[1], BF16) for w in qkv_ws]
    if keep_kv:
        outs += [(qkv_ws[1].shape[1], F32), (qkv_ws[2].shape[1], F32)]
    return pl.pallas_call(
        functools.partial(_inproj_kernel, n_plain=len(plain_ws), norm=norm_ws is not None,
                          rope=rope_tabs is not None, keep_k=keep_kv, keep_v=keep_kv),
        grid=(t // TM,),
        in_specs=in_specs,
        out_specs=[pl.BlockSpec((TM, n), row) for n, _ in outs],
        out_shape=[jax.ShapeDtypeStruct((t, n), dt) for n, dt in outs],
        compiler_params=pltpu.CompilerParams(dimension_semantics=("arbitrary",), vmem_limit_bytes=VMEM_LIMIT),
        name="modulated_in_proj",
    )(*ins)


def _moe_kernel(x_ref, ya_ref, yb_ref, mod_ref, nw_ref, woa_ref, wob_ref, wr_ref, br_ref, wg_ref, wu_ref, wd_ref,
                fw_ref, o_ref, x1_ref, h_ref, comb_ref, acc_ref, *, final):
    f = pl.program_id(1)
    m = mod_ref[0]

    @pl.when(f == 0)
    def _():
        y = _bf16_dot(ya_ref[...], woa_ref[...]) + _bf16_dot(yb_ref[...], wob_ref[...])
        x1 = x_ref[...] + m[2:3] * y
        x1_ref[...] = x1
        h = _modulated(x1, nw_ref[...], m[3:4], m[4:5])
        h_hi = h.astype(BF16)
        h_ref[...] = h_hi
        h_lo = (h - h_hi.astype(F32)).astype(BF16)
        logits = (jnp.dot(jnp.concatenate([h_hi, h_lo], axis=-1), wr_ref[0:2 * D_MODEL, :], preferred_element_type=F32)
                  + jnp.dot(h_hi, wr_ref[2 * D_MODEL:3 * D_MODEL, :], preferred_element_type=F32) + br_ref[...])
        lane = lax.broadcasted_iota(jnp.int32, logits.shape, 1)
        lane_f = lane.astype(F32)
        is_group = lane < N_GROUPS
        lg = jnp.where(is_group, logits, -jnp.inf)
        g_max = jnp.max(lg, axis=-1, keepdims=True)
        g_idx = jnp.min(jnp.where(lg == g_max, lane_f, float(ROUTER_LANES)), axis=-1, keepdims=True)
        gate_g = 1.0 / jnp.sum(jnp.where(is_group, jnp.exp(logits - g_max), 0.0), axis=-1, keepdims=True)
        lane_group = ((lane + (EXP_PER_GROUP - N_GROUPS)) // EXP_PER_GROUP - 1).astype(F32)
        in_group = (lane >= N_GROUPS) & (lane < N_GROUPS + N_EXP) & (lane_group == g_idx)
        le = jnp.where(in_group, logits, -jnp.inf)
        v1 = jnp.max(le, axis=-1, keepdims=True)
        i1 = jnp.min(jnp.where(le == v1, lane_f, float(ROUTER_LANES)), axis=-1, keepdims=True)
        le2 = jnp.where(lane_f == i1, -jnp.inf, le)
        v2 = jnp.max(le2, axis=-1, keepdims=True)
        i2 = jnp.min(jnp.where(le2 == v2, lane_f, float(ROUTER_LANES)), axis=-1, keepdims=True)
        e2 = jnp.exp(v2 - v1)
        p1 = gate_g / (1.0 + e2)
        comb = jnp.where(lane_f == i1, p1, 0.0) + jnp.where(lane_f == i2, p1 * e2, 0.0)
        for grp in range(N_GROUPS):
            comb_ref[grp] = pltpu.roll(comb, ROUTER_LANES - (N_GROUPS + grp * EXP_PER_GROUP), 1)
        acc_ref[...] = jnp.zeros_like(acc_ref)

    h = h_ref[...]
    hid = jax.nn.silu(jnp.dot(h, wg_ref[...], preferred_element_type=F32)) * jnp.dot(
        h, wu_ref[...], preferred_element_type=F32)
    comb = comb_ref[f]
    hid = jnp.concatenate([hid[:, k * EXPERT_FF:(k + 1) * EXPERT_FF] * comb[:, k:k + 1]
                           for k in range(EXP_PER_GROUP)], axis=-1)
    acc_ref[...] += jnp.dot(hid.astype(BF16), wd_ref[...], preferred_element_type=F32)

    @pl.when(f == N_GROUPS - 1)
    def _():
        out = x1_ref[...] + m[5:6] * acc_ref[...]
        if final:
            out = out * lax.rsqrt(jnp.mean(out * out, axis=-1, keepdims=True) + EPS) * fw_ref[...]
        o_ref[...] = out


def moe_pallas(x, ya, yb, mod, nw, wo_a, wo_b, wr, br, wg, wu, wd, final_w, rows_per_mod, final):
    t = x.shape[0]
    tiles_per_mod = rows_per_mod // TM
    row = lambda i, f: (i, 0)
    const = lambda i, f: (0, 0)
    return pl.pallas_call(
        functools.partial(_moe_kernel, final=final),
        grid=(t // TM, N_GROUPS),
        in_specs=[pl.BlockSpec((TM, D_MODEL), row),
                  pl.BlockSpec((TM, ya.shape[1]), row),
                  pl.BlockSpec((TM, yb.shape[1]), row),
                  pl.BlockSpec((1, MOD_ROWS, D_MODEL), lambda i, f: (i // tiles_per_mod, 0, 0)),
                  pl.BlockSpec((1, D_MODEL), const),
                  pl.BlockSpec(wo_a.shape, const),
                  pl.BlockSpec(wo_b.shape, const),
                  pl.BlockSpec((3 * D_MODEL, ROUTER_LANES), const),
                  pl.BlockSpec((1, ROUTER_LANES), const),
                  pl.BlockSpec((D_MODEL, MOE_FCHUNK), lambda i, f: (0, f)),
                  pl.BlockSpec((D_MODEL, MOE_FCHUNK), lambda i, f: (0, f)),
                  pl.BlockSpec((MOE_FCHUNK, D_MODEL), lambda i, f: (f, 0)),
                  pl.BlockSpec((1, D_MODEL), const)],
        out_specs=pl.BlockSpec((TM, D_MODEL), row),
        out_shape=jax.ShapeDtypeStruct((t, D_MODEL), F32),
        scratch_shapes=[pltpu.VMEM((TM, D_MODEL), F32), pltpu.VMEM((TM, D_MODEL), BF16),
                        pltpu.VMEM((N_GROUPS, TM, ROUTER_LANES), F32), pltpu.VMEM((TM, D_MODEL), F32)],
        compiler_params=pltpu.CompilerParams(dimension_semantics=("arbitrary", "arbitrary"),
                                             vmem_limit_bytes=VMEM_LIMIT),
        name="out_proj_moe",
    )(x, ya, yb, mod, nw.reshape(1, D_MODEL), wo_a, wo_b, wr, br, wg, wu, wd, final_w.reshape(1, D_MODEL))


def _pad_cols(w, n):
    return jnp.pad(w, ((0, 0), (0, n - w.shape[1])))


def _router_weights(rg, re):
    w = _pad_cols(jnp.concatenate([rg, re], axis=1).astype(F32), ROUTER_LANES)
    hi = w.astype(BF16)
    lo = (w - hi.astype(F32)).astype(BF16)
    return jnp.concatenate([hi, hi, lo], axis=0)


def _mod_table(m):
    m = m.reshape(m.shape[0], 6, D_MODEL)
    return jnp.pad(m, ((0, 0), (0, MOD_ROWS - 6), (0, 0)))


def kernel(x_prompt, x_sample, c, c_ctx, state_dn, cache_b_k, cache_b_v, cache_c_k, cache_c_v, state_s5,
           ada_w, ada_b, norm_w, w_in_e, dn_conv_w, dn_a_log, dn_dt_bias, dn_norm_w, b_sink, w_out_e,
           w_in_o, c_q_norm, c_k_norm, s5_lam_re, s5_lam_im, s5_log_dt, s5_b_re, s5_b_im, s5_c_re, s5_c_im,
           s5_d, s5_glu_w, s5_glu_b, w_out_o, moe_rg, moe_rg_b, moe_re, moe_re_b, moe_wg, moe_wu, moe_wd,
           final_norm_w):
    rope_tabs = rope_tables()
    cvec = jnp.concatenate([c_ctx[None, :], c, jnp.zeros((ADA_ROWS - 1 - DEC_BATCH, D_MODEL), F32)], axis=0)
    mods = ada_pallas(cvec, ada_w, ada_b)
    xp = x_prompt.reshape(BATCH * SEQ, D_MODEL)
    xs = x_sample.reshape(DEC_BATCH * DEC_SEQ, D_MODEL)
    new_dn, new_bk, new_bv, new_ck, new_cv, new_s5 = [], [], [], [], [], []
    for layer in range(DEPTH):
        mod_p = _mod_table(mods[layer, 0:1])
        mod_s = _mod_table(mods[layer, 1:1 + DEC_BATCH])
        i = layer // 2
        if layer % 2 == 0:
            ep = {'conv_w': dn_conv_w[i], 'a_log': dn_a_log[i], 'dt_bias': dn_dt_bias[i],
                  'norm_w': dn_norm_w[i], 'sink': b_sink[i]}
            qkv_w, z_w, a_w, bb_w, bq_w, bk_w, bv_w = split_cols(w_in_e[i].astype(BF16), EVEN_SPLITS)
            plain_w = [qkv_w, z_w, _pad_cols(jnp.concatenate([a_w, bb_w], axis=1), LANES)]
            att_w = [bq_w, bk_w, bv_w]
            w_out = w_out_e[i].astype(BF16)
            wo_a, wo_b = w_out[:HA * DK], w_out[HA * DK:]
            proj_p = inproj_pallas(xp, mod_p, norm_w[layer, 0], plain_w, att_w, BATCH * SEQ, keep_kv=True)
            proj_s = inproj_pallas(xs, mod_s, norm_w[layer, 0], plain_w, att_w, DEC_SEQ, rope_tabs=rope_tabs)
            ya_p, yb_p, dn, bk, bv = even_mixer_context(proj_p, ep)
            ya_s, yb_s = even_mixer_latent(proj_s, ep, state_dn[:, i], cache_b_k[:, i], cache_b_v[:, i])
            new_dn.append(dn)
            new_bk.append(bk)
            new_bv.append(bv)
        else:
            op = {'q_norm': c_q_norm[i], 'k_norm': c_k_norm[i], 'lam_re': s5_lam_re[i],
                  'lam_im': s5_lam_im[i], 'log_dt': s5_log_dt[i], 'b_re': s5_b_re[i], 'b_im': s5_b_im[i],
                  'c_re': s5_c_re[i], 'c_im': s5_c_im[i], 'd': s5_d[i], 'glu_w': s5_glu_w[i],
                  'glu_b': s5_glu_b[i]}
            cq_w, ck_w, cv_w, u_w = split_cols(w_in_o[i].astype(BF16), ODD_SPLITS)
            norms = (op['q_norm'], op['k_norm'])
            w_out = w_out_o[i].astype(BF16)
            wo_a, wo_b = w_out[:HC * HEAD_DIM], w_out[HC * HEAD_DIM:]
            proj_p = inproj_pallas(xp, mod_p, norm_w[layer, 0], [u_w], [cq_w, ck_w, cv_w], BATCH * SEQ,
                                   norm_ws=norms, keep_kv=True)
            proj_s = inproj_pallas(xs, mod_s, norm_w[layer, 0], [u_w], [cq_w, ck_w, cv_w], DEC_SEQ,
                                   norm_ws=norms, rope_tabs=rope_tabs)
            ya_p, yb_p, ck, cv, s5 = odd_mixer_context(proj_p, op)
            ya_s, yb_s = odd_mixer_latent(proj_s, op, cache_c_k[:, i], cache_c_v[:, i], state_s5[:, i])
            new_ck.append(ck)
            new_cv.append(cv)
            new_s5.append(s5)
        wr = _router_weights(moe_rg[layer], moe_re[layer])
        br = _pad_cols(jnp.concatenate([moe_rg_b[layer], moe_re_b[layer]])[None, :], ROUTER_LANES)
        moe_w = (wo_a, wo_b, wr, br, moe_wg[layer].astype(BF16), moe_wu[layer].astype(BF16),
                 moe_wd[layer].astype(BF16))
        last = layer == DEPTH - 1
        xp = moe_pallas(xp, ya_p, yb_p, mod_p, norm_w[layer, 1], *moe_w, final_norm_w, BATCH * SEQ, last)
        xs = moe_pallas(xs, ya_s, yb_s, mod_s, norm_w[layer, 1], *moe_w, final_norm_w, DEC_SEQ, last)
    y_prompt = xp.reshape(x_prompt.shape)
    y_sample = xs.reshape(x_sample.shape)
    return (y_prompt, y_sample, jnp.stack(new_dn, axis=1), jnp.stack(new_bk, axis=1), jnp.stack(new_bv, axis=1),
            jnp.stack(new_ck, axis=1), jnp.stack(new_cv, axis=1), jnp.stack(new_s5, axis=1))
```

```python
import functools

import numpy as np
import jax
import jax.numpy as jnp
from jax import lax
from jax.experimental import pallas as pl
from jax.experimental.pallas import tpu as pltpu

D_MODEL = 1024
BATCH = 32
SEQ = 256
DEPTH = 2
DEC_BATCH = 8
DEC_SEQ = 2048
PAST_LEN = 256

GRID_W = 64
HEAD_DIM = 64
BLK = 128
ROPE_THETA = 10000.0
EPS = 1e-6
N_EVEN = (DEPTH + 1) // 2
N_ODD = DEPTH // 2
HA = D_MODEL // 128
DK = HEAD_DIM
CONV_W = 5
DN_CHUNK = 64
HB = D_MODEL // 128
HKV_B = HB // 4
WINDOW = 128
HC = (3 * D_MODEL // 4) // HEAD_DIM
HKV_C = HC // 3
D_S5 = D_MODEL // 4
S5_CH = 16
G_D = D_S5 // S5_CH
S5_P = 64
N_GROUPS = 4
EXP_PER_GROUP = 8
N_EXP = N_GROUPS * EXP_PER_GROUP
EXPERT_FF = D_MODEL // 8
TOP_K = 2

EVEN_SPLITS = (3 * HA * DK, HA * DK, 2 * HA, 2 * HA, HB * HEAD_DIM, HKV_B * HEAD_DIM, HKV_B * HEAD_DIM)
D_IN_EVEN = sum(EVEN_SPLITS)
D_MIX_EVEN = HA * DK + HB * HEAD_DIM
ODD_SPLITS = (HC * HEAD_DIM, HKV_C * HEAD_DIM, HKV_C * HEAD_DIM, D_S5)
D_IN_ODD = sum(ODD_SPLITS)
D_MIX_ODD = HC * HEAD_DIM + D_S5

F32 = jnp.float32
BF16 = jnp.bfloat16
MIX_DTYPE = BF16


def split_cols(x, sizes):
    idx = np.cumsum(sizes)[:-1].tolist()
    return jnp.split(x, idx, axis=-1)


def axial_rope_tables(rows):
    row = jnp.repeat(jnp.arange(rows), GRID_W).astype(F32)
    col = jnp.tile(jnp.arange(GRID_W), rows).astype(F32)
    quarter = HEAD_DIM // 4
    freqs = ROPE_THETA ** (-jnp.arange(quarter, dtype=F32) / quarter)
    ang_r = row[:, None] * freqs
    ang_c = col[:, None] * freqs
    ang = jnp.concatenate([ang_r, ang_r, ang_c, ang_c], axis=-1)
    return jnp.cos(ang), jnp.sin(ang)


SUBLANES = 8
VMEM_LIMIT = 56 * 1024 * 1024
S5_ROWS = 1024
S5_STATE = G_D * S5_P


def _s5_scan_kernel(u_ref, bmat_ref, cmat_ref, a_ref, s0_ref, y_ref, sfin_ref, xs_ref, st_ref, *, nblk, bsz):
    dr = pl.program_id(0)
    blk = pl.program_id(1)

    @pl.when(blk == 0)
    def _():
        st_ref[...] = s0_ref[0]

    xs_ref[...] = jnp.dot(u_ref[...].astype(BF16), bmat_ref[0], preferred_element_type=F32)
    a_re = jnp.broadcast_to(a_ref[0, 0:1, :], (SUBLANES, S5_STATE))
    a_im = jnp.broadcast_to(a_ref[0, 1:2, :], (SUBLANES, S5_STATE))
    steps = S5_ROWS // bsz
    for sg in range(bsz // SUBLANES):
        rows = pl.ds(sg * SUBLANES, SUBLANES)

        def body(i, carry, sg=sg):
            x_re, x_im = carry
            l = i + dr * (steps - 1 - 2 * i)
            r = pl.ds(pl.multiple_of(l * bsz + sg * SUBLANES, SUBLANES), SUBLANES)
            n_re = a_re * x_re - a_im * x_im + xs_ref[r, 0:S5_STATE]
            n_im = a_re * x_im + a_im * x_re + xs_ref[r, S5_STATE:2 * S5_STATE]
            xs_ref[r, 0:S5_STATE] = n_re
            xs_ref[r, S5_STATE:2 * S5_STATE] = n_im
            return n_re, n_im

        x_re, x_im = lax.fori_loop(0, steps, body, (st_ref[rows, 0:S5_STATE], st_ref[rows, S5_STATE:2 * S5_STATE]),
                                   unroll=4)
        st_ref[rows, 0:S5_STATE] = x_re
        st_ref[rows, S5_STATE:2 * S5_STATE] = x_im
    y_ref[0] = jnp.dot(xs_ref[...].astype(BF16), cmat_ref[0], preferred_element_type=F32)

    @pl.when(blk == nblk - 1)
    def _():
        sfin_ref[0] = st_ref[...]


def _s5_out_kernel(yf_ref, yb_ref, u_ref, d_ref, w_ref, b_ref, o_ref):
    y = yf_ref[0] + yb_ref[0] + u_ref[...] * d_ref[...]
    y = jax.nn.gelu(y)
    gate = jnp.dot(y.astype(BF16), w_ref[...], preferred_element_type=F32) + b_ref[...]
    o_ref[...] = (y * jax.nn.sigmoid(gate)).astype(o_ref.dtype)


def s5_operators(op):
    lam_re, lam_im = op['lam_re'].astype(F32), op['lam_im'].astype(F32)
    dt = jnp.exp(op['log_dt'].astype(F32))[..., None]
    mag = jnp.exp(lam_re * dt)
    a_re, a_im = mag * jnp.cos(lam_im * dt), mag * jnp.sin(lam_im * dt)
    den = lam_re * lam_re + lam_im * lam_im
    k_re = ((a_re - 1) * lam_re + a_im * lam_im) / den
    k_im = (a_im * lam_re - (a_re - 1) * lam_im) / den
    b_re, b_im = op['b_re'].astype(F32), op['b_im'].astype(F32)
    bb_re = k_re[..., None] * b_re - k_im[..., None] * b_im
    bb_im = k_re[..., None] * b_im + k_im[..., None] * b_re
    eye = jnp.eye(G_D, dtype=F32)

    def b_blocks(t):
        return jnp.einsum('dgpc,gh->dgchp', t, eye).reshape(2, D_S5, S5_STATE)

    def c_blocks(t):
        return jnp.einsum('dgcp,gh->dgphc', t, eye).reshape(2, S5_STATE, D_S5)

    b_blk = jnp.concatenate([b_blocks(bb_re), b_blocks(bb_im)], axis=-1).astype(BF16)
    c_blk = jnp.concatenate([c_blocks(op['c_re'].astype(F32)), -c_blocks(op['c_im'].astype(F32))],
                            axis=1).astype(BF16)
    a_vec = jnp.stack([a_re.reshape(2, S5_STATE), a_im.reshape(2, S5_STATE)], axis=1)
    return b_blk, c_blk, a_vec


def s5_pallas(u, op, ops, s0):
    b, l, _ = u.shape
    b_blk, c_blk, a_vec = ops
    n = b * l
    nblk = n // S5_ROWS
    u_t = jnp.swapaxes(u, 0, 1).reshape(n, D_S5)

    def rows_map(dr, blk):
        return (blk + dr * (nblk - 1 - 2 * blk), 0)

    y2, s_fin = pl.pallas_call(
        functools.partial(_s5_scan_kernel, nblk=nblk, bsz=b),
        grid=(2, nblk),
        in_specs=[
            pl.BlockSpec((S5_ROWS, D_S5), rows_map),
            pl.BlockSpec((1, D_S5, 2 * S5_STATE), lambda dr, blk: (dr, 0, 0)),
            pl.BlockSpec((1, 2 * S5_STATE, D_S5), lambda dr, blk: (dr, 0, 0)),
            pl.BlockSpec((1, 2, S5_STATE), lambda dr, blk: (dr, 0, 0)),
            pl.BlockSpec((1, b, 2 * S5_STATE), lambda dr, blk: (dr, 0, 0)),
        ],
        out_specs=[
            pl.BlockSpec((1, S5_ROWS, D_S5), lambda dr, blk: (dr,) + rows_map(dr, blk)),
            pl.BlockSpec((1, b, 2 * S5_STATE), lambda dr, blk: (dr, 0, 0)),
        ],
        out_shape=[jax.ShapeDtypeStruct((2, n, D_S5), F32), jax.ShapeDtypeStruct((2, b, 2 * S5_STATE), F32)],
        scratch_shapes=[pltpu.VMEM((S5_ROWS, 2 * S5_STATE), F32), pltpu.VMEM((b, 2 * S5_STATE), F32)],
        compiler_params=pltpu.CompilerParams(dimension_semantics=("arbitrary", "arbitrary"),
                                             vmem_limit_bytes=VMEM_LIMIT),
        name="s5_scan",
    )(u_t, b_blk, c_blk, a_vec, s0)
    tm = S5_ROWS
    y = pl.pallas_call(
        _s5_out_kernel,
        grid=(n // tm,),
        in_specs=[
            pl.BlockSpec((1, tm, D_S5), lambda i: (0, i, 0)),
            pl.BlockSpec((1, tm, D_S5), lambda i: (1, i, 0)),
            pl.BlockSpec((tm, D_S5), lambda i: (i, 0)),
            pl.BlockSpec((1, D_S5), lambda i: (0, 0)),
            pl.BlockSpec((D_S5, D_S5), lambda i: (0, 0)),
            pl.BlockSpec((1, D_S5), lambda i: (0, 0)),
        ],
        out_specs=pl.BlockSpec((tm, D_S5), lambda i: (i, 0)),
        out_shape=jax.ShapeDtypeStruct((n, D_S5), MIX_DTYPE),
        name="s5_out",
    )(y2, y2, u_t, op['d'].astype(F32).reshape(1, D_S5), op['glu_w'].astype(BF16),
      op['glu_b'].astype(F32).reshape(1, D_S5))
    return jnp.swapaxes(y.reshape(l, b, D_S5), 0, 1), s_fin


LANES = 128
DN_PAIRS = HA * DK // LANES
NEG_BIG = -1e30
DN_UNROLL = 4
DN_BLOCK = DN_CHUNK // 4


def _nt_dot(a, b):
    return lax.dot_general(a, b, (((1,), (1,)), ((), ())), preferred_element_type=F32)


def _bf16_dot(a, b):
    return jnp.dot(a.astype(BF16), b.astype(BF16), preferred_element_type=F32)


def _select_dot(x, sel):
    x1 = x.astype(BF16)
    r1 = x - x1.astype(F32)
    x2 = r1.astype(BF16)
    x3 = (r1 - x2.astype(F32)).astype(BF16)
    dot = functools.partial(jnp.dot, preferred_element_type=F32)
    return dot(x1, sel) + dot(x2, sel) + dot(x3, sel)


def _head_sums(x):
    lane = lax.broadcasted_iota(jnp.int32, x.shape, 1)
    lo = lane < HEAD_DIM
    s0 = jnp.sum(jnp.where(lo, x, 0.0), axis=-1, keepdims=True)
    s1 = jnp.sum(jnp.where(lo, 0.0, x), axis=-1, keepdims=True)
    return jnp.where(lo, s0, s1)


def _chunk_cumsum(x, reverse):
    n = x.shape[0]
    pos = lax.broadcasted_iota(jnp.int32, x.shape, 0) % DN_CHUNK
    s = 1
    while s < DN_CHUNK:
        if reverse:
            x = x + jnp.where(pos < DN_CHUNK - s, pltpu.roll(x, n - s, 0), 0.0)
        else:
            x = x + jnp.where(pos >= s, pltpu.roll(x, s, 0), 0.0)
        s *= 2
    return x


def _short_conv_silu(x, w):
    n = x.shape[0]
    row = lax.broadcasted_iota(jnp.int32, x.shape, 0)
    pad = CONV_W // 2
    y = x * w[pad:pad + 1, :]
    for t in range(CONV_W):
        s = t - pad
        if s == 0:
            continue
        shifted = pltpu.roll(x, (-s) % n, 0)
        valid = (row + s >= 0) & (row + s < n)
        y = y + jnp.where(valid, shifted, 0.0) * w[t:t + 1, :]
    return jax.nn.silu(y)


def _unit_tri_solve(ms, rhss, ri, ci):
    assert DN_CHUNK == 4 * DN_BLOCK
    n = rhss[0].shape[-1]
    same_block = (ri // DN_BLOCK) == (ci // DN_BLOCK)
    dot = functools.partial(jnp.dot, preferred_element_type=F32)

    def apply(p, r):
        hi = r.astype(BF16)
        lo = (r - hi.astype(F32)).astype(BF16)
        x = dot(p, jnp.concatenate([hi, lo], axis=-1))
        return x[:, 0:n] + x[:, n:2 * n]

    diag = [jnp.where(same_block, m, 0.0) for m in ms]
    rest = [jnp.where(same_block, 0.0, m).astype(BF16) for m in ms]
    pows = [d.astype(BF16) for d in diag]
    tinv = [jnp.where(ri == ci, 1.0, 0.0) - d for d in diag]
    power = 2
    while power < DN_BLOCK:
        pows = [dot(p, p).astype(BF16) for p in pows]
        tinv = [t + dot(t.astype(BF16), p) for t, p in zip(tinv, pows)]
        power *= 2
    tinv = [t.astype(BF16) for t in tinv]
    p1 = [dot(t, e).astype(BF16) for t, e in zip(tinv, rest)]
    sols = [apply(t, r) for t, r in zip(tinv, rhss)]
    p2 = [dot(p, p).astype(BF16) for p in p1]
    sols = [s - apply(p, s) for p, s in zip(p1, sols)]
    return [s + apply(p, s) for p, s in zip(p2, sols)]


def _deltanet_kernel(q_ref, k_ref, v_ref, z_ref, a_ref, bb_ref, cwq_ref, cwk_ref, cwv_ref, alog_ref, dtb_ref, nw_ref,
                     s0_ref, y_ref, sfin_ref,
                     qs_ref, ks_ref, vs_ref, o_ref, gc_ref, beta_ref, u_ref, w_ref, attn_ref, qp_ref, kpt_ref,
                     etot_ref, *, seq):
    hp = pl.program_id(1)
    nchunk = seq // DN_CHUNK
    c64 = DN_CHUNK
    head_lanes = [slice(j * HEAD_DIM, (j + 1) * HEAD_DIM) for j in range(2)]

    q = _short_conv_silu(q_ref[0], cwq_ref[...])
    q = q * lax.rsqrt(_head_sums(q * q) + EPS) * (DK ** -0.5)
    k = _short_conv_silu(k_ref[0], cwk_ref[...])
    k = k * lax.rsqrt(_head_sums(k * k) + EPS)
    v = _short_conv_silu(v_ref[0], cwv_ref[...])
    qs_ref[...] = q
    ks_ref[...] = k
    vs_ref[...] = v
    o_ref[...] = jnp.zeros_like(o_ref)

    g_col = -jnp.exp(alog_ref[...]) * jax.nn.softplus(a_ref[0] + dtb_ref[...])
    b_col = jax.nn.sigmoid(bb_ref[0])
    sel_row = lax.broadcasted_iota(jnp.int32, (2 * HA, 2 * LANES), 0)
    sel_lane = lax.broadcasted_iota(jnp.int32, (2 * HA, 2 * LANES), 1)
    sel = (sel_row == (sel_lane // LANES) * HA + 2 * hp + (sel_lane // HEAD_DIM) % 2).astype(BF16)
    g_b = _select_dot(g_col, sel)
    gc_ref[:, 0:LANES] = _chunk_cumsum(g_b[:, 0:LANES], reverse=False)
    gc_ref[:, LANES:2 * LANES] = _chunk_cumsum(g_b[:, LANES:2 * LANES], reverse=True)
    beta_ref[...] = _select_dot(b_col, sel)

    ri = lax.broadcasted_iota(jnp.int32, (c64, c64), 0)
    ci = lax.broadcasted_iota(jnp.int32, (c64, c64), 1)

    def intra_chunk(step, carry):
        chunk_ids = [step * DN_UNROLL + t for t in range(DN_UNROLL)]
        rows_c = [pl.ds(pl.multiple_of(c * c64, c64), c64) for c in chunk_ids]
        loads = [(qs_ref[r, :], ks_ref[r, :], vs_ref[r, :], gc_ref[r, :], beta_ref[r, :]) for r in rows_c]
        heads = [(t, j, ld[0][:, head_lanes[j]], ld[1][:, head_lanes[j]], ld[2][:, head_lanes[j]])
                 for t, ld in enumerate(loads) for j in range(2)]
        gate_lanes = [slice((c % 2) * LANES + (c // 2) * HEAD_DIM, (c % 2) * LANES + (c // 2 + 1) * HEAD_DIM)
                      for c in range(4)]
        gates = {(t, c): (ld[3][:, gate_lanes[c]], ld[4][:, gate_lanes[c]])
                 for t, ld in enumerate(loads) for c in range(4)}
        kbs = [h[3].astype(BF16) for h in heads]
        kks = [_nt_dot(kb, kb) for kb in kbs]
        qks = [_nt_dot(h[2].astype(BF16), kb) for h, kb in zip(heads, kbs)]
        chains, ms, rhss = [], [], []
        for (t, j, qj, kj, vj), kk, qk in zip(heads, kks, qks):
            for d in range(2):
                gcb, bet = gates[(t, 2 * j + d)]
                diff = gcb - gcb.T
                earlier = (ri >= ci) if d == 0 else (ri <= ci)
                strict = (ri > ci) if d == 0 else (ri < ci)
                dec = jnp.exp(jnp.where(earlier, diff, NEG_BIG))
                eg = jnp.exp(gcb)
                gtot = gcb[c64 - 1:c64, :] if d == 0 else gcb[0:1, :]
                ms.append(jnp.where(strict, kk * bet * dec, 0.0))
                rhss.append(jnp.concatenate([vj * bet, kj * bet * eg], axis=-1))
                chains.append((t, 2 * j + d, qk * dec, qj * eg, (kj * jnp.exp(gtot - gcb)).T,
                               jnp.broadcast_to(jnp.exp(gtot), (SUBLANES, HEAD_DIM))))
        uws = _unit_tri_solve(ms, rhss, ri, ci)
        for (t, c, attn, qp, kpt, etot), uw in zip(chains, uws):
            u_ref[c, rows_c[t], :] = uw[:, 0:HEAD_DIM]
            w_ref[c, rows_c[t], :] = uw[:, HEAD_DIM:2 * HEAD_DIM].astype(BF16)
            attn_ref[c, rows_c[t], :] = attn.astype(BF16)
            qp_ref[c, rows_c[t], :] = qp.astype(BF16)
            kpt_ref[c, rows_c[t], :] = kpt.astype(BF16)
            etot_ref[c, pl.ds(pl.multiple_of(chunk_ids[t] * SUBLANES, SUBLANES), SUBLANES), :] = etot
        return carry

    lax.fori_loop(0, nchunk // DN_UNROLL, intra_chunk, 0)

    def inter_chunk(i, states):
        rows_d = [pl.ds(pl.multiple_of(cc * c64, c64), c64) for cc in (i, nchunk - 1 - i)]
        erow_d = [pl.ds(pl.multiple_of(cc * SUBLANES, SUBLANES), SUBLANES) for cc in (i, nchunk - 1 - i)]
        loaded = [(w_ref[c, rows_d[c % 2], :], qp_ref[c, rows_d[c % 2], :], u_ref[c, rows_d[c % 2], :],
                   attn_ref[c, rows_d[c % 2], :], kpt_ref[c, rows_d[c % 2], :], etot_ref[c, erow_d[c % 2], :],
                   o_ref[c // 2, rows_d[c % 2], :]) for c in range(4)]
        ws_qs = [jnp.dot(jnp.concatenate([ld[0], ld[1]], axis=0), s.astype(BF16), preferred_element_type=F32)
                 for ld, s in zip(loaded, states)]
        v_new = [(ld[2] - r[0:c64]).astype(BF16) for ld, r in zip(loaded, ws_qs)]
        av = [jnp.dot(ld[3], vn, preferred_element_type=F32) for ld, vn in zip(loaded, v_new)]
        kv = [jnp.dot(ld[4], vn, preferred_element_type=F32) for ld, vn in zip(loaded, v_new)]
        for c in range(4):
            o_ref[c // 2, rows_d[c % 2], :] = loaded[c][6] + ws_qs[c][c64:2 * c64] + av[c]
        return tuple(s * ld[5][0:1, :] + x for s, ld, x in zip(states, loaded, kv))

    init = tuple(s0_ref[0, c % 2, c // 2] for c in range(4))
    fin = lax.fori_loop(0, nchunk, inter_chunk, init)
    for c in range(4):
        sfin_ref[0, c % 2, c // 2] = fin[c]

    o = jnp.concatenate([o_ref[0], o_ref[1]], axis=-1)
    o = o * lax.rsqrt(_head_sums(o * o) * (1.0 / DK) + EPS) * nw_ref[...]
    y_ref[0] = (o * jax.nn.silu(z_ref[0])).astype(y_ref.dtype)


def deltanet_pallas(qkv, z, a, bb, ep, s0):
    b, l, _ = qkv.shape
    npair = DN_PAIRS

    def col_spec(off):
        return pl.BlockSpec((1, l, LANES), lambda i, hp: (i, 0, off + hp))

    def cw_spec(off):
        return pl.BlockSpec((CONV_W, LANES), lambda i, hp: (0, off + hp))

    def full(shape):
        return pl.BlockSpec(shape, lambda i, hp: (0,) * len(shape))

    state_spec = pl.BlockSpec((1, 2, 2, DK, DK), lambda i, hp: (i, 0, hp, 0, 0))
    row_f32 = pltpu.VMEM((l, LANES), F32)
    gate_f32 = pltpu.VMEM((l, 2 * LANES), F32)
    chain_f32 = pltpu.VMEM((4, l, HEAD_DIM), F32)
    chain_bf16 = pltpu.VMEM((4, l, HEAD_DIM), BF16)
    y, s_fin = pl.pallas_call(
        functools.partial(_deltanet_kernel, seq=l),
        grid=(b, npair),
        in_specs=[col_spec(0), col_spec(npair), col_spec(2 * npair), col_spec(0),
                  pl.BlockSpec((1, l, 2 * HA), lambda i, hp: (i, 0, 0)),
                  pl.BlockSpec((1, l, 2 * HA), lambda i, hp: (i, 0, 0)),
                  cw_spec(0), cw_spec(npair), cw_spec(2 * npair),
                  full((1, 2 * HA)), full((1, 2 * HA)), full((1, LANES)), state_spec],
        out_specs=[col_spec(0), state_spec],
        out_shape=[jax.ShapeDtypeStruct((b, l, HA * DK), MIX_DTYPE), jax.ShapeDtypeStruct((b, 2, HA, DK, DK), F32)],
        scratch_shapes=[row_f32, row_f32, row_f32, pltpu.VMEM((2, l, HEAD_DIM), F32), gate_f32, gate_f32,
                        chain_f32, chain_bf16, chain_bf16, chain_bf16, chain_bf16,
                        pltpu.VMEM((4, l // DN_CHUNK * SUBLANES, HEAD_DIM), F32)],
        compiler_params=pltpu.CompilerParams(dimension_semantics=("arbitrary", "arbitrary"),
                                             vmem_limit_bytes=VMEM_LIMIT),
        name="deltanet",
    )(qkv, qkv, qkv, z, a, bb, ep['conv_w'], ep['conv_w'], ep['conv_w'],
      ep['a_log'].astype(F32).reshape(1, 2 * HA), ep['dt_bias'].astype(F32).reshape(1, 2 * HA),
      jnp.tile(ep['norm_w'].astype(F32), 2).reshape(1, LANES), s0)
    return y, s_fin


TM = 512
ATT_TQ = BLK
ROT = HEAD_DIM // 4


def rope_tables():
    cos, sin = axial_rope_tables(DEC_SEQ // GRID_W)
    cos2 = jnp.tile(cos, (1, LANES // HEAD_DIM))
    sin2 = jnp.tile(sin, (1, LANES // HEAD_DIM))
    first = (jnp.arange(LANES) % (2 * ROT)) < ROT
    return cos2, jnp.where(first, -sin2, 0.0), jnp.where(first, 0.0, sin2)


def _head_prep(x, w_ref, tabs, scale):
    outs, normed = [], []
    for c in range(x.shape[1] // LANES):
        xc = x[:, c * LANES:(c + 1) * LANES]
        if w_ref is not None:
            xc = xc * lax.rsqrt(_head_sums(xc * xc) * (1.0 / HEAD_DIM) + EPS) * w_ref[...]
            normed.append(xc)
        if tabs is not None:
            cos_ref, sup_ref, sdn_ref = tabs
            xc = (xc * cos_ref[...] + pltpu.roll(xc, LANES - ROT, 1) * sup_ref[...]
                  + pltpu.roll(xc, ROT, 1) * sdn_ref[...])
        outs.append((xc * scale).astype(BF16))
    cat = lambda parts: jnp.concatenate(parts, axis=-1) if len(parts) > 1 else parts[0]
    return cat(outs), (cat(normed) if w_ref is not None else None)


def _attn_kernel(q_ref, k_ref, v_ref, *refs, grp, hk, windowed, has_ctx, has_sink):
    refs = list(refs)
    kc_ref, vc_ref = (refs.pop(0), refs.pop(0)) if has_ctx else (None, None)
    sink_ref = refs.pop(0) if has_sink else None
    o_ref = refs[0]
    tq = q_ref.shape[1]
    seq_k = k_ref.shape[1]
    if windowed:
        qi = pl.program_id(1)
        start = jnp.clip((qi - 1) * tq, 0, seq_k - 3 * tq)
        krows = pl.ds(pl.multiple_of(start, tq), 3 * tq)
        qpos = qi * tq + lax.broadcasted_iota(jnp.int32, (grp * tq, 3 * tq), 0) % tq
        kpos = start + lax.broadcasted_iota(jnp.int32, (grp * tq, 3 * tq), 1)
        keep = jnp.abs(qpos - kpos) <= WINDOW
    else:
        krows = pl.ds(0, seq_k)
    q = q_ref[0]
    groups = range(hk)
    lanes = [slice(g * HEAD_DIM, (g + 1) * HEAD_DIM) for g in groups]
    qgs = [jnp.concatenate([q[:, (g * grp + j) * HEAD_DIM:(g * grp + j + 1) * HEAD_DIM] for j in range(grp)], axis=0)
           for g in groups]
    ss = [_nt_dot(qg, k_ref[0, krows, gl]) for qg, gl in zip(qgs, lanes)]
    if windowed:
        ss = [jnp.where(keep, s, NEG_BIG) for s in ss]
    ms = [jnp.max(s, axis=-1, keepdims=True) for s in ss]
    if has_ctx:
        kc = kc_ref[0].astype(BF16)
        scs = [_nt_dot(qg, kc[:, gl]) for qg, gl in zip(qgs, lanes)]
        ms = [jnp.maximum(m, jnp.max(sc, axis=-1, keepdims=True)) for m, sc in zip(ms, scs)]
    if has_sink:
        head_rows = [slice(j * tq, (j + 1) * tq) for j in range(grp)]
        ms = [jnp.concatenate([jnp.maximum(m[head_rows[j]], sink_ref[g * grp + j]) for j in range(grp)], axis=0)
              for g, m in zip(groups, ms)]
    ps = [jnp.exp(s - m) for s, m in zip(ss, ms)]
    dens = [jnp.sum(p, axis=-1, keepdims=True) for p in ps]
    os_ = [jnp.dot(p.astype(BF16), v_ref[0, krows, gl], preferred_element_type=F32) for p, gl in zip(ps, lanes)]
    if has_ctx:
        vc = vc_ref[0].astype(BF16)
        pcs = [jnp.exp(sc - m) for sc, m in zip(scs, ms)]
        dens = [den + jnp.sum(pc, axis=-1, keepdims=True) for den, pc in zip(dens, pcs)]
        os_ = [o + jnp.dot(pc.astype(BF16), vc[:, gl], preferred_element_type=F32)
               for o, pc, gl in zip(os_, pcs, lanes)]
    if has_sink:
        dens = [den + jnp.concatenate([jnp.exp(sink_ref[g * grp + j] - m[head_rows[j]]) for j in range(grp)], axis=0)
                for g, den, m in zip(groups, dens, ms)]
    outs = [None] * (grp * hk)
    for g in groups:
        o = os_[g] / dens[g]
        for j in range(grp):
            outs[g * grp + j] = o[j * tq:(j + 1) * tq]
    o_ref[0] = jnp.concatenate(outs, axis=-1).astype(o_ref.dtype)


def attention_pallas(q, k, v, ctx=None, sink=None, windowed=False):
    b, lq, hd = q.shape
    lk, kd = k.shape[1], k.shape[2]
    hk = kd // HEAD_DIM
    grp = hd // kd
    tq = ATT_TQ if windowed or lq > 2 * ATT_TQ else lq
    ins = [q, k, v]
    in_specs = [pl.BlockSpec((1, tq, hd), lambda i, t: (i, t, 0)),
                pl.BlockSpec((1, lk, kd), lambda i, t: (i, 0, 0)),
                pl.BlockSpec((1, lk, kd), lambda i, t: (i, 0, 0))]
    if ctx is not None:
        ins += list(ctx)
        in_specs += [pl.BlockSpec((1, ctx[0].shape[1], kd), lambda i, t: (i, 0, 0))] * 2
    if sink is not None:
        ins.append(sink.astype(F32))
        in_specs.append(pl.BlockSpec(memory_space=pltpu.SMEM))
    return pl.pallas_call(
        functools.partial(_attn_kernel, grp=grp, hk=hk, windowed=windowed, has_ctx=ctx is not None,
                          has_sink=sink is not None),
        grid=(b, lq // tq),
        in_specs=in_specs,
        out_specs=pl.BlockSpec((1, tq, hd), lambda i, t: (i, t, 0)),
        out_shape=jax.ShapeDtypeStruct((b, lq, hd), MIX_DTYPE),
        compiler_params=pltpu.CompilerParams(dimension_semantics=("arbitrary", "arbitrary"),
                                             vmem_limit_bytes=VMEM_LIMIT),
        name="attention",
    )(*ins)


def even_mixer_context(proj, ep):
    qkv, z, ab, qh, kh, vh, bk, bv = proj
    qkv, z, ab = [t.reshape(BATCH, SEQ, -1) for t in (qkv, z, ab)]
    a, bb = ab[..., 0:2 * HA], ab[..., 2 * HA:4 * HA]
    y_a, dn_fin = deltanet_pallas(qkv, z, a, bb, ep, jnp.zeros((BATCH, 2, HA, DK, DK), F32))
    y_b = attention_pallas(*[t.reshape(BATCH, SEQ, -1) for t in (qh, kh, vh)], sink=ep['sink'])
    return (y_a.reshape(-1, HA * DK), y_b.reshape(-1, HB * HEAD_DIM), dn_fin,
            bk.reshape(BATCH, SEQ, HKV_B, HEAD_DIM), bv.reshape(BATCH, SEQ, HKV_B, HEAD_DIM))


def even_mixer_latent(proj, ep, dn_state, ctx_k, ctx_v):
    qkv, z, ab, qh, kh, vh = proj
    qkv, z, ab = [t.reshape(DEC_BATCH, DEC_SEQ, -1) for t in (qkv, z, ab)]
    a, bb = ab[..., 0:2 * HA], ab[..., 2 * HA:4 * HA]
    y_a, _ = deltanet_pallas(qkv, z, a, bb, ep, dn_state.astype(F32))
    ctx = (ctx_k.reshape(DEC_BATCH, PAST_LEN, -1), ctx_v.reshape(DEC_BATCH, PAST_LEN, -1))
    y_b = attention_pallas(*[t.reshape(DEC_BATCH, DEC_SEQ, -1) for t in (qh, kh, vh)], ctx=ctx, sink=ep['sink'],
                           windowed=True)
    return y_a.reshape(-1, HA * DK), y_b.reshape(-1, HB * HEAD_DIM)


def odd_mixer_context(proj, op):
    u, qh, kh, vh, k_normed, cv = proj
    y_c = attention_pallas(*[t.reshape(BATCH, SEQ, -1) for t in (qh, kh, vh)])
    y_d, s_fin = s5_pallas(u.reshape(BATCH, SEQ, -1), op, s5_operators(op), jnp.zeros((2, BATCH, 2 * S5_STATE), F32))
    s_fin = jnp.transpose(s_fin.reshape(2, BATCH, 2, G_D, S5_P), (1, 0, 3, 4, 2))
    return (y_c.reshape(-1, HC * HEAD_DIM), y_d.reshape(-1, D_S5), k_normed.reshape(BATCH, SEQ, HKV_C, HEAD_DIM),
            cv.reshape(BATCH, SEQ, HKV_C, HEAD_DIM), s_fin)


def odd_mixer_latent(proj, op, ctx_k, ctx_v, s5_state):
    u, qh, kh, vh = proj
    ctx = (ctx_k.reshape(DEC_BATCH, PAST_LEN, -1), ctx_v.reshape(DEC_BATCH, PAST_LEN, -1))
    y_c = attention_pallas(*[t.reshape(DEC_BATCH, DEC_SEQ, -1) for t in (qh, kh, vh)], ctx=ctx)
    s0 = jnp.transpose(s5_state.astype(F32), (1, 0, 4, 2, 3)).reshape(2, DEC_BATCH, 2 * S5_STATE)
    y_d, _ = s5_pallas(u.reshape(DEC_BATCH, DEC_SEQ, -1), op, s5_operators(op), s0)
    return y_c.reshape(-1, HC * HEAD_DIM), y_d.reshape(-1, D_S5)


ADA_ROWS = 16
ADA_TN = 1024
MOD_ROWS = 8
ROUTER_LANES = LANES
MOE_FCHUNK = EXP_PER_GROUP * EXPERT_FF


def _ada_kernel(c_ref, w_ref, b_ref, o_ref):
    o_ref[0] = _bf16_dot(jax.nn.silu(c_ref[...]), w_ref[0]) + b_ref[0]


def ada_pallas(cvec, ada_w, ada_b):
    n = 6 * D_MODEL
    return pl.pallas_call(
        _ada_kernel,
        grid=(DEPTH, n // ADA_TN),
        in_specs=[pl.BlockSpec((ADA_ROWS, D_MODEL), lambda l, j: (0, 0)),
                  pl.BlockSpec((1, D_MODEL, ADA_TN), lambda l, j: (l, 0, j)),
                  pl.BlockSpec((1, 1, ADA_TN), lambda l, j: (l, 0, j))],
        out_specs=pl.BlockSpec((1, ADA_ROWS, ADA_TN), lambda l, j: (l, 0, j)),
        out_shape=jax.ShapeDtypeStruct((DEPTH, ADA_ROWS, n), F32),
        name="ada_modulation",
    )(cvec, ada_w, ada_b.reshape(DEPTH, 1, n))


def _modulated(x, nw, shift, scale):
    return x * lax.rsqrt(jnp.mean(x * x, axis=-1, keepdims=True) + EPS) * nw * (1 + scale) + shift


def _inproj_kernel(x_ref, mod_ref, nw_ref, *refs, n_plain, norm, rope, keep_k, keep_v):
    refs = list(refs)
    m = mod_ref[0]
    h = _modulated(x_ref[...], nw_ref[...], m[0:1], m[1:2]).astype(BF16)
    w_refs = [refs.pop(0) for _ in range(n_plain + 3)]
    qw_ref, kw_ref = (refs.pop(0), refs.pop(0)) if norm else (None, None)
    tabs = [refs.pop(0) for _ in range(3)] if rope else None
    plain_refs = [refs.pop(0) for _ in range(n_plain)]
    qo_ref, ko_ref, vo_ref = refs.pop(0), refs.pop(0), refs.pop(0)
    for w_ref, o_ref in zip(w_refs, plain_refs):
        o_ref[...] = jnp.dot(h, w_ref[...], preferred_element_type=F32)
    q, k, v = [jnp.dot(h, w_ref[...], preferred_element_type=F32) for w_ref in w_refs[n_plain:]]
    qo_ref[...] = _head_prep(q, qw_ref, tabs, HEAD_DIM ** -0.5)[0]
    ko, k_normed = _head_prep(k, kw_ref, tabs, 1.0)
    ko_ref[...] = ko
    vo_ref[...] = v.astype(BF16)
    if keep_k:
        refs.pop(0)[...] = k_normed if norm else k
    if keep_v:
        refs.pop(0)[...] = v


def inproj_pallas(x, mod, nw, plain_ws, qkv_ws, rows_per_mod, norm_ws=None, rope_tabs=None, keep_kv=False):
    t = x.shape[0]
    tiles_per_mod = rows_per_mod // TM
    row = lambda i: (i, 0)
    const = lambda i: (0, 0)
    weights = list(plain_ws) + list(qkv_ws)
    ins = [x, mod, nw.reshape(1, D_MODEL)] + weights
    in_specs = [pl.BlockSpec((TM, D_MODEL), row),
                pl.BlockSpec((1, MOD_ROWS, D_MODEL), lambda i: (i // tiles_per_mod, 0, 0)),
                pl.BlockSpec((1, D_MODEL), const)] + [pl.BlockSpec(w.shape, const) for w in weights]
    if norm_ws is not None:
        ins += [jnp.tile(w.astype(F32), LANES // HEAD_DIM).reshape(1, LANES) for w in norm_ws]
        in_specs += [pl.BlockSpec((1, LANES), const)] * 2
    if rope_tabs is not None:
        tiles = DEC_SEQ // TM
        ins += list(rope_tabs)
        in_specs += [pl.BlockSpec((TM, LANES), lambda i: (i % tiles, 0))] * 3
    outs = [(w.shape[1], F32) for w in plain_ws] + [(w.shape[1], BF16) for w in qkv_ws]
    if keep_kv:
        outs += [(qkv_ws[1].shape[1], F32), (qkv_ws[2].shape[1], F32)]
    return pl.pallas_call(
        functools.partial(_inproj_kernel, n_plain=len(plain_ws), norm=norm_ws is not None,
                          rope=rope_tabs is not None, keep_k=keep_kv, keep_v=keep_kv),
        grid=(t // TM,),
        in_specs=in_specs,
        out_specs=[pl.BlockSpec((TM, n), row) for n, _ in outs],
        out_shape=[jax.ShapeDtypeStruct((t, n), dt) for n, dt in outs],
        compiler_params=pltpu.CompilerParams(dimension_semantics=("arbitrary",), vmem_limit_bytes=VMEM_LIMIT),
        name="modulated_in_proj",
    )(*ins)


def _moe_kernel(x_ref, ya_ref, yb_ref, mod_ref, nw_ref, woa_ref, wob_ref, wr_ref, br_ref, wg_ref, wu_ref, wd_ref,
                fw_ref, o_ref, x1_ref, h_ref, comb_ref, acc_ref, *, final):
    f = pl.program_id(1)
    m = mod_ref[0]

    @pl.when(f == 0)
    def _():
        y = _bf16_dot(ya_ref[...], woa_ref[...]) + _bf16_dot(yb_ref[...], wob_ref[...])
        x1 = x_ref[...] + m[2:3] * y
        x1_ref[...] = x1
        h = _modulated(x1, nw_ref[...], m[3:4], m[4:5])
        h_hi = h.astype(BF16)
        h_ref[...] = h_hi
        h_lo = (h - h_hi.astype(F32)).astype(BF16)
        logits = (jnp.dot(jnp.concatenate([h_hi, h_lo], axis=-1), wr_ref[0:2 * D_MODEL, :], preferred_element_type=F32)
                  + jnp.dot(h_hi, wr_ref[2 * D_MODEL:3 * D_MODEL, :], preferred_element_type=F32) + br_ref[...])
        lane = lax.broadcasted_iota(jnp.int32, logits.shape, 1)
        lane_f = lane.astype(F32)
        is_group = lane < N_GROUPS
        lg = jnp.where(is_group, logits, -jnp.inf)
        g_max = jnp.max(lg, axis=-1, keepdims=True)
        g_idx = jnp.min(jnp.where(lg == g_max, lane_f, float(ROUTER_LANES)), axis=-1, keepdims=True)
        gate_g = 1.0 / jnp.sum(jnp.where(is_group, jnp.exp(logits - g_max), 0.0), axis=-1, keepdims=True)
        lane_group = ((lane + (EXP_PER_GROUP - N_GROUPS)) // EXP_PER_GROUP - 1).astype(F32)
        in_group = (lane >= N_GROUPS) & (lane < N_GROUPS + N_EXP) & (lane_group == g_idx)
        le = jnp.where(in_group, logits, -jnp.inf)
        v1 = jnp.max(le, axis=-1, keepdims=True)
        i1 = jnp.min(jnp.where(le == v1, lane_f, float(ROUTER_LANES)), axis=-1, keepdims=True)
        le2 = jnp.where(lane_f == i1, -jnp.inf, le)
        v2 = jnp.max(le2, axis=-1, keepdims=True)
        i2 = jnp.min(jnp.where(le2 == v2, lane_f, float(ROUTER_LANES)), axis=-1, keepdims=True)
        e2 = jnp.exp(v2 - v1)
        p1 = gate_g / (1.0 + e2)
        comb = jnp.where(lane_f == i1, p1, 0.0) + jnp.where(lane_f == i2, p1 * e2, 0.0)
        for grp in range(N_GROUPS):
            comb_ref[grp] = pltpu.roll(comb, ROUTER_LANES - (N_GROUPS + grp * EXP_PER_GROUP), 1)
        acc_ref[...] = jnp.zeros_like(acc_ref)

    h = h_ref[...]
    hid = jax.nn.silu(jnp.dot(h, wg_ref[...], preferred_element_type=F32)) * jnp.dot(
        h, wu_ref[...], preferred_element_type=F32)
    comb = comb_ref[f]
    hid = jnp.concatenate([hid[:, k * EXPERT_FF:(k + 1) * EXPERT_FF] * comb[:, k:k + 1]
                           for k in range(EXP_PER_GROUP)], axis=-1)
    acc_ref[...] += jnp.dot(hid.astype(BF16), wd_ref[...], preferred_element_type=F32)

    @pl.when(f == N_GROUPS - 1)
    def _():
        out = x1_ref[...] + m[5:6] * acc_ref[...]
        if final:
            out = out * lax.rsqrt(jnp.mean(out * out, axis=-1, keepdims=True) + EPS) * fw_ref[...]
        o_ref[...] = out


def moe_pallas(x, ya, yb, mod, nw, wo_a, wo_b, wr, br, wg, wu, wd, final_w, rows_per_mod, final):
    t = x.shape[0]
    tiles_per_mod = rows_per_mod // TM
    row = lambda i, f: (i, 0)
    const = lambda i, f: (0, 0)
    return pl.pallas_call(
        functools.partial(_moe_kernel, final=final),
        grid=(t // TM, N_GROUPS),
        in_specs=[pl.BlockSpec((TM, D_MODEL), row),
                  pl.BlockSpec((TM, ya.shape[1]), row),
                  pl.BlockSpec((TM, yb.shape[1]), row),
                  pl.BlockSpec((1, MOD_ROWS, D_MODEL), lambda i, f: (i // tiles_per_mod, 0, 0)),
                  pl.BlockSpec((1, D_MODEL), const),
                  pl.BlockSpec(wo_a.shape, const),
                  pl.BlockSpec(wo_b.shape, const),
                  pl.BlockSpec((3 * D_MODEL, ROUTER_LANES), const),
                  pl.BlockSpec((1, ROUTER_LANES), const),
                  pl.BlockSpec((D_MODEL, MOE_FCHUNK), lambda i, f: (0, f)),
                  pl.BlockSpec((D_MODEL, MOE_FCHUNK), lambda i, f: (0, f)),
                  pl.BlockSpec((MOE_FCHUNK, D_MODEL), lambda i, f: (f, 0)),
                  pl.BlockSpec((1, D_MODEL), const)],
        out_specs=pl.BlockSpec((TM, D_MODEL), row),
        out_shape=jax.ShapeDtypeStruct((t, D_MODEL), F32),
        scratch_shapes=[pltpu.VMEM((TM, D_MODEL), F32), pltpu.VMEM((TM, D_MODEL), BF16),
                        pltpu.VMEM((N_GROUPS, TM, ROUTER_LANES), F32), pltpu.VMEM((TM, D_MODEL), F32)],
        compiler_params=pltpu.CompilerParams(dimension_semantics=("arbitrary", "arbitrary"),
                                             vmem_limit_bytes=VMEM_LIMIT),
        name="out_proj_moe",
    )(x, ya, yb, mod, nw.reshape(1, D_MODEL), wo_a, wo_b, wr, br, wg, wu, wd, final_w.reshape(1, D_MODEL))


def _pad_cols(w, n):
    return jnp.pad(w, ((0, 0), (0, n - w.shape[1])))


def _router_weights(rg, re):
    w = _pad_cols(jnp.concatenate([rg, re], axis=1).astype(F32), ROUTER_LANES)
    hi = w.astype(BF16)
    lo = (w - hi.astype(F32)).astype(BF16)
    return jnp.concatenate([hi, hi, lo], axis=0)


def _mod_table(m):
    m = m.reshape(m.shape[0], 6, D_MODEL)
    return jnp.pad(m, ((0, 0), (0, MOD_ROWS - 6), (0, 0)))


def kernel(x_prompt, x_sample, c, c_ctx, state_dn, cache_b_k, cache_b_v, cache_c_k, cache_c_v, state_s5,
           ada_w, ada_b, norm_w, w_in_e, dn_conv_w, dn_a_log, dn_dt_bias, dn_norm_w, b_sink, w_out_e,
           w_in_o, c_q_norm, c_k_norm, s5_lam_re, s5_lam_im, s5_log_dt, s5_b_re, s5_b_im, s5_c_re, s5_c_im,
           s5_d, s5_glu_w, s5_glu_b, w_out_o, moe_rg, moe_rg_b, moe_re, moe_re_b, moe_wg, moe_wu, moe_wd,
           final_norm_w):
    rope_tabs = rope_tables()
    cvec = jnp.concatenate([c_ctx[None, :], c, jnp.zeros((ADA_ROWS - 1 - DEC_BATCH, D_MODEL), F32)], axis=0)
    mods = ada_pallas(cvec, ada_w, ada_b)
    xp = x_prompt.reshape(BATCH * SEQ, D_MODEL)
    xs = x_sample.reshape(DEC_BATCH * DEC_SEQ, D_MODEL)
    new_dn, new_bk, new_bv, new_ck, new_cv, new_s5 = [], [], [], [], [], []
    for layer in range(DEPTH):
        mod_p = _mod_table(mods[layer, 0:1])
        mod_s = _mod_table(mods[layer, 1:1 + DEC_BATCH])
        i = layer // 2
        if layer % 2 == 0:
            ep = {'conv_w': dn_conv_w[i], 'a_log': dn_a_log[i], 'dt_bias': dn_dt_bias[i],
                  'norm_w': dn_norm_w[i], 'sink': b_sink[i]}
            qkv_w, z_w, a_w, bb_w, bq_w, bk_w, bv_w = split_cols(w_in_e[i].astype(BF16), EVEN_SPLITS)
            plain_w = [qkv_w, z_w, _pad_cols(jnp.concatenate([a_w, bb_w], axis=1), LANES)]
            att_w = [bq_w, bk_w, bv_w]
            w_out = w_out_e[i].astype(BF16)
            wo_a, wo_b = w_out[:HA * DK], w_out[HA * DK:]
            proj_p = inproj_pallas(xp, mod_p, norm_w[layer, 0], plain_w, att_w, BATCH * SEQ, keep_kv=True)
            proj_s = inproj_pallas(xs, mod_s, norm_w[layer, 0], plain_w, att_w, DEC_SEQ, rope_tabs=rope_tabs)
            ya_p, yb_p, dn, bk, bv = even_mixer_context(proj_p, ep)
            ya_s, yb_s = even_mixer_latent(proj_s, ep, state_dn[:, i], cache_b_k[:, i], cache_b_v[:, i])
            new_dn.append(dn)
            new_bk.append(bk)
            new_bv.append(bv)
        else:
            op = {'q_norm': c_q_norm[i], 'k_norm': c_k_norm[i], 'lam_re': s5_lam_re[i],
                  'lam_im': s5_lam_im[i], 'log_dt': s5_log_dt[i], 'b_re': s5_b_re[i], 'b_im': s5_b_im[i],
                  'c_re': s5_c_re[i], 'c_im': s5_c_im[i], 'd': s5_d[i], 'glu_w': s5_glu_w[i],
                  'glu_b': s5_glu_b[i]}
            cq_w, ck_w, cv_w, u_w = split_cols(w_in_o[i].astype(BF16), ODD_SPLITS)
            norms = (op['q_norm'], op['k_norm'])
            w_out = w_out_o[i].astype(BF16)
            wo_a, wo_b = w_out[:HC * HEAD_DIM], w_out[HC * HEAD_DIM:]
            proj_p = inproj_pallas(xp, mod_p, norm_w[layer, 0], [u_w], [cq_w, ck_w, cv_w], BATCH * SEQ,
                                   norm_ws=norms, keep_kv=True)
            proj_s = inproj_pallas(xs, mod_s, norm_w[layer, 0], [u_w], [cq_w, ck_w, cv_w], DEC_SEQ,
                                   norm_ws=norms, rope_tabs=rope_tabs)
            ya_p, yb_p, ck, cv, s5 = odd_mixer_context(proj_p, op)
            ya_s, yb_s = odd_mixer_latent(proj_s, op, cache_c_k[:, i], cache_c_v[:, i], state_s5[:, i])
            new_ck.append(ck)
            new_cv.append(cv)
            new_s5.append(s5)
        wr = _router_weights(moe_rg[layer], moe_re[layer])
        br = _pad_cols(jnp.concatenate([moe_rg_b[layer], moe_re_b[layer]])[None, :], ROUTER_LANES)
        moe_w = (wo_a, wo_b, wr, br, moe_wg[layer].astype(BF16), moe_wu[layer].astype(BF16),
                 moe_wd[layer].astype(BF16))
        last = layer == DEPTH - 1
        xp = moe_pallas(xp, ya_p, yb_p, mod_p, norm_w[layer, 1], *moe_w, final_norm_w, BATCH * SEQ, last)
        xs = moe_pallas(xs, ya_s, yb_s, mod_s, norm_w[layer, 1], *moe_w, final_norm_w, DEC_SEQ, last)
    y_prompt = xp.reshape(x_prompt.shape)
    y_sample = xs.reshape(x_sample.shape)
    return (y_prompt, y_sample, jnp.stack(new_dn, axis=1), jnp.stack(new_bk, axis=1), jnp.stack(new_bv, axis=1),
            jnp.stack(new_ck, axis=1), jnp.stack(new_cv, axis=1), jnp.stack(new_s5, axis=1))
```
